```python
import math
import jax
import jax.numpy as jnp
from jax import lax
import numpy as np


D_MODEL = 1024
BATCH = 2
SEQ = 16384
DEPTH = 4
DEC_BATCH = 4
DEC_SEQ = 4096
PAST_LEN = 128

N_MIXERS = 4
N_META = 16
GRID_W = 64
CHUNK = 64
N_PAD = CHUNK - N_META
Q_BLOCK = 128
EPS = 1e-6
F32 = jnp.float32

A_QK_HEADS = 8
A_V_HEADS = 16
A_DK = 128
A_DV = 128
A_QK = A_QK_HEADS * A_DK
A_VW = A_V_HEADS * A_DV
A_CONV_CH = 2 * A_QK + A_VW
A_IN = A_CONV_CH + A_VW + 4 * A_V_HEADS
CONV_W = 3
B_HEADS = 8
B_KV_HEADS = 2
B_GROUP = B_HEADS // B_KV_HEADS
B_DH = 128
B_QW = B_HEADS * B_DH
B_KVW = B_KV_HEADS * B_DH
B_IN = 2 * B_QW + 2 * B_KVW
ROPE_AXIS = B_DH // 2
ROPE_THETA = 10000.0
C_HEADS = 8
C_DQK = 64
C_DV = 128
C_QKW = C_HEADS * 2 * C_DQK
C_VW = C_HEADS * C_DV
C_IN = 2 * C_QKW + 2 * C_VW
REL_BUCKETS = 32
REL_MAX_DIST = 128
D_HEADS = 4
D_DK = 128
D_DV = 256
D_QKW = D_HEADS * D_DK
D_VW = D_HEADS * D_DV
GATE_RANK = 16
GATE_TAU = 16.0
D_IN = 2 * D_QKW + 2 * D_VW + 2 * GATE_RANK

N_LAYERS_A = (DEPTH + 3) // 4
N_LAYERS_B = (DEPTH + 2) // 4
N_LAYERS_C = (DEPTH + 1) // 4
N_LAYERS_D = DEPTH // 4

kernel_name = 'hybrid_bidir_encoder_gdn_gqa_diff_gla'


def rms_norm(x, g):
    xf = x.astype(F32)
    y = xf * lax.rsqrt(jnp.mean(xf * xf, axis=-1, keepdims=True) + EPS)
    return (y * g.astype(F32)).astype(x.dtype)


def l2_norm(x):
    xf = x.astype(F32)
    return (xf * lax.rsqrt(jnp.sum(xf * xf, axis=-1, keepdims=True) + EPS)).astype(x.dtype)


def pad_front(x):
    return jnp.pad(x, [(0, 0), (N_PAD, 0)] + [(0, 0)] * (x.ndim - 2))


def pad_axis_end(x, axis, n):
    widths = [(0, 0)] * x.ndim
    widths[axis] = (0, n)
    return jnp.pad(x, widths)


def flip_t(x):
    return jnp.flip(x, axis=1)


def to_chunks(x):
    b, t, h, d = x.shape
    return x.reshape(b, t // CHUNK, CHUNK, h, d).transpose(1, 0, 3, 2, 4)


def from_chunks(x):
    n, b, h, c, d = x.shape
    return x.transpose(1, 0, 3, 2, 4).reshape(b, n * c, h, d)


def depthwise_conv(x, w):
    half = CONV_W // 2
    return lax.conv_general_dilated(x, w[:, None, :].astype(x.dtype), window_strides=(1,),
                                    padding=[(half, half)], dimension_numbers=('NWC', 'WIO', 'NWC'),
                                    feature_group_count=x.shape[-1])


def delta_scan(q, k, v, g, beta):
    qc, kc, vc = to_chunks(q), to_chunks(k), to_chunks(v)
    gc = to_chunks(g[..., None])[..., 0]
    bc = to_chunks(beta[..., None])[..., 0]
    _, bsz, h, c, dk = kc.shape
    dv = vc.shape[-1]
    tri = jnp.tril(jnp.ones((c, c), bool))
    strict = jnp.tril(jnp.ones((c, c), bool), -1)
    eye = jnp.eye(c, dtype=F32)

    def step(s, inp):
        q_i, k_i, v_i, g_i, b_i = inp
        gcum = jnp.cumsum(g_i, axis=-1)
        decay = jnp.exp(jnp.where(tri, gcum[..., :, None] - gcum[..., None, :], -jnp.inf))
        kb = k_i * b_i[..., None]
        a_mat = jnp.where(strict, jnp.einsum('bhid,bhjd->bhij', kb, k_i) * decay, 0.0)
        t_mat = lax.linalg.triangular_solve(eye + a_mat, jnp.broadcast_to(eye, a_mat.shape),
                                            left_side=True, lower=True, unit_diagonal=True)
        u = t_mat @ (v_i * b_i[..., None])
        w = t_mat @ (kb * jnp.exp(gcum)[..., None])
        v_new = u - w @ s
        o = (q_i * jnp.exp(gcum)[..., None]) @ s + (jnp.einsum('bhid,bhjd->bhij', q_i, k_i) * decay) @ v_new
        s = s * jnp.exp(gcum[..., -1])[..., None, None] + jnp.einsum(
            'bhcd,bhce->bhde', k_i * jnp.exp(gcum[..., -1:] - gcum)[..., None], v_new)
        return s, o

    _, o = lax.scan(step, jnp.zeros((bsz, h, dk, dv), F32), (qc, kc, vc, gc, bc))
    return from_chunks(o)


def gla_scan(q, k, v, lg):
    qc, kc, vc, gc = to_chunks(q), to_chunks(k), to_chunks(v), to_chunks(lg)
    _, bsz, h, c, dk = kc.shape
    dv = vc.shape[-1]
    tri = jnp.tril(jnp.ones((c, c), bool))[..., None]

    def step(s, inp):
        q_i, k_i, v_i, g_i = inp
        bcum = jnp.cumsum(g_i, axis=2)
        decay = jnp.exp(jnp.where(tri, bcum[:, :, :, None, :] - bcum[:, :, None, :, :], -jnp.inf))
        a_mat = jnp.einsum('bhid,bhjd,bhijd->bhij', q_i, k_i, decay)
        o = jnp.einsum('bhcd,bhde->bhce', q_i * jnp.exp(bcum), s) + a_mat @ v_i
        s = s * jnp.exp(bcum[:, :, -1])[..., None] + jnp.einsum(
            'bhcd,bhce->bhde', k_i * jnp.exp(bcum[:, :, -1:] - bcum), v_i)
        return s, o

    _, o = lax.scan(step, jnp.zeros((bsz, h, dk, dv), F32), (qc, kc, vc, gc))
    return from_chunks(o)


def gdn_mixer(u, w_in, conv_w, a_log, dt_bias, out_g):
    bsz, L, _ = u.shape
    qkv, z, ab = jnp.split(u @ w_in, [A_CONV_CH, A_CONV_CH + A_VW], axis=-1)
    qkv = jax.nn.silu(depthwise_conv(qkv, conv_w))
    q, k, v = jnp.split(qkv, [A_QK, 2 * A_QK], axis=-1)
    rep = A_V_HEADS // A_QK_HEADS
    q = jnp.repeat(l2_norm(q.reshape(bsz, L, A_QK_HEADS, A_DK).astype(F32)), rep, axis=2) * A_DK ** -0.5
    k = jnp.repeat(l2_norm(k.reshape(bsz, L, A_QK_HEADS, A_DK).astype(F32)), rep, axis=2)
    v = v.reshape(bsz, L, A_V_HEADS, A_DV).astype(F32)
    ab = ab.astype(F32).reshape(bsz, L, 2, 2, A_V_HEADS)
    g = -jnp.exp(a_log.astype(F32)) * jax.nn.softplus(ab[:, :, 0] + dt_bias.astype(F32))
    beta = jax.nn.sigmoid(ab[:, :, 1])
    q, k, v, g, beta = (pad_front(t) for t in (q, k, v, g, beta))
    o = delta_scan(q, k, v, g[:, :, 0], beta[:, :, 0]) + flip_t(delta_scan(
        flip_t(q), flip_t(k), flip_t(v), flip_t(g[:, :, 1]), flip_t(beta[:, :, 1])))
    o = rms_norm(o[:, N_PAD:], out_g)
    return (o.reshape(bsz, L, A_VW) * jax.nn.silu(z.astype(F32))).astype(u.dtype)


def axial_rope(n_tok):
    rows = n_tok // GRID_W
    row = jnp.repeat(jnp.arange(rows, dtype=F32), GRID_W)
    col = (jnp.arange(rows * GRID_W) % GRID_W).astype(F32)
    zeros = jnp.zeros((N_META,), F32)
    row = jnp.concatenate([zeros, row])
    col = jnp.concatenate([zeros, col])
    inv = ROPE_THETA ** (-jnp.arange(0, ROPE_AXIS, 2, dtype=F32) / ROPE_AXIS)
    ang = jnp.concatenate([row[:, None] * inv, col[:, None] * inv], axis=-1)
    return jnp.cos(ang), jnp.sin(ang)


def apply_rope(x, cos, sin):
    xf = x.astype(F32)
    half = xf.shape[-1] // 2
    x1, x2 = xf[..., :half], xf[..., half:]
    c, s = cos[None, :, None, :], sin[None, :, None, :]
    return jnp.concatenate([x1 * c - x2 * s, x2 * c + x1 * s], axis=-1).astype(x.dtype)


def gqa_mixer(u, w_in, q_g, k_g, cos, sin):
    bsz, L, _ = u.shape
    q, k, v, z = jnp.split(u @ w_in, [B_QW, B_QW + B_KVW, B_QW + 2 * B_KVW], axis=-1)
    q = apply_rope(rms_norm(q.reshape(bsz, L, B_HEADS, B_DH), q_g), cos, sin)
    k = apply_rope(rms_norm(k.reshape(bsz, L, B_KV_HEADS, B_DH), k_g), cos, sin)
    kt = k.transpose(0, 2, 1, 3)
    vt = v.reshape(bsz, L, B_KV_HEADS, B_DH).transpose(0, 2, 1, 3)
    qg = q.reshape(bsz, L, B_KV_HEADS, B_GROUP, B_DH).transpose(0, 2, 3, 1, 4)
    n_blk = -(-L // Q_BLOCK)
    q_blocks = jnp.moveaxis(pad_axis_end(qg, 3, n_blk * Q_BLOCK - L).reshape(
        bsz, B_KV_HEADS, B_GROUP, n_blk, Q_BLOCK, B_DH), 3, 0)
    scale = B_DH ** -0.5

    def block(qb):
        s = jnp.einsum('bkgqd,bksd->bkgqs', qb, kt, preferred_element_type=F32) * scale
        p = jax.nn.softmax(s, axis=-1)
        return jnp.einsum('bkgqs,bksd->bkgqd', p.astype(vt.dtype), vt)

    o = lax.map(block, q_blocks)
    o = o.transpose(1, 0, 4, 2, 3, 5).reshape(bsz, n_blk * Q_BLOCK, B_QW)[:, :L]
    return (o.astype(F32) * jax.nn.silu(z.astype(F32))).astype(u.dtype)


def t5_bucket(rel):
    nb = REL_BUCKETS // 2
    max_exact = nb // 2
    ret = jnp.where(rel > 0, nb, 0)
    n = jnp.abs(rel)
    nf = jnp.maximum(n, 1).astype(F32)
    large = max_exact + (jnp.log(nf / max_exact) / math.log(REL_MAX_DIST / max_exact)
                         * (nb - max_exact)).astype(jnp.int32)
    large = jnp.minimum(large, nb - 1)
    return ret + jnp.where(n < max_exact, n, large)


def diff_mixer(u, w_in, q_g, k_g, lam, sub_g, rel_bias, layer_idx):
    bsz, L, _ = u.shape
    q, k, v, z = jnp.split(u @ w_in, [C_QKW, 2 * C_QKW, 2 * C_QKW + C_VW], axis=-1)
    q = rms_norm(q.reshape(bsz, L, C_HEADS, 2, C_DQK), q_g).transpose(0, 2, 3, 1, 4)
    k = rms_norm(k.reshape(bsz, L, C_HEADS, 2, C_DQK), k_g).transpose(0, 2, 3, 1, 4)
    v = v.reshape(bsz, L, C_HEADS, C_DV).transpose(0, 2, 1, 3)
    lam = lam.astype(F32)
    lam_init = 0.8 - 0.6 * math.exp(-0.3 * layer_idx)
    lam_full = jnp.exp(jnp.sum(lam[0] * lam[1])) - jnp.exp(jnp.sum(lam[2] * lam[3])) + lam_init
    n_blk = -(-L // Q_BLOCK)
    q_blocks = jnp.moveaxis(pad_axis_end(q, 3, n_blk * Q_BLOCK - L).reshape(
        bsz, C_HEADS, 2, n_blk, Q_BLOCK, C_DQK), 3, 0)
    starts = jnp.arange(n_blk, dtype=jnp.int32) * Q_BLOCK
    k_pos = jnp.arange(L, dtype=jnp.int32)
    table = rel_bias.astype(F32)
    scale = C_DQK ** -0.5

    def block(args):
        qb, start = args
        q_pos = start + jnp.arange(Q_BLOCK, dtype=jnp.int32)
        bias = jnp.moveaxis(table[t5_bucket(k_pos[None, :] - q_pos[:, None])], -1, 0)
        s = jnp.einsum('bhmqd,bhmkd->bhmqk', qb, k, preferred_element_type=F32) * scale + bias[None, :, None]
        p = jax.nn.softmax(s, axis=-1)
        a = p[:, :, 0] - lam_full * p[:, :, 1]
        return jnp.einsum('bhqk,bhkd->bhqd', a.astype(v.dtype), v)

    o = lax.map(block, (q_blocks, starts))
    o = o.transpose(1, 0, 3, 2, 4).reshape(bsz, n_blk * Q_BLOCK, C_HEADS, C_DV)[:, :L]
    o = rms_norm(o, sub_g).astype(F32) * (1.0 - lam_init)
    return (o.reshape(bsz, L, C_VW) * jax.nn.silu(z.astype(F32))).astype(u.dtype)


def gla_mixer(u, w_in, gate_w2, gate_b, out_g):
    bsz, L, _ = u.shape
    q, k, v, z, lr = jnp.split(u @ w_in, [D_QKW, 2 * D_QKW, 2 * D_QKW + D_VW, 2 * D_QKW + 2 * D_VW], axis=-1)
    q = q.reshape(bsz, L, D_HEADS, D_DK).astype(F32) * D_DK ** -0.5
    k = k.reshape(bsz, L, D_HEADS, D_DK).astype(F32)
    v = v.reshape(bsz, L, D_HEADS, D_DV).astype(F32)
    logit = jnp.einsum('blnr,nrk->blnk', lr.reshape(bsz, L, 2, GATE_RANK), gate_w2) + gate_b
    lg = (jax.nn.log_sigmoid(logit.astype(F32)) / GATE_TAU).reshape(bsz, L, 2, D_HEADS, D_DK)
    q, k, v, lg = (pad_front(t) for t in (q, k, v, lg))
    o = gla_scan(q, k, v, lg[:, :, 0]) + flip_t(gla_scan(flip_t(q), flip_t(k), flip_t(v), flip_t(lg[:, :, 1])))
    o = rms_norm(o[:, N_PAD:], out_g)
    return (o.reshape(bsz, L, D_VW) * jax.nn.silu(z.astype(F32))).astype(u.dtype)


def trunk(x, meta_tokens, rel_bias, a_norm, a_w_in, a_conv, a_a_log, a_dt_bias, a_out_norm, a_w_out,
          b_norm, b_w_in, b_q_norm, b_k_norm, b_w_out,
          c_norm, c_w_in, c_q_norm, c_k_norm, c_lambda, c_sub_norm, c_w_out,
          d_norm, d_w_in, d_gate_w2, d_gate_b, d_out_norm, d_w_out):
    bsz, n_tok, _ = x.shape
    meta = jnp.broadcast_to(meta_tokens.astype(x.dtype)[None], (bsz, N_META, D_MODEL))
    h = jnp.concatenate([meta, x], axis=1)
    cos, sin = axial_rope(n_tok)
    for i in range(DEPTH):
        m, j = i % N_MIXERS, i // N_MIXERS
        if m == 0:
            y = gdn_mixer(rms_norm(h, a_norm[j]), a_w_in[j], a_conv[j], a_a_log[j], a_dt_bias[j],
                          a_out_norm[j]) @ a_w_out[j]
        elif m == 1:
            y = gqa_mixer(rms_norm(h, b_norm[j]), b_w_in[j], b_q_norm[j], b_k_norm[j], cos, sin) @ b_w_out[j]
        elif m == 2:
            y = diff_mixer(rms_norm(h, c_norm[j]), c_w_in[j], c_q_norm[j], c_k_norm[j], c_lambda[j],
                           c_sub_norm[j], rel_bias, i) @ c_w_out[j]
        else:
            y = gla_mixer(rms_norm(h, d_norm[j]), d_w_in[j], d_gate_w2[j], d_gate_b[j], d_out_norm[j]) @ d_w_out[j]
        h = h + y.astype(h.dtype)
    return h[:, N_META:]


def setup_inputs(seed: int = 0) -> dict:
    key = jax.random.key(seed)
    ks = list(jax.random.split(key, 32))

    def nrm(idx, shape, scale):
        return scale * jax.random.normal(ks[idx], shape, F32)

    def gain(idx, shape):
        return 1.0 + nrm(idx, shape, 0.02)

    dt = jnp.exp(jax.random.uniform(ks[7], (N_LAYERS_A, 2, A_V_HEADS), F32, math.log(1e-3), math.log(1e-1)))
    return {
        'x_prompt': nrm(0, (BATCH, SEQ, D_MODEL), 1.0),
        'x_sample': nrm(1, (DEC_BATCH, DEC_SEQ, D_MODEL), 1.0),
        'meta_tokens': nrm(2, (N_META, D_MODEL), 1.0),
        'rel_bias': nrm(3, (REL_BUCKETS, C_HEADS), 0.2),
        'a_norm': gain(4, (N_LAYERS_A, D_MODEL)),
        'a_w_in': nrm(5, (N_LAYERS_A, D_MODEL, A_IN), D_MODEL ** -0.5),
        'a_conv': nrm(6, (N_LAYERS_A, CONV_W, A_CONV_CH), CONV_W ** -0.5),
        'a_a_log': jnp.log(jax.random.uniform(ks[8], (N_LAYERS_A, 2, A_V_HEADS), F32, 1.0, 16.0)),
        'a_dt_bias': dt + jnp.log(-jnp.expm1(-dt)),
        'a_out_norm': gain(9, (N_LAYERS_A, A_DV)),
        'a_w_out': nrm(10, (N_LAYERS_A, A_VW, D_MODEL), A_VW ** -0.5),
        'b_norm': gain(11, (N_LAYERS_B, D_MODEL)),
        'b_w_in': nrm(12, (N_LAYERS_B, D_MODEL, B_IN), D_MODEL ** -0.5),
        'b_q_norm': gain(13, (N_LAYERS_B, B_DH)),
        'b_k_norm': gain(14, (N_LAYERS_B, B_DH)),
        'b_w_out': nrm(15, (N_LAYERS_B, B_QW, D_MODEL), B_QW ** -0.5),
        'c_norm': gain(16, (N_LAYERS_C, D_MODEL)),
        'c_w_in': nrm(17, (N_LAYERS_C, D_MODEL, C_IN), D_MODEL ** -0.5),
        'c_q_norm': gain(18, (N_LAYERS_C, C_DQK)),
        'c_k_norm': gain(19, (N_LAYERS_C, C_DQK)),
        'c_lambda': nrm(20, (N_LAYERS_C, 4, C_DQK), 0.1),
        'c_sub_norm': gain(21, (N_LAYERS_C, C_DV)),
        'c_w_out': nrm(22, (N_LAYERS_C, C_VW, D_MODEL), C_VW ** -0.5),
        'd_norm': gain(23, (N_LAYERS_D, D_MODEL)),
        'd_w_in': nrm(24, (N_LAYERS_D, D_MODEL, D_IN), D_MODEL ** -0.5),
        'd_gate_w2': nrm(25, (N_LAYERS_D, 2, GATE_RANK, D_QKW), GATE_RANK ** -0.5),
        'd_gate_b': nrm(26, (N_LAYERS_D, 2, D_QKW), 0.1),
        'd_out_norm': gain(27, (N_LAYERS_D, D_DV)),
        'd_w_out': nrm(28, (N_LAYERS_D, D_VW, D_MODEL), D_VW ** -0.5),
    }


def reference(x_prompt, x_sample, meta_tokens, rel_bias, a_norm, a_w_in, a_conv, a_a_log, a_dt_bias,
              a_out_norm, a_w_out, b_norm, b_w_in, b_q_norm, b_k_norm, b_w_out,
              c_norm, c_w_in, c_q_norm, c_k_norm, c_lambda, c_sub_norm, c_w_out,
              d_norm, d_w_in, d_gate_w2, d_gate_b, d_out_norm, d_w_out):
    weights = (meta_tokens, rel_bias, a_norm, a_w_in, a_conv, a_a_log, a_dt_bias, a_out_norm, a_w_out,
               b_norm, b_w_in, b_q_norm, b_k_norm, b_w_out,
               c_norm, c_w_in, c_q_norm, c_k_norm, c_lambda, c_sub_norm, c_w_out,
               d_norm, d_w_in, d_gate_w2, d_gate_b, d_out_norm, d_w_out)
    y_prompt = trunk(x_prompt, *weights)
    y_sample = trunk(x_sample, *weights)
    return (y_prompt, y_sample)
```

```python
import functools
import math

import numpy as np
import jax
import jax.numpy as jnp
from jax import lax
from jax.experimental import pallas as pl
from jax.experimental.pallas import tpu as pltpu

F32 = jnp.float32
BF16 = jnp.bfloat16

D_MODEL = 1024
N_META = 16
CHUNK = 64
GRID_W = 64
EPS = 1e-6
LANES = 128
VMEM_LIMIT = 56 * 1024 * 1024
NEG = -1e30

A_QK_HEADS, A_V_HEADS, A_DK, A_DV = 8, 16, 128, 128
A_QK, A_VW = A_QK_HEADS * A_DK, A_V_HEADS * A_DV
A_CONV_CH = 2 * A_QK + A_VW
B_HEADS, B_KV_HEADS, B_DH = 8, 2, 128
B_GROUP = B_HEADS // B_KV_HEADS
B_QW, B_KVW = B_HEADS * B_DH, B_KV_HEADS * B_DH
ROPE_AXIS = B_DH // 2
ROPE_THETA = 10000.0
C_HEADS, C_DQK, C_DV = 8, 64, 128
C_QKW, C_VW = C_HEADS * 2 * C_DQK, C_HEADS * C_DV
REL_BUCKETS, REL_MAX_DIST = 32, 128
D_HEADS, D_DK, D_DV = 4, 128, 256
D_QKW, D_VW = D_HEADS * D_DK, D_HEADS * D_DV
GATE_RANK, GATE_TAU = 16, 16.0


def _cparams(sem):
    return pltpu.CompilerParams(dimension_semantics=sem, vmem_limit_bytes=VMEM_LIMIT)


def _sigmoid(x):
    return 1.0 / (1.0 + jnp.exp(-x))


def _dot(a, b):
    return jnp.dot(a.astype(BF16), b.astype(BF16), preferred_element_type=F32)


def _dot_nt(a, b):
    return lax.dot_general(a.astype(BF16), b.astype(BF16), (((1,), (1,)), ((), ())),
                           preferred_element_type=F32)


def _dot_tn(a, b):
    return lax.dot_general(a.astype(BF16), b.astype(BF16), (((0,), (0,)), ((), ())),
                           preferred_element_type=F32)


def _split3(x):
    x1 = x.astype(BF16)
    r1 = x - x1.astype(F32)
    x2 = r1.astype(BF16)
    x3 = (r1 - x2.astype(F32)).astype(BF16)
    return x1, x2, x3


def _dot_exact_lhs(m01, x):
    m = m01.astype(BF16)
    x1, x2, x3 = _split3(x)
    return (jnp.dot(m, x1, preferred_element_type=F32) + jnp.dot(m, x2, preferred_element_type=F32)
            + jnp.dot(m, x3, preferred_element_type=F32))


def _dot_exact_rhs(x, m01):
    m = m01.astype(BF16)
    x1, x2, x3 = _split3(x)
    return (jnp.dot(x1, m, preferred_element_type=F32) + jnp.dot(x2, m, preferred_element_type=F32)
            + jnp.dot(x3, m, preferred_element_type=F32))


def _norm_proj_kernel(x_ref, g_ref, w_ref, o_ref, xn_ref):
    @pl.when(pl.program_id(1) == 0)
    def _():
        x = x_ref[...]
        ms = jnp.mean(x * x, axis=-1, keepdims=True)
        xn_ref[...] = (x * lax.rsqrt(ms + EPS) * g_ref[...]).astype(BF16)

    o_ref[...] = jnp.dot(xn_ref[...], w_ref[...], preferred_element_type=F32)


def _norm_proj(x, g, w):
    m, k = x.shape
    n = w.shape[1]
    tm = 512 if m % 512 == 0 else m
    tn = 512 if n % 512 == 0 else n
    return pl.pallas_call(
        _norm_proj_kernel,
        grid=(m // tm, n // tn),
        in_specs=[pl.BlockSpec((tm, k), lambda i, j: (i, 0)),
                  pl.BlockSpec((1, k), lambda i, j: (0, 0)),
                  pl.BlockSpec((k, tn), lambda i, j: (0, j))],
        out_specs=pl.BlockSpec((tm, tn), lambda i, j: (i, j)),
        out_shape=jax.ShapeDtypeStruct((m, n), F32),
        scratch_shapes=[pltpu.VMEM((tm, k), BF16)],
        compiler_params=_cparams(("parallel", "arbitrary")),
        name="norm_proj",
    )(x, g.reshape(1, k).astype(F32), w.astype(BF16))


def _out_proj_kernel(*refs, n_o, hw):
    o_refs = refs[:n_o]
    z_ref, g_ref, w_ref, h_ref, out_ref = refs[n_o:]
    k = z_ref.shape[1]
    cw = hw if hw else 256
    acc = h_ref[...]
    for c0 in range(0, k, cw):
        o = o_refs[0][:, c0:c0 + cw]
        for r in o_refs[1:]:
            o = o + r[:, c0:c0 + cw]
        if hw:
            ms = jnp.mean(o * o, axis=-1, keepdims=True)
            o = o * lax.rsqrt(ms + EPS) * g_ref[:, c0:c0 + cw]
        z = z_ref[:, c0:c0 + cw]
        gated = o * (z * _sigmoid(z))
        acc = acc + jnp.dot(gated.astype(BF16), w_ref[c0:c0 + cw, :], preferred_element_type=F32)
    out_ref[...] = acc


def _out_proj(os, z, g, w, h, hw):
    m, k = z.shape
    n = w.shape[1]
    tm = 256 if m % 256 == 0 else m
    row = lambda i: (i, 0)
    fixed = lambda i: (0, 0)
    return pl.pallas_call(
        functools.partial(_out_proj_kernel, n_o=len(os), hw=hw),
        grid=(m // tm,),
        in_specs=[pl.BlockSpec((tm, k), row) for _ in os] + [
            pl.BlockSpec((tm, k), row), pl.BlockSpec((1, k), fixed),
            pl.BlockSpec((k, n), fixed), pl.BlockSpec((tm, n), row)],
        out_specs=pl.BlockSpec((tm, n), row),
        out_shape=jax.ShapeDtypeStruct((m, n), F32),
        compiler_params=_cparams(("parallel",)),
        name="out_proj",
    )(*os, z, g.reshape(1, k).astype(F32), w.astype(BF16), h)


def _softmax_update(s, c, v, m_scr, l_scr, acc_scr):
    m_prev = m_scr[...]
    m_new = jnp.maximum(m_prev, jnp.max(s, axis=-1, keepdims=True) + c)
    alpha = jnp.exp(m_prev - m_new)
    p = jnp.exp(s - (m_new - c))
    l_scr[...] = alpha * l_scr[...] + jnp.sum(p, axis=-1, keepdims=True)
    acc_scr[...] = alpha * acc_scr[...] + jnp.dot(p.astype(BF16), v, preferred_element_type=F32)
    m_scr[...] = m_new


def _attn_init(m_scr, l_scr, acc_scr):
    m_scr[...] = jnp.full(m_scr.shape, NEG, F32)
    l_scr[...] = jnp.zeros(l_scr.shape, F32)
    acc_scr[...] = jnp.zeros(acc_scr.shape, F32)


def _gqa_kernel(q_ref, k_ref, v_ref, km_ref, vm_ref, buf_ref, o_ref, m_scr, l_scr, acc_scr):
    del buf_ref
    ki = pl.program_id(3)
    tq = q_ref.shape[0]
    q = jnp.concatenate([q_ref[:, g * B_DH:(g + 1) * B_DH] for g in range(B_GROUP)], axis=0)

    @pl.when(ki == 0)
    def _():
        _attn_init(m_scr, l_scr, acc_scr)
        lane = lax.broadcasted_iota(jnp.int32, (1, LANES), 1)
        pad = jnp.where(lane < N_META, 0.0, NEG)
        _softmax_update(_dot_nt(q, km_ref[0]) + pad, 0.0, vm_ref[0], m_scr, l_scr, acc_scr)

    _softmax_update(_dot_nt(q, k_ref[...]), 0.0, v_ref[...], m_scr, l_scr, acc_scr)

    @pl.when(ki == pl.num_programs(3) - 1)
    def _():
        o = acc_scr[...] / l_scr[...]
        for g in range(B_GROUP):
            o_ref[:, g * B_DH:(g + 1) * B_DH] = o[g * tq:(g + 1) * tq]


def _gqa_attention(q, k, v, km, vm, buf, n_seq, n_q, n_k, q_row0, k_row0, tq, tk):
    nq, nk = n_q // tq, n_k // tk
    qb0, kb0 = q_row0 // tq, k_row0 // tk
    r = B_GROUP * tq
    return pl.pallas_call(
        _gqa_kernel,
        grid=(n_seq, B_KV_HEADS, nq, nk),
        in_specs=[pl.BlockSpec((tq, B_GROUP * B_DH), lambda s, h, i, j: (qb0 + s * nq + i, h)),
                  pl.BlockSpec((tk, B_DH), lambda s, h, i, j: (kb0 + s * nk + j, h)),
                  pl.BlockSpec((tk, B_DH), lambda s, h, i, j: (kb0 + s * nk + j, h)),
                  pl.BlockSpec((1, LANES, B_DH), lambda s, h, i, j: (s, 0, h)),
                  pl.BlockSpec((1, LANES, B_DH), lambda s, h, i, j: (s, 0, h)),
                  pl.BlockSpec(memory_space=pl.ANY)],
        out_specs=pl.BlockSpec((tq, B_GROUP * B_DH), lambda s, h, i, j: (qb0 + s * nq + i, h)),
        out_shape=jax.ShapeDtypeStruct(buf.shape, F32),
        scratch_shapes=[pltpu.VMEM((r, 1), F32), pltpu.VMEM((r, 1), F32), pltpu.VMEM((r, B_DH), F32)],
        input_output_aliases={5: 0},
        compiler_params=_cparams(("parallel", "parallel", "parallel", "arbitrary")),
        name="gqa_attention",
    )(q, k, v, km, vm, buf)


def _diff_kernel(q_ref, k_ref, v_ref, km_ref, vm_ref, tt_ref, bm_ref, cst_ref, buf_ref, o_ref,
                 m_scr, l_scr, acc_scr, *, meta_q):
    del buf_ref
    qi, ki = pl.program_id(2), pl.program_id(3)
    tq, tk = q_ref.shape[0], k_ref.shape[0]
    nt = tk // LANES
    q = q_ref[...]
    lane = lax.broadcasted_iota(jnp.int32, q.shape, 1)
    zero = jnp.zeros_like(q)
    qq = jnp.concatenate([jnp.where(lane < C_DQK, q, zero), jnp.where(lane >= C_DQK, q, zero)], axis=0)
    c_left = cst_ref[0, 0:1, :]
    c_right = cst_ref[0, 1:2, :]
    lam = cst_ref[0, 2:3, 0:1]

    @pl.when(ki == 0)
    def _():
        _attn_init(m_scr, l_scr, acc_scr)
        bm = bm_ref[0, 0]
        s = _dot_nt(qq, km_ref[0]) + jnp.concatenate([bm, bm], axis=0)
        _softmax_update(s, 0.0, vm_ref[0], m_scr, l_scr, acc_scr)

    s = _dot_nt(qq, k_ref[...])
    v = v_ref[...]

    def tile_bias(off, rows):
        if off <= -2:
            return jnp.broadcast_to(c_left, (rows, LANES))
        if off >= 2:
            return jnp.broadcast_to(c_right, (rows, LANES))
        return tt_ref[0, off + 1]

    def add_bias(d):
        if meta_q:
            rows = [[tt_ref[0, 0]] + [tile_bias(2, tq)] * (nt - 1)]
        else:
            rows = [[tile_bias(d * nt + b - a, LANES) for b in range(nt)] for a in range(tq // LANES)]
        bias = jnp.concatenate([jnp.concatenate(r, axis=1) for r in rows], axis=0)
        return s + jnp.concatenate([bias, bias], axis=0)

    d = ki - qi
    specials = (0,) if meta_q else (-1, 0, 1)
    for dd in specials:
        @pl.when(d == dd)
        def _(dd=dd):
            _softmax_update(add_bias(dd), 0.0, v, m_scr, l_scr, acc_scr)

    @pl.when((d < specials[0]) | (d > specials[-1]))
    def _():
        c = jnp.where(d < 0, c_left[:, 0:1], c_right[:, 0:1])
        _softmax_update(s, c, v, m_scr, l_scr, acc_scr)

    @pl.when(ki == pl.num_programs(3) - 1)
    def _():
        o = acc_scr[...] / l_scr[...]
        o_ref[...] = o[:tq] - lam * o[tq:]


def _diff_attention(q, k, v, km, vm, tt, bm, cst, buf, n_seq, n_q, n_k, q_row0, k_row0, tq, tk, meta_q):
    nq, nk = n_q // tq, n_k // tk
    qb0, kb0 = q_row0 // tq, k_row0 // tk
    r = 2 * tq
    tr = tt.shape[2]
    return pl.pallas_call(
        functools.partial(_diff_kernel, meta_q=meta_q),
        grid=(n_seq, C_HEADS, nq, nk),
        in_specs=[pl.BlockSpec((tq, LANES), lambda s, h, i, j: (qb0 + s * nq + i, h)),
                  pl.BlockSpec((tk, LANES), lambda s, h, i, j: (kb0 + s * nk + j, h)),
                  pl.BlockSpec((tk, C_DV), lambda s, h, i, j: (kb0 + s * nk + j, h)),
                  pl.BlockSpec((1, LANES, LANES), lambda s, h, i, j: (s, 0, h)),
                  pl.BlockSpec((1, LANES, C_DV), lambda s, h, i, j: (s, 0, h)),
                  pl.BlockSpec((1, 3, tr, LANES), lambda s, h, i, j: (h, 0, 0, 0)),
                  pl.BlockSpec((1, 1, tq, LANES), lambda s, h, i, j: (h, jnp.minimum(i, 1), 0, 0)),
                  pl.BlockSpec((1, 8, LANES), lambda s, h, i, j: (h, 0, 0)),
                  pl.BlockSpec(memory_space=pl.ANY)],
        out_specs=pl.BlockSpec((tq, C_DV), lambda s, h, i, j: (qb0 + s * nq + i, h)),
        out_shape=jax.ShapeDtypeStruct(buf.shape, F32),
        scratch_shapes=[pltpu.VMEM((r, 1), F32), pltpu.VMEM((r, 1), F32), pltpu.VMEM((r, C_DV), F32)],
        input_output_aliases={8: 0},
        compiler_params=_cparams(("parallel", "parallel", "parallel", "arbitrary")),
        name="diff_attention",
    )(q, k, v, km, vm, tt, bm, cst, buf)


def _tri_masks(rev):
    i = lax.broadcasted_iota(jnp.int32, (CHUNK, CHUNK), 0)
    j = lax.broadcasted_iota(jnp.int32, (CHUNK, CHUNK), 1)
    if rev:
        return j >= i, j > i, i, j
    return j <= i, j < i, i, j


def _unit_lower_inverse(a):
    i = lax.broadcasted_iota(jnp.int32, a.shape, 0)
    j = lax.broadcasted_iota(jnp.int32, a.shape, 1)
    x = jnp.where(i == j, 1.0, 0.0) - a
    p = a
    for _ in range(5):
        p = _dot(p, p)
        x = x + _dot(x, p)
    return x


def _gdn_unit(q, k, kk, qk, v, gcol, grow, beta, s, rev):
    tri, strict, _, _ = _tri_masks(rev)
    cum_mat = jnp.where(tri, 1.0, 0.0)
    gc = _dot_exact_lhs(cum_mat, jnp.broadcast_to(gcol, (CHUNK, LANES)))
    gr = _dot_exact_rhs(jnp.broadcast_to(grow, (8, CHUNK)), _cum_mat_t(rev))[0:1, :]
    decay = jnp.exp(jnp.where(tri, gc[:, :CHUNK] - gr, -jnp.inf))
    a = jnp.where(strict, beta * kk * decay, 0.0)
    t = _unit_lower_inverse(a)
    egc = jnp.exp(gc)
    rhs = jnp.concatenate([v * beta, k * (beta * egc)], axis=1)
    uw = _dot(t, rhs)
    u, w = uw[:, :A_DV], uw[:, A_DV:]
    r2 = _dot(jnp.concatenate([w, q * egc], axis=0), s)
    v_new = u - r2[:CHUNK]
    o = r2[CHUNK:] + _dot(qk * decay, v_new)
    gtot = gc[0:1, :] if rev else gc[CHUNK - 1:CHUNK, :]
    kd = k * jnp.exp(gtot - gc)
    s_new = s * jnp.exp(gtot) + _dot_tn(kd, v_new)
    return o, s_new


def _cum_mat_t(rev):
    t = lax.broadcasted_iota(jnp.int32, (CHUNK, CHUNK), 0)
    j = lax.broadcasted_iota(jnp.int32, (CHUNK, CHUNK), 1)
    return jnp.where((t >= j) if rev else (t <= j), 1.0, 0.0)


def _gdn_kernel(qf_ref, kf_ref, vf_ref, cf_ref, rf_ref, qb_ref, kb_ref, vb_ref, cb_ref, rb_ref,
                s0f_ref, s0b_ref, bf_buf, bb_buf, of_ref, ob_ref, sff_ref, sfb_ref, s_scr):
    del bf_buf, bb_buf
    c = pl.program_id(2)

    @pl.when(c == 0)
    def _():
        s_scr[0] = s0f_ref[0, 0]
        s_scr[1] = s0b_ref[0, 0]

    streams = ((qf_ref, kf_ref, vf_ref, cf_ref, rf_ref, of_ref, False),
               (qb_ref, kb_ref, vb_ref, cb_ref, rb_ref, ob_ref, True))
    for d, (q_ref, k_ref, v_ref, c_ref, r_ref, o_ref, rev) in enumerate(streams):
        q, k = q_ref[...], k_ref[...]
        kk, qk = _dot_nt(k, k), _dot_nt(q, k)
        col = c_ref[0]
        row = r_ref[0, 0]
        for e in range(2):
            o, s_new = _gdn_unit(q, k, kk, qk, v_ref[:, e * A_DV:(e + 1) * A_DV],
                                 col[:, e:e + 1], row[e:e + 1, :], col[:, 2 + e:3 + e],
                                 s_scr[d, e], rev)
            o_ref[:, e * A_DV:(e + 1) * A_DV] = o
            s_scr[d, e] = s_new

    @pl.when(c == pl.num_programs(2) - 1)
    def _():
        sff_ref[0, 0] = s_scr[0]
        sfb_ref[0, 0] = s_scr[1]


def _gdn_scan(q, k, v, colf, colb, rowf, rowb, s0f, s0b, buf_f, buf_b, n_seq, n_tok, row0):
    nc = n_tok // CHUNK
    cb0 = row0 // CHUNK
    fwd = lambda s, j, c: cb0 + s * nc + c
    bwd = lambda s, j, c: cb0 + s * nc + (nc - 1 - c)

    def stream_specs(ix):
        return [pl.BlockSpec((CHUNK, A_DK), lambda s, j, c: (ix(s, j, c), j)),
                pl.BlockSpec((CHUNK, A_DK), lambda s, j, c: (ix(s, j, c), j)),
                pl.BlockSpec((CHUNK, 2 * A_DV), lambda s, j, c: (ix(s, j, c), j)),
                pl.BlockSpec((1, CHUNK, 4), lambda s, j, c: (j, ix(s, j, c), 0)),
                pl.BlockSpec((1, 1, 2, CHUNK), lambda s, j, c: (ix(s, j, c), j, 0, 0))]

    st_spec = pl.BlockSpec((1, 1, 2, A_DK, A_DV), lambda s, j, c: (s, j, 0, 0, 0))
    st_shape = jax.ShapeDtypeStruct((n_seq, A_QK_HEADS, 2, A_DK, A_DV), F32)
    any_spec = pl.BlockSpec(memory_space=pl.ANY)

    return pl.pallas_call(
        _gdn_kernel,
        grid=(n_seq, A_QK_HEADS, nc),
        in_specs=stream_specs(fwd) + stream_specs(bwd) + [st_spec, st_spec, any_spec, any_spec],
        out_specs=[pl.BlockSpec((CHUNK, 2 * A_DV), lambda s, j, c: (fwd(s, j, c), j)),
                   pl.BlockSpec((CHUNK, 2 * A_DV), lambda s, j, c: (bwd(s, j, c), j)),
                   st_spec, st_spec],
        out_shape=[jax.ShapeDtypeStruct(buf_f.shape, F32), jax.ShapeDtypeStruct(buf_b.shape, F32),
                   st_shape, st_shape],
        scratch_shapes=[pltpu.VMEM((2, 2, A_DK, A_DV), F32)],
        input_output_aliases={12: 0, 13: 1},
        compiler_params=_cparams(("parallel", "parallel", "arbitrary")),
        name="gdn_scan",
    )(q, k, v, colf, rowf, q, k, v, colb, rowb, s0f, s0b, buf_f, buf_b)


def _gla_block_rows(bc, rev):
    n = CHUNK // 16
    zero = jnp.zeros((1, bc.shape[1]), F32)
    if rev:
        nxt = [bc[16 * (b + 1):16 * (b + 1) + 1] for b in range(n - 1)] + [zero]
        own = [bc[16 * b:16 * b + 1] for b in range(n)]
        mid = bc[32:33]
    else:
        nxt = [zero] + [bc[16 * b - 1:16 * b] for b in range(1, n)]
        own = [bc[16 * b + 15:16 * b + 16] for b in range(n)]
        mid = bc[31:32]
    bcast = lambda rows: jnp.concatenate([jnp.broadcast_to(r, (16, bc.shape[1])) for r in rows], axis=0)
    return bcast(nxt), bcast(own), mid


def _gla_unit(q, k, v, lg, st, rev):
    tri, _, i, j = _tri_masks(rev)
    cum_mat = jnp.where(tri, 1.0, 0.0)
    bc = _dot_exact_lhs(cum_mat, lg)
    btot = bc[0:1] if rev else bc[CHUNK - 1:CHUNK]
    ref16, own16, mid = _gla_block_rows(bc, rev)
    ib, jb = jnp.right_shift(i, 4), jnp.right_shift(j, 4)
    qd = q * jnp.exp(bc - ref16)
    a = jnp.where(tri & (ib == jb), _dot_nt(qd, k * jnp.exp(ref16 - bc)), 0.0)
    pair = (ib == jb - 1) if rev else (ib == jb + 1)
    pair = pair & (jnp.bitwise_and(jnp.minimum(ib, jb), 1) == 0)
    a = a + jnp.where(pair, _dot_nt(qd, k * jnp.exp(own16 - bc)), 0.0)
    half = ((i < 32) & (j >= 32)) if rev else ((i >= 32) & (j < 32))
    q32 = q * jnp.exp(jnp.minimum(bc - mid, 0.0))
    k32 = k * jnp.exp(jnp.minimum(mid - bc, 0.0))
    a = a + jnp.where(half, _dot_nt(q32, k32), 0.0)
    o = _dot_nt(q * jnp.exp(bc), st) + _dot(a, v)
    st_new = st * jnp.exp(btot) + _dot_tn(v, k * jnp.exp(btot - bc))
    return o, st_new


def _gla_kernel(qf_ref, kf_ref, vf_ref, gf_ref, qb_ref, kb_ref, vb_ref, gb_ref, s0f_ref, s0b_ref,
                bf_buf, bb_buf, of_ref, ob_ref, sff_ref, sfb_ref, s_scr):
    del bf_buf, bb_buf
    c = pl.program_id(2)

    @pl.when(c == 0)
    def _():
        s_scr[0] = s0f_ref[0, 0]
        s_scr[1] = s0b_ref[0, 0]

    streams = ((qf_ref, kf_ref, vf_ref, gf_ref, of_ref, False),
               (qb_ref, kb_ref, vb_ref, gb_ref, ob_ref, True))
    for d, (q_ref, k_ref, v_ref, g_ref, o_ref, rev) in enumerate(streams):
        o, s_new = _gla_unit(q_ref[...], k_ref[...], v_ref[...], g_ref[...], s_scr[d], rev)
        o_ref[...] = o
        s_scr[d] = s_new

    @pl.when(c == pl.num_programs(2) - 1)
    def _():
        sff_ref[0, 0] = s_scr[0]
        sfb_ref[0, 0] = s_scr[1]


def _gla_scan(q, k, v, lg, s0f, s0b, buf_f, buf_b, n_seq, n_tok, row0):
    nc = n_tok // CHUNK
    cb0 = row0 // CHUNK
    fwd = lambda s, h, c: cb0 + s * nc + c
    bwd = lambda s, h, c: cb0 + s * nc + (nc - 1 - c)

    def stream_specs(ix, d):
        return [pl.BlockSpec((CHUNK, D_DK), lambda s, h, c: (ix(s, h, c), h)),
                pl.BlockSpec((CHUNK, D_DK), lambda s, h, c: (ix(s, h, c), h)),
                pl.BlockSpec((CHUNK, D_DV), lambda s, h, c: (ix(s, h, c), h)),
                pl.BlockSpec((CHUNK, D_DK), lambda s, h, c: (ix(s, h, c), d * D_HEADS + h))]

    st_spec = pl.BlockSpec((1, 1, D_DV, D_DK), lambda s, h, c: (s, h, 0, 0))
    st_shape = jax.ShapeDtypeStruct((n_seq, D_HEADS, D_DV, D_DK), F32)
    any_spec = pl.BlockSpec(memory_space=pl.ANY)
    return pl.pallas_call(
        _gla_kernel,
        grid=(n_seq, D_HEADS, nc),
        in_specs=stream_specs(fwd, 0) + stream_specs(bwd, 1) + [st_spec, st_spec, any_spec, any_spec],
        out_specs=[pl.BlockSpec((CHUNK, D_DV), lambda s, h, c: (fwd(s, h, c), h)),
                   pl.BlockSpec((CHUNK, D_DV), lambda s, h, c: (bwd(s, h, c), h)),
                   st_spec, st_spec],
        out_shape=[jax.ShapeDtypeStruct(buf_f.shape, F32), jax.ShapeDtypeStruct(buf_b.shape, F32),
                   st_shape, st_shape],
        scratch_shapes=[pltpu.VMEM((2, D_DV, D_DK), F32)],
        input_output_aliases={10: 0, 11: 1},
        compiler_params=_cparams(("parallel", "parallel", "arbitrary")),
        name="gla_scan",
    )(q, k, v, lg, q, k, v, lg, s0f, s0b, buf_f, buf_b)


class _Layout:
    def __init__(self, shapes):
        self.groups = []
        row, seq = 0, 0
        for b, n in shapes:
            self.groups.append((b, n, row, seq))
            row += b * n
            seq += b
        self.rows, self.n_seq = row, seq

    def to_seqs(self, main, meta):
        out = []
        c = main.shape[-1]
        for b, n, row, seq in self.groups:
            m = meta[seq * N_META:(seq + b) * N_META].reshape(b, N_META, c)
            x = main[row:row + b * n].reshape(b, n, c)
            out.append(jnp.concatenate([m, x], axis=1))
        return out

    def from_seqs(self, seqs):
        main = jnp.concatenate([s[:, N_META:].reshape(-1, s.shape[-1]) for s in seqs], axis=0)
        meta = jnp.concatenate([s[:, :N_META].reshape(-1, s.shape[-1]) for s in seqs], axis=0)
        return main, meta


def _meta_chunk(meta):
    c = meta.shape[-1]
    m = meta.reshape(-1, N_META, c)
    return jnp.pad(m, ((0, 0), (CHUNK - N_META, 0), (0, 0))).reshape(-1, c)


def _meta_unchunk(x):
    c = x.shape[-1]
    return x.reshape(-1, CHUNK, c)[:, CHUNK - N_META:].reshape(-1, c)


def _map_seqs(lay, fn, main, meta):
    return lay.from_seqs([fn(s) for s in lay.to_seqs(main, meta)])


def _run_scan(scan, lay, main_args, meta_args, state_shape, width):
    s_all = lay.n_seq
    zeros = jnp.zeros((s_all,) + state_shape, F32)
    mrows = s_all * CHUNK
    mbuf = jnp.zeros((mrows, width), F32)
    _, _, s_meta, _ = scan(*meta_args, zeros, zeros, mbuf, mbuf, s_all, CHUNK, 0)
    of = jnp.zeros((lay.rows, width), F32)
    ob = jnp.zeros((lay.rows, width), F32)
    finals = []
    for b, n, row, seq in lay.groups:
        of, ob, _, sfb = scan(*main_args, s_meta[seq:seq + b], zeros[seq:seq + b], of, ob, b, n, row)
        finals.append(sfb)
    mof, mob, _, _ = scan(*meta_args, zeros, jnp.concatenate(finals, axis=0), mbuf, mbuf, s_all, CHUNK, 0)
    return (of, ob), (_meta_unchunk(mof), _meta_unchunk(mob))


def _gdn_layer(lay, h, hm, norm_g, w_in, conv_w, a_log, dt_bias, out_g, w_out):
    n_main = A_CONV_CH + A_VW
    p, pm = _norm_proj(h, norm_g, w_in[:, :n_main]), _norm_proj(hm, norm_g, w_in[:, :n_main])
    ab, abm = _norm_proj(h, norm_g, w_in[:, n_main:]), _norm_proj(hm, norm_g, w_in[:, n_main:])

    def prep(x):
        prev = jnp.pad(x[:, :-1], ((0, 0), (1, 0), (0, 0)))
        nxt = jnp.pad(x[:, 1:], ((0, 0), (0, 1), (0, 0)))
        y = jax.nn.silu(prev * conv_w[0] + x * conv_w[1] + nxt * conv_w[2])
        b, l, _ = y.shape
        qk = y[..., :2 * A_QK].reshape(b, l, 2 * A_QK_HEADS, A_DK)
        qk = qk * lax.rsqrt(jnp.sum(qk * qk, axis=-1, keepdims=True) + EPS)
        qk = qk.reshape(b, l, 2, A_QK) * jnp.array([A_DK ** -0.5, 1.0], F32)[:, None]
        return jnp.concatenate([qk.reshape(b, l, 2 * A_QK), y[..., 2 * A_QK:]], axis=-1)

    qkv, qkvm = _map_seqs(lay, prep, p[:, :A_CONV_CH], pm[:, :A_CONV_CH])
    z, zm = p[:, A_CONV_CH:], pm[:, A_CONV_CH:]

    def gates(x):
        x = x.reshape(-1, 2, 2, A_V_HEADS)
        g = -jnp.exp(a_log.astype(F32)) * jax.nn.softplus(x[:, 0] + dt_bias.astype(F32))
        return g, jax.nn.sigmoid(x[:, 1])

    def pack(g, beta, d):
        r = g.shape[0]
        gd, bd = g[:, d].reshape(r, A_QK_HEADS, 2), beta[:, d].reshape(r, A_QK_HEADS, 2)
        col = jnp.concatenate([gd, bd], axis=-1).transpose(1, 0, 2)
        row = gd.reshape(r // CHUNK, CHUNK, A_QK_HEADS, 2).transpose(0, 2, 3, 1)
        return col, row

    def scan_args(qkv_rows, g, beta):
        colf, rowf = pack(g, beta, 0)
        colb, rowb = pack(g, beta, 1)
        return (qkv_rows[:, :A_QK], qkv_rows[:, A_QK:2 * A_QK], qkv_rows[:, 2 * A_QK:], colf, colb, rowf, rowb)

    g, beta = gates(ab)
    gm, betam = gates(abm)
    pad_gate = lambda t: _meta_chunk(t.reshape(-1, 2 * A_V_HEADS)).reshape(-1, 2, A_V_HEADS)
    (of, ob), (mof, mob) = _run_scan(
        _gdn_scan, lay, scan_args(qkv, g, beta), scan_args(_meta_chunk(qkvm), pad_gate(gm), pad_gate(betam)),
        (A_QK_HEADS, 2, A_DK, A_DV), A_VW)
    gain = jnp.tile(out_g.astype(F32), A_V_HEADS)
    return (_out_proj([of, ob], z, gain, w_out, h, A_DV), _out_proj([mof, mob], zm, gain, w_out, hm, A_DV))


def _rope_tables(n_tok):
    rows = n_tok // GRID_W
    row = jnp.repeat(jnp.arange(rows, dtype=F32), GRID_W)
    col = (jnp.arange(rows * GRID_W) % GRID_W).astype(F32)
    inv = ROPE_THETA ** (-jnp.arange(0, ROPE_AXIS, 2, dtype=F32) / ROPE_AXIS)
    ang = jnp.concatenate([row[:, None] * inv, col[:, None] * inv], axis=-1)
    return jnp.cos(ang), jnp.sin(ang)


def _attn_tiles(n_tok):
    return min(512, n_tok)


def _gqa_layer(lay, h, hm, norm_g, w_in, q_g, k_g, w_out):
    p, pm = _norm_proj(h, norm_g, w_in), _norm_proj(hm, norm_g, w_in)

    def norm_heads(x, g):
        r = x.shape[0]
        x = x.reshape(r, -1, B_DH)
        return x * lax.rsqrt(jnp.mean(x * x, axis=-1, keepdims=True) + EPS) * g.astype(F32)

    def rope(x, cos, sin):
        half = B_DH // 2
        x1, x2 = x[..., :half], x[..., half:]
        c, s = cos[None, :, None, :], sin[None, :, None, :]
        return jnp.concatenate([x1 * c - x2 * s, x2 * c + x1 * s], axis=-1)

    scale = B_DH ** -0.5
    qn = norm_heads(p[:, :B_QW], q_g) * scale
    kn = norm_heads(p[:, B_QW:B_QW + B_KVW], k_g)
    q_parts, k_parts = [], []
    for b, n, row, _ in lay.groups:
        cos, sin = _rope_tables(n)
        q_parts.append(rope(qn[row:row + b * n].reshape(b, n, B_HEADS, B_DH), cos, sin).reshape(b * n, B_QW))
        k_parts.append(rope(kn[row:row + b * n].reshape(b, n, B_KV_HEADS, B_DH), cos, sin).reshape(b * n, B_KVW))
    q = jnp.concatenate(q_parts, axis=0).astype(BF16)
    k = jnp.concatenate(k_parts, axis=0).astype(BF16)
    v = p[:, B_QW + B_KVW:B_QW + 2 * B_KVW].astype(BF16)
    z = p[:, B_QW + 2 * B_KVW:]
    qm = (norm_heads(pm[:, :B_QW], q_g) * scale).reshape(-1, B_QW).astype(BF16)
    km = norm_heads(pm[:, B_QW:B_QW + B_KVW], k_g).reshape(-1, B_KVW).astype(BF16)
    vm = pm[:, B_QW + B_KVW:B_QW + 2 * B_KVW].astype(BF16)
    zm = pm[:, B_QW + 2 * B_KVW:]
    pad_keys = lambda t: jnp.pad(t.reshape(lay.n_seq, N_META, -1), ((0, 0), (0, LANES - N_META), (0, 0)))
    kmp, vmp = pad_keys(km), pad_keys(vm)

    o = jnp.zeros((lay.rows, B_QW), F32)
    om = jnp.zeros((lay.n_seq * N_META, B_QW), F32)
    for b, n, row, seq in lay.groups:
        t = _attn_tiles(n)
        o = _gqa_attention(q, k, v, kmp[seq:seq + b], vmp[seq:seq + b], o, b, n, n, row, row, t, t)
        om = _gqa_attention(qm, k, v, kmp[seq:seq + b], vmp[seq:seq + b], om, b, N_META, n,
                            seq * N_META, row, N_META, t)
    ones = jnp.ones((B_QW,), F32)
    return _out_proj([o], z, ones, w_out, h, 0), _out_proj([om], zm, ones, w_out, hm, 0)


def _t5_bucket_np(rel):
    nb = REL_BUCKETS // 2
    max_exact = nb // 2
    n = np.abs(rel)
    steps = nb - max_exact
    large = np.full(n.shape, max_exact, np.int64)
    for kstep in range(1, steps + 1):
        large += (n.astype(np.float64) ** steps >= (max_exact ** steps) * float(REL_MAX_DIST // max_exact) ** kstep)
    large = np.minimum(large, nb - 1)
    return np.where(rel > 0, nb, 0) + np.where(n < max_exact, n, large)


def _diff_layer(lay, h, hm, norm_g, w_in, q_g, k_g, lam, sub_g, w_out, rel_bias, layer_idx):
    p, pm = _norm_proj(h, norm_g, w_in), _norm_proj(hm, norm_g, w_in)

    def norm_maps(x, g):
        r = x.shape[0]
        x = x.reshape(r, -1, C_DQK)
        return (x * lax.rsqrt(jnp.mean(x * x, axis=-1, keepdims=True) + EPS) * g.astype(F32)).reshape(r, -1)

    scale = C_DQK ** -0.5
    q, qm = (norm_maps(p[:, :C_QKW], q_g) * scale).astype(BF16), (norm_maps(pm[:, :C_QKW], q_g) * scale).astype(BF16)
    k, km = norm_maps(p[:, C_QKW:2 * C_QKW], k_g).astype(BF16), norm_maps(pm[:, C_QKW:2 * C_QKW], k_g).astype(BF16)
    v, vm = p[:, 2 * C_QKW:2 * C_QKW + C_VW].astype(BF16), pm[:, 2 * C_QKW:2 * C_QKW + C_VW].astype(BF16)
    z, zm = p[:, 2 * C_QKW + C_VW:], pm[:, 2 * C_QKW + C_VW:]
    pad_keys = lambda t: jnp.pad(t.reshape(lay.n_seq, N_META, -1), ((0, 0), (0, LANES - N_META), (0, 0)))
    kmp, vmp = pad_keys(km), pad_keys(vm)

    lam = lam.astype(F32)
    lam_init = 0.8 - 0.6 * math.exp(-0.3 * layer_idx)
    lam_full = jnp.exp(jnp.sum(lam[0] * lam[1])) - jnp.exp(jnp.sum(lam[2] * lam[3])) + lam_init
    table = rel_bias.astype(F32).T
    look = lambda rel: table[:, _t5_bucket_np(rel)]
    nb = REL_BUCKETS // 2
    cst = jnp.zeros((C_HEADS, 8, LANES), F32)
    cst = cst.at[:, 0].set(table[:, nb - 1:nb]).at[:, 1].set(table[:, 2 * nb - 1:2 * nb]).at[:, 2].set(lam_full)
    ar = np.arange(LANES)
    tt = jnp.stack([look(o * LANES + ar[None, :] - ar[:, None]) for o in (-1, 0, 1)], axis=1)
    key_pad = jnp.asarray(np.where(ar < N_META, 0.0, NEG), F32)
    mq = np.arange(N_META)
    tt_m = jnp.stack([look(N_META + ar[None, :] - mq[:, None])] * 3, axis=1)
    bm_m = (jnp.where(ar < N_META, look(ar[None, :] - mq[:, None]), 0.0) + key_pad)[:, None]

    o = jnp.zeros((lay.rows, C_VW), F32)
    om = jnp.zeros((lay.n_seq * N_META, C_VW), F32)
    for b, n, row, seq in lay.groups:
        t = _attn_tiles(n)
        pq = np.arange(t)
        near = jnp.where(ar < N_META, look(ar[None, :] - N_META - pq[:, None]), 0.0) + key_pad
        far = jnp.broadcast_to(jnp.where(ar < N_META, table[:, nb - 1][:, None, None], 0.0) + key_pad,
                               (C_HEADS, t, LANES))
        bm = jnp.stack([near, far], axis=1)
        o = _diff_attention(q, k, v, kmp[seq:seq + b], vmp[seq:seq + b], tt, bm, cst, o,
                            b, n, n, row, row, t, t, False)
        om = _diff_attention(qm, k, v, kmp[seq:seq + b], vmp[seq:seq + b], tt_m, bm_m, cst, om,
                             b, N_META, n, seq * N_META, row, N_META, t, True)
    gain = jnp.tile(sub_g.astype(F32), C_HEADS) * (1.0 - lam_init)
    return _out_proj([o], z, gain, w_out, h, C_DV), _out_proj([om], zm, gain, w_out, hm, C_DV)


def _gla_layer(lay, h, hm, norm_g, w_in, gate_w2, gate_b, out_g, w_out):
    n_main = 2 * D_QKW + 2 * D_VW
    p, pm = _norm_proj(h, norm_g, w_in[:, :n_main]), _norm_proj(hm, norm_g, w_in[:, :n_main])
    lr, lrm = _norm_proj(h, norm_g, w_in[:, n_main:]), _norm_proj(hm, norm_g, w_in[:, n_main:])

    def log_gates(x):
        logit = jnp.einsum('rnk,nkc->rnc', x.reshape(-1, 2, GATE_RANK), gate_w2) + gate_b
        return (jax.nn.log_sigmoid(logit.astype(F32)) / GATE_TAU).reshape(-1, 2 * D_QKW)

    def scan_args(rows, lg):
        return (rows[:, :D_QKW] * D_DK ** -0.5, rows[:, D_QKW:2 * D_QKW], rows[:, 2 * D_QKW:2 * D_QKW + D_VW], lg)

    (of, ob), (mof, mob) = _run_scan(
        _gla_scan, lay, scan_args(p, log_gates(lr)),
        scan_args(_meta_chunk(pm), _meta_chunk(log_gates(lrm))), (D_HEADS, D_DV, D_DK), D_VW)
    z, zm = p[:, 2 * D_QKW + D_VW:], pm[:, 2 * D_QKW + D_VW:]
    gain = jnp.tile(out_g.astype(F32), D_HEADS)
    return _out_proj([of, ob], z, gain, w_out, h, D_DV), _out_proj([mof, mob], zm, gain, w_out, hm, D_DV)


def _trunk(xs, meta_tokens, rel_bias, a, b, c, d, depth):
    lay = _Layout([(x.shape[0], x.shape[1]) for x in xs])
    h = jnp.concatenate([x.reshape(-1, D_MODEL) for x in xs], axis=0)
    hm = jnp.tile(meta_tokens.astype(F32), (lay.n_seq, 1))
    for i in range(depth):
        m, j = i % 4, i // 4
        if m == 0:
            h, hm = _gdn_layer(lay, h, hm, *(t[j] for t in a))
        elif m == 1:
            h, hm = _gqa_layer(lay, h, hm, *(t[j] for t in b))
        elif m == 2:
            h, hm = _diff_layer(lay, h, hm, *(t[j] for t in c), rel_bias, i)
        else:
            h, hm = _gla_layer(lay, h, hm, *(t[j] for t in d))
    return [h[row:row + bsz * n].reshape(bsz, n, D_MODEL) for bsz, n, row, _ in lay.groups]


def kernel(x_prompt, x_sample, meta_tokens, rel_bias, a_norm, a_w_in, a_conv, a_a_log, a_dt_bias, a_out_norm, a_w_out, b_norm, b_w_in, b_q_norm, b_k_norm, b_w_out, c_norm, c_w_in, c_q_norm, c_k_norm, c_lambda, c_sub_norm, c_w_out, d_norm, d_w_in, d_gate_w2, d_gate_b, d_out_norm, d_w_out):
    y_prompt, y_sample = _trunk(
        [x_prompt, x_sample], meta_tokens, rel_bias,
        (a_norm, a_w_in, a_conv, a_a_log, a_dt_bias, a_out_norm, a_w_out),
        (b_norm, b_w_in, b_q_norm, b_k_norm, b_w_out),
        (c_norm, c_w_in, c_q_norm, c_k_norm, c_lambda, c_sub_norm, c_w_out),
        (d_norm, d_w_in, d_gate_w2, d_gate_b, d_out_norm, d_w_out), 4)
    return (y_prompt, y_sample)
```

```python
import functools
import math

import numpy as np
import jax
import jax.numpy as jnp
from jax import lax
from jax.experimental import pallas as pl
from jax.experimental.pallas import tpu as pltpu

F32 = jnp.float32
BF16 = jnp.bfloat16

D_MODEL = 1024
N_META = 16
CHUNK = 64
GRID_W = 64
EPS = 1e-6
LANES = 128
VMEM_LIMIT = 56 * 1024 * 1024
NEG = -1e30
LOG2E = math.log2(math.e)
STRIP = 512
ONES_ROWS = 16
GQA_TQ, GQA_TK = 512, 1024
DIFF_T = 1024

A_QK_HEADS, A_V_HEADS, A_DK, A_DV = 8, 16, 128, 128
A_QK, A_VW = A_QK_HEADS * A_DK, A_V_HEADS * A_DV
A_CONV_CH = 2 * A_QK + A_VW
A_HG = 4
B_HEADS, B_KV_HEADS, B_DH = 8, 2, 128
B_GROUP = B_HEADS // B_KV_HEADS
B_QW, B_KVW = B_HEADS * B_DH, B_KV_HEADS * B_DH
ROPE_AXIS = B_DH // 2
ROPE_THETA = 10000.0
C_HEADS, C_DQK, C_DV = 8, 64, 128
C_QKW, C_VW = C_HEADS * 2 * C_DQK, C_HEADS * C_DV
REL_BUCKETS, REL_MAX_DIST = 32, 128
D_HEADS, D_DK, D_DV = 4, 128, 256
D_QKW, D_VW = D_HEADS * D_DK, D_HEADS * D_DV
GATE_RANK, GATE_TAU = 16, 16.0


def _cparams(sem):
    return pltpu.CompilerParams(dimension_semantics=sem, vmem_limit_bytes=VMEM_LIMIT)


def _sigmoid(x):
    return 1.0 / (1.0 + jnp.exp(-x))


def _dot(a, b):
    return jnp.dot(a.astype(BF16), b.astype(BF16), preferred_element_type=F32)


def _dot_nt(a, b):
    return lax.dot_general(a.astype(BF16), b.astype(BF16), (((1,), (1,)), ((), ())),
                           preferred_element_type=F32)


def _dot_tn(a, b):
    return lax.dot_general(a.astype(BF16), b.astype(BF16), (((0,), (0,)), ((), ())),
                           preferred_element_type=F32)


def _split3(x):
    x1 = x.astype(BF16)
    r1 = x - x1.astype(F32)
    x2 = r1.astype(BF16)
    x3 = (r1 - x2.astype(F32)).astype(BF16)
    return x1, x2, x3


def _dot_exact_lhs(m01, x):
    m = m01.astype(BF16)
    x1, x2, x3 = _split3(x)
    return (jnp.dot(m, x1, preferred_element_type=F32) + jnp.dot(m, x2, preferred_element_type=F32)
            + jnp.dot(m, x3, preferred_element_type=F32))


def _dot_exact_rhs(x, m01):
    m = m01.astype(BF16)
    x1, x2, x3 = _split3(x)
    return (jnp.dot(x1, m, preferred_element_type=F32) + jnp.dot(x2, m, preferred_element_type=F32)
            + jnp.dot(x3, m, preferred_element_type=F32))


def _norm_proj_kernel(x_ref, g_ref, w_ref, o_ref, xn_ref):
    @pl.when(pl.program_id(1) == 0)
    def _():
        x = x_ref[...]
        ms = jnp.mean(x * x, axis=-1, keepdims=True)
        xn_ref[...] = (x * lax.rsqrt(ms + EPS) * g_ref[...]).astype(BF16)

    o_ref[...] = jnp.dot(xn_ref[...], w_ref[...], preferred_element_type=F32)


def _norm_proj(x, g, w):
    m, k = x.shape
    n = w.shape[1]
    tm = 512 if m % 512 == 0 else m
    tn = 512 if n % 512 == 0 else n
    return pl.pallas_call(
        _norm_proj_kernel,
        grid=(m // tm, n // tn),
        in_specs=[pl.BlockSpec((tm, k), lambda i, j: (i, 0)),
                  pl.BlockSpec((1, k), lambda i, j: (0, 0)),
                  pl.BlockSpec((k, tn), lambda i, j: (0, j))],
        out_specs=pl.BlockSpec((tm, tn), lambda i, j: (i, j)),
        out_shape=jax.ShapeDtypeStruct((m, n), F32),
        scratch_shapes=[pltpu.VMEM((tm, k), BF16)],
        compiler_params=_cparams(("parallel", "arbitrary")),
        name="norm_proj",
    )(x, g.reshape(1, k).astype(F32), w.astype(BF16))


def _out_proj_kernel(*refs, n_o, hw):
    o_refs = refs[:n_o]
    z_ref, g_ref, w_ref, h_ref, out_ref = refs[n_o:]
    k = z_ref.shape[1]
    cw = hw if hw else 256
    acc = h_ref[...]
    for c0 in range(0, k, cw):
        o = o_refs[0][:, c0:c0 + cw]
        for r in o_refs[1:]:
            o = o + r[:, c0:c0 + cw]
        if hw:
            ms = jnp.mean(o * o, axis=-1, keepdims=True)
            o = o * lax.rsqrt(ms + EPS) * g_ref[:, c0:c0 + cw]
        z = z_ref[:, c0:c0 + cw]
        gated = o * (z * _sigmoid(z))
        acc = acc + jnp.dot(gated.astype(BF16), w_ref[c0:c0 + cw, :], preferred_element_type=F32)
    out_ref[...] = acc


def _out_proj(os, z, g, w, h, hw):
    m, k = z.shape
    n = w.shape[1]
    tm = 256 if m % 256 == 0 else m
    row = lambda i: (i, 0)
    fixed = lambda i: (0, 0)
    return pl.pallas_call(
        functools.partial(_out_proj_kernel, n_o=len(os), hw=hw),
        grid=(m // tm,),
        in_specs=[pl.BlockSpec((tm, k), row) for _ in os] + [
            pl.BlockSpec((tm, k), row), pl.BlockSpec((1, k), fixed),
            pl.BlockSpec((k, n), fixed), pl.BlockSpec((tm, n), row)],
        out_specs=pl.BlockSpec((tm, n), row),
        out_shape=jax.ShapeDtypeStruct((m, n), F32),
        compiler_params=_cparams(("parallel",)),
        name="out_proj",
    )(*os, z, g.reshape(1, k).astype(F32), w.astype(BF16), h)


def _strip_update(s_t, c, v_ext, m_scr, l_scr, acc_scr, cols):
    dv = acc_scr.shape[0]
    m_prev = m_scr[:, cols]
    m_new = jnp.maximum(m_prev, jnp.max(s_t, axis=0, keepdims=True) + c)
    alpha = jnp.exp2(m_prev - m_new)
    p = jnp.exp2(s_t - (m_new - c)).astype(BF16)
    pv = jnp.dot(v_ext, p, preferred_element_type=F32)
    acc_scr[:, cols] = alpha * acc_scr[:, cols] + pv[:dv]
    l_scr[:, cols] = alpha * l_scr[:, cols] + pv[dv:dv + 1]
    m_scr[:, cols] = m_new


def _attn_init(m_scr, l_scr, acc_scr):
    m_scr[...] = jnp.full(m_scr.shape, NEG, F32)
    l_scr[...] = jnp.zeros(l_scr.shape, F32)
    acc_scr[...] = jnp.zeros(acc_scr.shape, F32)


def _with_ones(v_t):
    return jnp.concatenate([v_t, jnp.ones((ONES_ROWS, v_t.shape[1]), BF16)], axis=0)


def _strips(width):
    w = min(STRIP, width)
    return [slice(c0, c0 + w) for c0 in range(0, width, w)]


def _gqa_kernel(qt_ref, k_ref, vt_ref, km_ref, vmt_ref, buf_ref, o_ref, m_scr, l_scr, acc_scr):
    del buf_ref
    ki = pl.program_id(3)
    tq = qt_ref.shape[1]
    heads = [slice(g * B_DH, (g + 1) * B_DH) for g in range(B_GROUP)]

    @pl.when(ki == 0)
    def _():
        _attn_init(m_scr, l_scr, acc_scr)
        key = lax.broadcasted_iota(jnp.int32, (LANES, 1), 0)
        pad = jnp.where(key < N_META, 0.0, NEG)
        vm_ext = _with_ones(vmt_ref[0])
        for g, rows in enumerate(heads):
            for cols in _strips(tq):
                s_t = jnp.dot(km_ref[0], qt_ref[rows, cols], preferred_element_type=F32) + pad
                _strip_update(s_t, 0.0, vm_ext, m_scr, l_scr, acc_scr,
                              slice(g * tq + cols.start, g * tq + cols.stop))

    k = k_ref[...]
    v_ext = _with_ones(vt_ref[...])
    for g, rows in enumerate(heads):
        for cols in _strips(tq):
            s_t = jnp.dot(k, qt_ref[rows, cols], preferred_element_type=F32)
            _strip_update(s_t, 0.0, v_ext, m_scr, l_scr, acc_scr,
                          slice(g * tq + cols.start, g * tq + cols.stop))

    @pl.when(ki == pl.num_programs(3) - 1)
    def _():
        for g, rows in enumerate(heads):
            cols = slice(g * tq, (g + 1) * tq)
            o_ref[:, rows] = (acc_scr[:, cols] / l_scr[:, cols]).T


def _gqa_attention(qt, k, vt, km, vmt, buf, n_seq, n_q, n_k, q_col0, k_row0, tq, tk):
    nq, nk = n_q // tq, n_k // tk
    qb0, kb0 = q_col0 // tq, k_row0 // tk
    r = B_GROUP * tq
    gw = B_GROUP * B_DH
    return pl.pallas_call(
        _gqa_kernel,
        grid=(n_seq, B_KV_HEADS, nq, nk),
        in_specs=[pl.BlockSpec((gw, tq), lambda s, h, i, j: (h, qb0 + s * nq + i)),
                  pl.BlockSpec((tk, B_DH), lambda s, h, i, j: (kb0 + s * nk + j, h)),
                  pl.BlockSpec((B_DH, tk), lambda s, h, i, j: (h, kb0 + s * nk + j)),
                  pl.BlockSpec((1, LANES, B_DH), lambda s, h, i, j: (s, 0, h)),
                  pl.BlockSpec((1, B_DH, LANES), lambda s, h, i, j: (s, h, 0)),
                  pl.BlockSpec(memory_space=pl.ANY)],
        out_specs=pl.BlockSpec((tq, gw), lambda s, h, i, j: (qb0 + s * nq + i, h)),
        out_shape=jax.ShapeDtypeStruct(buf.shape, F32),
        scratch_shapes=[pltpu.VMEM((1, r), F32), pltpu.VMEM((1, r), F32), pltpu.VMEM((B_DH, r), F32)],
        input_output_aliases={5: 0},
        compiler_params=_cparams(("parallel", "parallel", "parallel", "arbitrary")),
        name="gqa_attention",
    )(qt, k, vt, km, vmt, buf)


def _diff_kernel(qt_ref, k_ref, vt_ref, km_ref, vmt_ref, tt_ref, bm_ref, cst_ref, buf_ref, o_ref,
                 m_scr, l_scr, acc_scr, *, meta_q):
    del buf_ref
    qi, ki = pl.program_id(2), pl.program_id(3)
    tq, tk = qt_ref.shape[1], k_ref.shape[0]
    nt = tk // LANES
    qt = qt_ref[...]
    dim = lax.broadcasted_iota(jnp.int32, qt.shape, 0)
    zero = jnp.zeros_like(qt)
    maps = (jnp.where(dim < C_DQK, qt, zero), jnp.where(dim >= C_DQK, qt, zero))
    c_left = cst_ref[0, 0:1, 0:1]
    c_right = cst_ref[0, 1:2, 0:1]
    lam = cst_ref[0, 2:3, 0:1]
    strips = [(mp, cols) for mp in range(2) for cols in _strips(tq)]
    scr = lambda mp, cols: slice(mp * tq + cols.start, mp * tq + cols.stop)

    @pl.when(ki == 0)
    def _():
        _attn_init(m_scr, l_scr, acc_scr)
        vm_ext = _with_ones(vmt_ref[0])
        for mp, cols in strips:
            s_t = jnp.dot(km_ref[0], maps[mp][:, cols], preferred_element_type=F32) + bm_ref[0, 0, :, cols]
            _strip_update(s_t, 0.0, vm_ext, m_scr, l_scr, acc_scr, scr(mp, cols))

    k = k_ref[...]
    v_ext = _with_ones(vt_ref[...])

    def tile_bias(off):
        if off <= -2:
            return jnp.broadcast_to(c_left, (LANES, LANES))
        if off >= 2:
            return jnp.broadcast_to(c_right, (LANES, LANES))
        return tt_ref[0, off + 1]

    def strip_bias(dd, cols):
        q_tiles = range(cols.start // LANES, cols.stop // LANES)
        if meta_q:
            rows = [[tt_ref[0, 0]]] + [[tile_bias(2)] for _ in range(nt - 1)]
        else:
            rows = [[tile_bias(dd * nt + b - a) for a in q_tiles] for b in range(nt)]
        return jnp.concatenate([jnp.concatenate(r, axis=1) for r in rows], axis=0)

    d = ki - qi
    specials = (0,) if meta_q else (-1, 0, 1)
    for dd in specials:
        @pl.when(d == dd)
        def _(dd=dd):
            for mp, cols in strips:
                s_t = jnp.dot(k, maps[mp][:, cols], preferred_element_type=F32) + strip_bias(dd, cols)
                _strip_update(s_t, 0.0, v_ext, m_scr, l_scr, acc_scr, scr(mp, cols))

    @pl.when((d < specials[0]) | (d > specials[-1]))
    def _():
        c = jnp.where(d < 0, c_left, c_right)
        for mp, cols in strips:
            s_t = jnp.dot(k, maps[mp][:, cols], preferred_element_type=F32)
            _strip_update(s_t, c, v_ext, m_scr, l_scr, acc_scr, scr(mp, cols))

    @pl.when(ki == pl.num_programs(3) - 1)
    def _():
        o1 = acc_scr[:, :tq] / l_scr[:, :tq]
        o2 = acc_scr[:, tq:] / l_scr[:, tq:]
        o_ref[...] = (o1 - lam * o2).T


def _diff_attention(qt, k, vt, km, vmt, tt, bm, cst, buf, n_seq, n_q, n_k, q_col0, k_row0, tq, tk, meta_q):
    nq, nk = n_q // tq, n_k // tk
    qb0, kb0 = q_col0 // tq, k_row0 // tk
    r = 2 * tq
    return pl.pallas_call(
        functools.partial(_diff_kernel, meta_q=meta_q),
        grid=(n_seq, C_HEADS, nq, nk),
        in_specs=[pl.BlockSpec((LANES, tq), lambda s, h, i, j: (h, qb0 + s * nq + i)),
                  pl.BlockSpec((tk, LANES), lambda s, h, i, j: (kb0 + s * nk + j, h)),
                  pl.BlockSpec((C_DV, tk), lambda s, h, i, j: (h, kb0 + s * nk + j)),
                  pl.BlockSpec((1, LANES, LANES), lambda s, h, i, j: (s, 0, h)),
                  pl.BlockSpec((1, C_DV, LANES), lambda s, h, i, j: (s, h, 0)),
                  pl.BlockSpec((1, 3, LANES, LANES), lambda s, h, i, j: (h, 0, 0, 0)),
                  pl.BlockSpec((1, 1, LANES, tq), lambda s, h, i, j: (h, jnp.minimum(i, 1), 0, 0)),
                  pl.BlockSpec((1, 8, LANES), lambda s, h, i, j: (h, 0, 0)),
                  pl.BlockSpec(memory_space=pl.ANY)],
        out_specs=pl.BlockSpec((tq, C_DV), lambda s, h, i, j: (qb0 + s * nq + i, h)),
        out_shape=jax.ShapeDtypeStruct(buf.shape, F32),
        scratch_shapes=[pltpu.VMEM((1, r), F32), pltpu.VMEM((1, r), F32), pltpu.VMEM((C_DV, r), F32)],
        input_output_aliases={8: 0},
        compiler_params=_cparams(("parallel", "parallel", "parallel", "arbitrary")),
        name="diff_attention",
    )(qt, k, vt, km, vmt, tt, bm, cst, buf)


def _tri_masks(rev):
    i = lax.broadcasted_iota(jnp.int32, (CHUNK, CHUNK), 0)
    j = lax.broadcasted_iota(jnp.int32, (CHUNK, CHUNK), 1)
    if rev:
        return j >= i, j > i, i, j
    return j <= i, j < i, i, j


def _cum_mat_t(rev):
    t = lax.broadcasted_iota(jnp.int32, (CHUNK, CHUNK), 0)
    j = lax.broadcasted_iota(jnp.int32, (CHUNK, CHUNK), 1)
    return jnp.where((t >= j) if rev else (t <= j), 1.0, 0.0)


def _gdn_kernel(qf_ref, kf_ref, vf_ref, cf_ref, rf_ref, qb_ref, kb_ref, vb_ref, cb_ref, rb_ref,
                s0f_ref, s0b_ref, bf_buf, bb_buf, of_ref, ob_ref, sff_ref, sfb_ref, s_scr):
    del bf_buf, bb_buf
    c = pl.program_id(2)
    nv = 2 * A_HG

    @pl.when(c == 0)
    def _():
        s_scr[0] = s0f_ref[0]
        s_scr[1] = s0b_ref[0]

    streams = ((qf_ref, kf_ref, vf_ref, cf_ref, rf_ref, False), (qb_ref, kb_ref, vb_ref, cb_ref, rb_ref, True))
    units = [(d, e) for d in range(2) for e in range(nv)]
    eye = jnp.where(_tri_masks(False)[0] & _tri_masks(True)[0], 1.0, 0.0)
    q, k, kk, qk, gc_all, gr_all, col, masks = [], [], [], [], [], [], [], []
    for q_ref, k_ref, _, c_ref, r_ref, rev in streams:
        tri, strict, _, _ = _tri_masks(rev)
        masks.append((tri, strict))
        qd = [q_ref[:, j * A_DK:(j + 1) * A_DK] for j in range(A_HG)]
        kd = [k_ref[:, j * A_DK:(j + 1) * A_DK] for j in range(A_HG)]
        q.append(qd)
        k.append(kd)
        kk.append([_dot_nt(x, x) for x in kd])
        qk.append([_dot_nt(x, y) for x, y in zip(qd, kd)])
        cv = c_ref[...]
        col.append(cv)
        gc_all.append(_dot_exact_lhs(jnp.where(tri, 1.0, 0.0), cv))
        gr_all.append(_dot_exact_rhs(r_ref[0, 0], _cum_mat_t(rev)))
    gcb = [jnp.broadcast_to(gc_all[d][:, e:e + 1], (CHUNK, LANES)) for d, e in units]
    beta = [jnp.broadcast_to(col[d][:, nv + e:nv + e + 1], (CHUNK, LANES)) for d, e in units]
    decay = [jnp.exp(jnp.where(masks[d][0], gcb[u][:, :CHUNK] - gr_all[d][e:e + 1, :], -jnp.inf))
             for u, (d, e) in enumerate(units)]
    a = [jnp.where(masks[d][1], beta[u][:, :CHUNK] * kk[d][e // 2] * decay[u], 0.0)
         for u, (d, e) in enumerate(units)]
    x = [eye - au for au in a]
    p = a
    for _ in range(5):
        p = [_dot(pu, pu) for pu in p]
        x = [xu + _dot(xu, pu) for xu, pu in zip(x, p)]
    egc = [jnp.exp(g) for g in gcb]
    v = [streams[d][2][:, e * A_DV:(e + 1) * A_DV] for d, e in units]
    uw = [_dot(x[u], jnp.concatenate([v[u] * beta[u], k[d][e // 2] * (beta[u] * egc[u])], axis=1))
          for u, (d, e) in enumerate(units)]
    s = [s_scr[d, e] for d, e in units]
    r2 = [_dot(jnp.concatenate([uw[u][:, A_DV:], q[d][e // 2] * egc[u]], axis=0), s[u])
          for u, (d, e) in enumerate(units)]
    v_new = [uw[u][:, :A_DV] - r2[u][:CHUNK] for u in range(len(units))]
    o = [r2[u][CHUNK:] + _dot(qk[d][e // 2] * decay[u], v_new[u]) for u, (d, e) in enumerate(units)]
    gtot = [gcb[u][0:1, :] if d else gcb[u][CHUNK - 1:CHUNK, :] for u, (d, e) in enumerate(units)]
    s_new = [s[u] * jnp.exp(gtot[u]) + _dot_tn(k[d][e // 2] * jnp.exp(gtot[u] - gcb[u]), v_new[u])
             for u, (d, e) in enumerate(units)]
    for u, (d, e) in enumerate(units):
        (ob_ref if d else of_ref)[:, e * A_DV:(e + 1) * A_DV] = o[u]
        s_scr[d, e] = s_new[u]

    @pl.when(c == pl.num_programs(2) - 1)
    def _():
        sff_ref[0] = s_scr[0]
        sfb_ref[0] = s_scr[1]


def _gdn_scan(q, k, v, colf, colb, rowf, rowb, s0f, s0b, buf_f, buf_b, n_seq, n_tok, row0):
    nc = n_tok // CHUNK
    cb0 = row0 // CHUNK
    ng = A_QK_HEADS // A_HG
    nv = 2 * A_HG
    fwd = lambda s, j, c: cb0 + s * nc + c
    bwd = lambda s, j, c: cb0 + s * nc + (nc - 1 - c)

    def stream_specs(ix):
        return [pl.BlockSpec((CHUNK, A_HG * A_DK), lambda s, j, c: (ix(s, j, c), j)),
                pl.BlockSpec((CHUNK, A_HG * A_DK), lambda s, j, c: (ix(s, j, c), j)),
                pl.BlockSpec((CHUNK, nv * A_DV), lambda s, j, c: (ix(s, j, c), j)),
                pl.BlockSpec((CHUNK, LANES), lambda s, j, c: (ix(s, j, c), j)),
                pl.BlockSpec((1, 1, nv, CHUNK), lambda s, j, c: (ix(s, j, c), j, 0, 0))]

    st_spec = pl.BlockSpec((1, nv, A_DK, A_DV), lambda s, j, c: (s, j, 0, 0))
    st_shape = jax.ShapeDtypeStruct((n_seq, A_V_HEADS, A_DK, A_DV), F32)
    any_spec = pl.BlockSpec(memory_space=pl.ANY)
    return pl.pallas_call(
        _gdn_kernel,
        grid=(n_seq, ng, nc),
        in_specs=stream_specs(fwd) + stream_specs(bwd) + [st_spec, st_spec, any_spec, any_spec],
        out_specs=[pl.BlockSpec((CHUNK, nv * A_DV), lambda s, j, c: (fwd(s, j, c), j)),
                   pl.BlockSpec((CHUNK, nv * A_DV), lambda s, j, c: (bwd(s, j, c), j)),
                   st_spec, st_spec],
        out_shape=[jax.ShapeDtypeStruct(buf_f.shape, F32), jax.ShapeDtypeStruct(buf_b.shape, F32),
                   st_shape, st_shape],
        scratch_shapes=[pltpu.VMEM((2, nv, A_DK, A_DV), F32)],
        input_output_aliases={12: 0, 13: 1},
        compiler_params=_cparams(("parallel", "parallel", "arbitrary")),
        name="gdn_scan",
    )(q, k, v, colf, rowf, q, k, v, colb, rowb, s0f, s0b, buf_f, buf_b)


def _gla_block_rows(bc, rev):
    n = CHUNK // 16
    zero = jnp.zeros((1, bc.shape[1]), F32)
    if rev:
        nxt = [bc[16 * (b + 1):16 * (b + 1) + 1] for b in range(n - 1)] + [zero]
        own = [bc[16 * b:16 * b + 1] for b in range(n)]
        mid = bc[32:33]
    else:
        nxt = [zero] + [bc[16 * b - 1:16 * b] for b in range(1, n)]
        own = [bc[16 * b + 15:16 * b + 16] for b in range(n)]
        mid = bc[31:32]
    bcast = lambda rows: jnp.concatenate([jnp.broadcast_to(r, (16, bc.shape[1])) for r in rows], axis=0)
    return bcast(nxt), bcast(own), mid


def _gla_unit(q, k, v, lg, st, rev):
    tri, _, i, j = _tri_masks(rev)
    cum_mat = jnp.where(tri, 1.0, 0.0)
    bc = _dot_exact_lhs(cum_mat, lg)
    btot = bc[0:1] if rev else bc[CHUNK - 1:CHUNK]
    ref16, own16, mid = _gla_block_rows(bc, rev)
    ib, jb = jnp.right_shift(i, 4), jnp.right_shift(j, 4)
    qd = q * jnp.exp(bc - ref16)
    a = jnp.where(tri & (ib == jb), _dot_nt(qd, k * jnp.exp(ref16 - bc)), 0.0)
    pair = (ib == jb - 1) if rev else (ib == jb + 1)
    pair = pair & (jnp.bitwise_and(jnp.minimum(ib, jb), 1) == 0)
    a = a + jnp.where(pair, _dot_nt(qd, k * jnp.exp(own16 - bc)), 0.0)
    half = ((i < 32) & (j >= 32)) if rev else ((i >= 32) & (j < 32))
    q32 = q * jnp.exp(jnp.minimum(bc - mid, 0.0))
    k32 = k * jnp.exp(jnp.minimum(mid - bc, 0.0))
    a = a + jnp.where(half, _dot_nt(q32, k32), 0.0)
    o = _dot_nt(q * jnp.exp(bc), st) + _dot(a, v)
    st_new = st * jnp.exp(btot) + _dot_tn(v, k * jnp.exp(btot - bc))
    return o, st_new


def _gla_kernel(qf_ref, kf_ref, vf_ref, gf_ref, qb_ref, kb_ref, vb_ref, gb_ref, s0f_ref, s0b_ref,
                bf_buf, bb_buf, of_ref, ob_ref, sff_ref, sfb_ref, s_scr):
    del bf_buf, bb_buf
    c = pl.program_id(2)

    @pl.when(c == 0)
    def _():
        s_scr[0] = s0f_ref[0, 0]
        s_scr[1] = s0b_ref[0, 0]

    streams = ((qf_ref, kf_ref, vf_ref, gf_ref, of_ref, False),
               (qb_ref, kb_ref, vb_ref, gb_ref, ob_ref, True))
    for d, (q_ref, k_ref, v_ref, g_ref, o_ref, rev) in enumerate(streams):
        o, s_new = _gla_unit(q_ref[...], k_ref[...], v_ref[...], g_ref[...], s_scr[d], rev)
        o_ref[...] = o
        s_scr[d] = s_new

    @pl.when(c == pl.num_programs(2) - 1)
    def _():
        sff_ref[0, 0] = s_scr[0]
        sfb_ref[0, 0] = s_scr[1]


def _gla_scan(q, k, v, lg, s0f, s0b, buf_f, buf_b, n_seq, n_tok, row0):
    nc = n_tok // CHUNK
    cb0 = row0 // CHUNK
    fwd = lambda s, h, c: cb0 + s * nc + c
    bwd = lambda s, h, c: cb0 + s * nc + (nc - 1 - c)

    def stream_specs(ix, d):
        return [pl.BlockSpec((CHUNK, D_DK), lambda s, h, c: (ix(s, h, c), h)),
                pl.BlockSpec((CHUNK, D_DK), lambda s, h, c: (ix(s, h, c), h)),
                pl.BlockSpec((CHUNK, D_DV), lambda s, h, c: (ix(s, h, c), h)),
                pl.BlockSpec((CHUNK, D_DK), lambda s, h, c: (ix(s, h, c), d * D_HEADS + h))]

    st_spec = pl.BlockSpec((1, 1, D_DV, D_DK), lambda s, h, c: (s, h, 0, 0))
    st_shape = jax.ShapeDtypeStruct((n_seq, D_HEADS, D_DV, D_DK), F32)
    any_spec = pl.BlockSpec(memory_space=pl.ANY)
    return pl.pallas_call(
        _gla_kernel,
        grid=(n_seq, D_HEADS, nc),
        in_specs=stream_specs(fwd, 0) + stream_specs(bwd, 1) + [st_spec, st_spec, any_spec, any_spec],
        out_specs=[pl.BlockSpec((CHUNK, D_DV), lambda s, h, c: (fwd(s, h, c), h)),
                   pl.BlockSpec((CHUNK, D_DV), lambda s, h, c: (bwd(s, h, c), h)),
                   st_spec, st_spec],
        out_shape=[jax.ShapeDtypeStruct(buf_f.shape, F32), jax.ShapeDtypeStruct(buf_b.shape, F32),
                   st_shape, st_shape],
        scratch_shapes=[pltpu.VMEM((2, D_DV, D_DK), F32)],
        input_output_aliases={10: 0, 11: 1},
        compiler_params=_cparams(("parallel", "parallel", "arbitrary")),
        name="gla_scan",
    )(q, k, v, lg, q, k, v, lg, s0f, s0b, buf_f, buf_b)


class _Layout:
    def __init__(self, shapes):
        self.groups = []
        row, seq = 0, 0
        for b, n in shapes:
            self.groups.append((b, n, row, seq))
            row += b * n
            seq += b
        self.rows, self.n_seq = row, seq

    def to_seqs(self, main, meta):
        out = []
        c = main.shape[-1]
        for b, n, row, seq in self.groups:
            m = meta[seq * N_META:(seq + b) * N_META].reshape(b, N_META, c)
            x = main[row:row + b * n].reshape(b, n, c)
            out.append(jnp.concatenate([m, x], axis=1))
        return out

    def from_seqs(self, seqs):
        main = jnp.concatenate([s[:, N_META:].reshape(-1, s.shape[-1]) for s in seqs], axis=0)
        meta = jnp.concatenate([s[:, :N_META].reshape(-1, s.shape[-1]) for s in seqs], axis=0)
        return main, meta


def _meta_chunk(meta):
    c = meta.shape[-1]
    m = meta.reshape(-1, N_META, c)
    return jnp.pad(m, ((0, 0), (CHUNK - N_META, 0), (0, 0))).reshape(-1, c)


def _meta_unchunk(x):
    c = x.shape[-1]
    return x.reshape(-1, CHUNK, c)[:, CHUNK - N_META:].reshape(-1, c)


def _map_seqs(lay, fn, main, meta):
    return lay.from_seqs([fn(s) for s in lay.to_seqs(main, meta)])


def _run_scan(scan, lay, main_args, meta_args, state_shape, width):
    s_all = lay.n_seq
    zeros = jnp.zeros((s_all,) + state_shape, F32)
    mrows = s_all * CHUNK
    mbuf = jnp.zeros((mrows, width), F32)
    _, _, s_meta, _ = scan(*meta_args, zeros, zeros, mbuf, mbuf, s_all, CHUNK, 0)
    of = jnp.zeros((lay.rows, width), F32)
    ob = jnp.zeros((lay.rows, width), F32)
    finals = []
    for b, n, row, seq in lay.groups:
        of, ob, _, sfb = scan(*main_args, s_meta[seq:seq + b], zeros[seq:seq + b], of, ob, b, n, row)
        finals.append(sfb)
    mof, mob, _, _ = scan(*meta_args, zeros, jnp.concatenate(finals, axis=0), mbuf, mbuf, s_all, CHUNK, 0)
    return (of, ob), (_meta_unchunk(mof), _meta_unchunk(mob))


def _gdn_layer(lay, h, hm, norm_g, w_in, conv_w, a_log, dt_bias, out_g, w_out):
    n_main = A_CONV_CH + A_VW
    p, pm = _norm_proj(h, norm_g, w_in[:, :n_main]), _norm_proj(hm, norm_g, w_in[:, :n_main])
    ab, abm = _norm_proj(h, norm_g, w_in[:, n_main:]), _norm_proj(hm, norm_g, w_in[:, n_main:])

    def prep(x):
        prev = jnp.pad(x[:, :-1], ((0, 0), (1, 0), (0, 0)))
        nxt = jnp.pad(x[:, 1:], ((0, 0), (0, 1), (0, 0)))
        y = jax.nn.silu(prev * conv_w[0] + x * conv_w[1] + nxt * conv_w[2])
        b, l, _ = y.shape
        qk = y[..., :2 * A_QK].reshape(b, l, 2 * A_QK_HEADS, A_DK)
        qk = qk * lax.rsqrt(jnp.sum(qk * qk, axis=-1, keepdims=True) + EPS)
        qk = qk.reshape(b, l, 2, A_QK) * jnp.array([A_DK ** -0.5, 1.0], F32)[:, None]
        return jnp.concatenate([qk.reshape(b, l, 2 * A_QK), y[..., 2 * A_QK:]], axis=-1)

    qkv, qkvm = _map_seqs(lay, prep, p[:, :A_CONV_CH], pm[:, :A_CONV_CH])
    z, zm = p[:, A_CONV_CH:], pm[:, A_CONV_CH:]

    def gates(x):
        x = x.reshape(-1, 2, 2, A_V_HEADS)
        g = -jnp.exp(a_log.astype(F32)) * jax.nn.softplus(x[:, 0] + dt_bias.astype(F32))
        return g, jax.nn.sigmoid(x[:, 1])

    ng, nv = A_QK_HEADS // A_HG, 2 * A_HG

    def pack(g, beta, d):
        r = g.shape[0]
        gd, bd = g[:, d].reshape(r, ng, nv), beta[:, d].reshape(r, ng, nv)
        col = jnp.concatenate([gd, bd, jnp.zeros((r, ng, LANES - 2 * nv), F32)], axis=-1).reshape(r, ng * LANES)
        row = gd.reshape(r // CHUNK, CHUNK, ng, nv).transpose(0, 2, 3, 1)
        return col, row

    def scan_args(qkv_rows, g, beta):
        colf, rowf = pack(g, beta, 0)
        colb, rowb = pack(g, beta, 1)
        return (qkv_rows[:, :A_QK], qkv_rows[:, A_QK:2 * A_QK], qkv_rows[:, 2 * A_QK:], colf, colb, rowf, rowb)

    g, beta = gates(ab)
    gm, betam = gates(abm)
    pad_gate = lambda t: _meta_chunk(t.reshape(-1, 2 * A_V_HEADS)).reshape(-1, 2, A_V_HEADS)
    (of, ob), (mof, mob) = _run_scan(
        _gdn_scan, lay, scan_args(qkv, g, beta), scan_args(_meta_chunk(qkvm), pad_gate(gm), pad_gate(betam)),
        (A_V_HEADS, A_DK, A_DV), A_VW)
    gain = jnp.tile(out_g.astype(F32), A_V_HEADS)
    return (_out_proj([of, ob], z, gain, w_out, h, A_DV), _out_proj([mof, mob], zm, gain, w_out, hm, A_DV))


def _rope_tables(n_tok):
    rows = n_tok // GRID_W
    row = jnp.repeat(jnp.arange(rows, dtype=F32), GRID_W)
    col = (jnp.arange(rows * GRID_W) % GRID_W).astype(F32)
    inv = ROPE_THETA ** (-jnp.arange(0, ROPE_AXIS, 2, dtype=F32) / ROPE_AXIS)
    ang = jnp.concatenate([row[:, None] * inv, col[:, None] * inv], axis=-1)
    return jnp.cos(ang), jnp.sin(ang)


def _pad_meta(x, n_seq):
    return jnp.pad(x.reshape(n_seq, N_META, -1), ((0, 0), (0, LANES - N_META), (0, 0)))


def _gqa_layer(lay, h, hm, norm_g, w_in, q_g, k_g, w_out):
    p, pm = _norm_proj(h, norm_g, w_in), _norm_proj(hm, norm_g, w_in)

    def norm_heads(x, g):
        r = x.shape[0]
        x = x.reshape(r, -1, B_DH)
        return x * lax.rsqrt(jnp.mean(x * x, axis=-1, keepdims=True) + EPS) * g.astype(F32)

    def rope(x, cos, sin):
        half = B_DH // 2
        x1, x2 = x[..., :half], x[..., half:]
        c, s = cos[None, :, None, :], sin[None, :, None, :]
        return jnp.concatenate([x1 * c - x2 * s, x2 * c + x1 * s], axis=-1)

    scale = B_DH ** -0.5 * LOG2E
    qn = norm_heads(p[:, :B_QW], q_g) * scale
    kn = norm_heads(p[:, B_QW:B_QW + B_KVW], k_g)
    q_parts, k_parts = [], []
    for b, n, row, _ in lay.groups:
        cos, sin = _rope_tables(n)
        q_parts.append(rope(qn[row:row + b * n].reshape(b, n, B_HEADS, B_DH), cos, sin).reshape(b * n, B_QW))
        k_parts.append(rope(kn[row:row + b * n].reshape(b, n, B_KV_HEADS, B_DH), cos, sin).reshape(b * n, B_KVW))
    qt = jnp.concatenate(q_parts, axis=0).astype(BF16).T
    k = jnp.concatenate(k_parts, axis=0).astype(BF16)
    vt = p[:, B_QW + B_KVW:B_QW + 2 * B_KVW].astype(BF16).T
    z = p[:, B_QW + 2 * B_KVW:]
    s_all = lay.n_seq
    qm = (norm_heads(pm[:, :B_QW], q_g) * scale).reshape(-1, B_QW).astype(BF16)
    km = norm_heads(pm[:, B_QW:B_QW + B_KVW], k_g).reshape(-1, B_KVW).astype(BF16)
    vm = pm[:, B_QW + B_KVW:B_QW + 2 * B_KVW].astype(BF16)
    zm = pm[:, B_QW + 2 * B_KVW:]
    qmt = _pad_meta(qm, s_all).reshape(s_all * LANES, B_QW).T
    kmp, vmt = _pad_meta(km, s_all), _pad_meta(vm, s_all).transpose(0, 2, 1)

    o = jnp.zeros((lay.rows, B_QW), F32)
    om = jnp.zeros((s_all * LANES, B_QW), F32)
    for b, n, row, seq in lay.groups:
        tq, tk = min(GQA_TQ, n), min(GQA_TK, n)
        o = _gqa_attention(qt, k, vt, kmp[seq:seq + b], vmt[seq:seq + b], o, b, n, n, row, row, tq, tk)
        om = _gqa_attention(qmt, k, vt, kmp[seq:seq + b], vmt[seq:seq + b], om, b, LANES, n,
                            seq * LANES, row, LANES, tk)
    om = om.reshape(s_all, LANES, B_QW)[:, :N_META].reshape(-1, B_QW)
    ones = jnp.ones((B_QW,), F32)
    return _out_proj([o], z, ones, w_out, h, 0), _out_proj([om], zm, ones, w_out, hm, 0)


def _t5_bucket_np(rel):
    nb = REL_BUCKETS // 2
    max_exact = nb // 2
    n = np.abs(rel)
    steps = nb - max_exact
    large = np.full(n.shape, max_exact, np.int64)
    for kstep in range(1, steps + 1):
        large += (n.astype(np.float64) ** steps >= (max_exact ** steps) * float(REL_MAX_DIST // max_exact) ** kstep)
    large = np.minimum(large, nb - 1)
    return np.where(rel > 0, nb, 0) + np.where(n < max_exact, n, large)


def _diff_layer(lay, h, hm, norm_g, w_in, q_g, k_g, lam, sub_g, w_out, rel_bias, layer_idx):
    p, pm = _norm_proj(h, norm_g, w_in), _norm_proj(hm, norm_g, w_in)

    def norm_maps(x, g):
        r = x.shape[0]
        x = x.reshape(r, -1, C_DQK)
        return (x * lax.rsqrt(jnp.mean(x * x, axis=-1, keepdims=True) + EPS) * g.astype(F32)).reshape(r, -1)

    s_all = lay.n_seq
    scale = C_DQK ** -0.5 * LOG2E
    qt = (norm_maps(p[:, :C_QKW], q_g) * scale).astype(BF16).T
    qm = (norm_maps(pm[:, :C_QKW], q_g) * scale).astype(BF16)
    k, km = norm_maps(p[:, C_QKW:2 * C_QKW], k_g).astype(BF16), norm_maps(pm[:, C_QKW:2 * C_QKW], k_g).astype(BF16)
    vt, vm = p[:, 2 * C_QKW:2 * C_QKW + C_VW].astype(BF16).T, pm[:, 2 * C_QKW:2 * C_QKW + C_VW].astype(BF16)
    z, zm = p[:, 2 * C_QKW + C_VW:], pm[:, 2 * C_QKW + C_VW:]
    qmt = _pad_meta(qm, s_all).reshape(s_all * LANES, C_QKW).T
    kmp, vmt = _pad_meta(km, s_all), _pad_meta(vm, s_all).transpose(0, 2, 1)

    lam = lam.astype(F32)
    lam_init = 0.8 - 0.6 * math.exp(-0.3 * layer_idx)
    lam_full = jnp.exp(jnp.sum(lam[0] * lam[1])) - jnp.exp(jnp.sum(lam[2] * lam[3])) + lam_init
    table = rel_bias.astype(F32).T * LOG2E
    look = lambda rel: table[:, _t5_bucket_np(rel)]
    nb = REL_BUCKETS // 2
    cst = jnp.zeros((C_HEADS, 8, LANES), F32)
    cst = cst.at[:, 0].set(table[:, nb - 1:nb]).at[:, 1].set(table[:, 2 * nb - 1:2 * nb]).at[:, 2].set(lam_full)
    ar = np.arange(LANES)
    key_pad = jnp.asarray(np.where(ar < N_META, 0.0, NEG), F32)[:, None]
    tt = jnp.stack([look(o * LANES + ar[:, None] - ar[None, :]) for o in (-1, 0, 1)], axis=1)
    mq = np.minimum(ar, N_META - 1)
    tt_m = jnp.stack([look(N_META + ar[:, None] - mq[None, :])] * 3, axis=1)
    meta_rows = (ar < N_META)[:, None]
    bm_m = (jnp.where(meta_rows, look(ar[:, None] - mq[None, :]), 0.0) + key_pad)[:, None]

    o = jnp.zeros((lay.rows, C_VW), F32)
    om = jnp.zeros((s_all * LANES, C_VW), F32)
    for b, n, row, seq in lay.groups:
        t = min(DIFF_T, n)
        pq = np.arange(t)
        near = jnp.where(meta_rows, look(ar[:, None] - N_META - pq[None, :]), 0.0) + key_pad
        far = jnp.broadcast_to(jnp.where(meta_rows, table[:, nb - 1][:, None, None], 0.0) + key_pad,
                               (C_HEADS, LANES, t))
        bm = jnp.stack([near, far], axis=1)
        o = _diff_attention(qt, k, vt, kmp[seq:seq + b], vmt[seq:seq + b], tt, bm, cst, o,
                            b, n, n, row, row, t, t, False)
        om = _diff_attention(qmt, k, vt, kmp[seq:seq + b], vmt[seq:seq + b], tt_m, bm_m, cst, om,
                             b, LANES, n, seq * LANES, row, LANES, t, True)
    om = om.reshape(s_all, LANES, C_VW)[:, :N_META].reshape(-1, C_VW)
    gain = jnp.tile(sub_g.astype(F32), C_HEADS) * (1.0 - lam_init)
    return _out_proj([o], z, gain, w_out, h, C_DV), _out_proj([om], zm, gain, w_out, hm, C_DV)


def _gla_layer(lay, h, hm, norm_g, w_in, gate_w2, gate_b, out_g, w_out):
    n_main = 2 * D_QKW + 2 * D_VW
    p, pm = _norm_proj(h, norm_g, w_in[:, :n_main]), _norm_proj(hm, norm_g, w_in[:, :n_main])
    lr, lrm = _norm_proj(h, norm_g, w_in[:, n_main:]), _norm_proj(hm, norm_g, w_in[:, n_main:])

    def log_gates(x):
        logit = jnp.einsum('rnk,nkc->rnc', x.reshape(-1, 2, GATE_RANK), gate_w2) + gate_b
        return (jax.nn.log_sigmoid(logit.astype(F32)) / GATE_TAU).reshape(-1, 2 * D_QKW)

    def scan_args(rows, lg):
        return (rows[:, :D_QKW] * D_DK ** -0.5, rows[:, D_QKW:2 * D_QKW], rows[:, 2 * D_QKW:2 * D_QKW + D_VW], lg)

    (of, ob), (mof, mob) = _run_scan(
        _gla_scan, lay, scan_args(p, log_gates(lr)),
        scan_args(_meta_chunk(pm), _meta_chunk(log_gates(lrm))), (D_HEADS, D_DV, D_DK), D_VW)
    z, zm = p[:, 2 * D_QKW + D_VW:], pm[:, 2 * D_QKW + D_VW:]
    gain = jnp.tile(out_g.astype(F32), D_HEADS)
    return _out_proj([of, ob], z, gain, w_out, h, D_DV), _out_proj([mof, mob], zm, gain, w_out, hm, D_DV)


def _trunk(xs, meta_tokens, rel_bias, a, b, c, d, depth):
    lay = _Layout([(x.shape[0], x.shape[1]) for x in xs])
    h = jnp.concatenate([x.reshape(-1, D_MODEL) for x in xs], axis=0)
    hm = jnp.tile(meta_tokens.astype(F32), (lay.n_seq, 1))
    for i in range(depth):
        m, j = i % 4, i // 4
        if m == 0:
            h, hm = _gdn_layer(lay, h, hm, *(t[j] for t in a))
        elif m == 1:
            h, hm = _gqa_layer(lay, h, hm, *(t[j] for t in b))
        elif m == 2:
            h, hm = _diff_layer(lay, h, hm, *(t[j] for t in c), rel_bias, i)
        else:
            h, hm = _gla_layer(lay, h, hm, *(t[j] for t in d))
    return [h[row:row + bsz * n].reshape(bsz, n, D_MODEL) for bsz, n, row, _ in lay.groups]


def kernel(x_prompt, x_sample, meta_tokens, rel_bias, a_norm, a_w_in, a_conv, a_a_log, a_dt_bias, a_out_norm, a_w_out, b_norm, b_w_in, b_q_norm, b_k_norm, b_w_out, c_norm, c_w_in, c_q_norm, c_k_norm, c_lambda, c_sub_norm, c_w_out, d_norm, d_w_in, d_gate_w2, d_gate_b, d_out_norm, d_w_out):
    y_prompt, y_sample = _trunk(
        [x_prompt, x_sample], meta_tokens, rel_bias,
        (a_norm, a_w_in, a_conv, a_a_log, a_dt_bias, a_out_norm, a_w_out),
        (b_norm, b_w_in, b_q_norm, b_k_norm, b_w_out),
        (c_norm, c_w_in, c_q_norm, c_k_norm, c_lambda, c_sub_norm, c_w_out),
        (d_norm, d_w_in, d_gate_w2, d_gate_b, d_out_norm, d_w_out), 4)
    return (y_prompt, y_sample)
```

```python
import functools
import math

import numpy as np
import jax
import jax.numpy as jnp
from jax import lax
from jax.experimental import pallas as pl
from jax.experimental.pallas import tpu as pltpu

F32 = jnp.float32
BF16 = jnp.bfloat16

D_MODEL = 1024
N_META = 16
CHUNK = 64
GRID_W = 64
EPS = 1e-6
LANES = 128
VMEM_LIMIT = 56 * 1024 * 1024
NEG = -1e30
LOG2E = math.log2(math.e)
STRIP = 1024
ONES_ROWS = 16
GQA_TQ, GQA_TK = 1024, 1024
DIFF_TQ, DIFF_TK = 2048, 1024

A_QK_HEADS, A_V_HEADS, A_DK, A_DV = 8, 16, 128, 128
A_QK, A_VW = A_QK_HEADS * A_DK, A_V_HEADS * A_DV
A_CONV_CH = 2 * A_QK + A_VW
A_HG = 4
B_HEADS, B_KV_HEADS, B_DH = 8, 2, 128
B_GROUP = B_HEADS // B_KV_HEADS
B_QW, B_KVW = B_HEADS * B_DH, B_KV_HEADS * B_DH
ROPE_AXIS = B_DH // 2
ROPE_THETA = 10000.0
C_HEADS, C_DQK, C_DV = 8, 64, 128
C_QKW, C_VW = C_HEADS * 2 * C_DQK, C_HEADS * C_DV
REL_BUCKETS, REL_MAX_DIST = 32, 128
D_HEADS, D_DK, D_DV = 4, 128, 256
D_QKW, D_VW = D_HEADS * D_DK, D_HEADS * D_DV
GATE_RANK, GATE_TAU = 16, 16.0


def _cparams(sem):
    return pltpu.CompilerParams(dimension_semantics=sem, vmem_limit_bytes=VMEM_LIMIT)


def _sigmoid(x):
    return 1.0 / (1.0 + jnp.exp(-x))


def _dot(a, b):
    return jnp.dot(a.astype(BF16), b.astype(BF16), preferred_element_type=F32)


def _dot_nt(a, b):
    return lax.dot_general(a.astype(BF16), b.astype(BF16), (((1,), (1,)), ((), ())),
                           preferred_element_type=F32)


def _dot_tn(a, b):
    return lax.dot_general(a.astype(BF16), b.astype(BF16), (((0,), (0,)), ((), ())),
                           preferred_element_type=F32)


def _split3(x):
    x1 = x.astype(BF16)
    r1 = x - x1.astype(F32)
    x2 = r1.astype(BF16)
    x3 = (r1 - x2.astype(F32)).astype(BF16)
    return x1, x2, x3


def _dot_exact_lhs(m01, x):
    m = m01.astype(BF16)
    x1, x2, x3 = _split3(x)
    return (jnp.dot(m, x1, preferred_element_type=F32) + jnp.dot(m, x2, preferred_element_type=F32)
            + jnp.dot(m, x3, preferred_element_type=F32))


def _dot_exact_rhs(x, m01):
    m = m01.astype(BF16)
    x1, x2, x3 = _split3(x)
    return (jnp.dot(x1, m, preferred_element_type=F32) + jnp.dot(x2, m, preferred_element_type=F32)
            + jnp.dot(x3, m, preferred_element_type=F32))


def _norm_proj_kernel(x_ref, g_ref, w_ref, o_ref, xn_ref):
    @pl.when(pl.program_id(1) == 0)
    def _():
        x = x_ref[...]
        ms = jnp.mean(x * x, axis=-1, keepdims=True)
        xn_ref[...] = (x * lax.rsqrt(ms + EPS) * g_ref[...]).astype(BF16)

    o_ref[...] = jnp.dot(xn_ref[...], w_ref[...], preferred_element_type=F32)


def _norm_proj(x, g, w):
    m, k = x.shape
    n = w.shape[1]
    tm = 512 if m % 512 == 0 else m
    tn = 512 if n % 512 == 0 else n
    return pl.pallas_call(
        _norm_proj_kernel,
        grid=(m // tm, n // tn),
        in_specs=[pl.BlockSpec((tm, k), lambda i, j: (i, 0)),
                  pl.BlockSpec((1, k), lambda i, j: (0, 0)),
                  pl.BlockSpec((k, tn), lambda i, j: (0, j))],
        out_specs=pl.BlockSpec((tm, tn), lambda i, j: (i, j)),
        out_shape=jax.ShapeDtypeStruct((m, n), F32),
        scratch_shapes=[pltpu.VMEM((tm, k), BF16)],
        compiler_params=_cparams(("parallel", "arbitrary")),
        name="norm_proj",
    )(x, g.reshape(1, k).astype(F32), w.astype(BF16))


def _out_proj_kernel(*refs, n_o, hw):
    o_refs = refs[:n_o]
    z_ref, g_ref, w_ref, h_ref, out_ref = refs[n_o:]
    k = z_ref.shape[1]
    cw = hw if hw else 256
    acc = h_ref[...]
    for c0 in range(0, k, cw):
        o = o_refs[0][:, c0:c0 + cw]
        for r in o_refs[1:]:
            o = o + r[:, c0:c0 + cw]
        if hw:
            ms = jnp.mean(o * o, axis=-1, keepdims=True)
            o = o * lax.rsqrt(ms + EPS) * g_ref[:, c0:c0 + cw]
        z = z_ref[:, c0:c0 + cw]
        gated = o * (z * _sigmoid(z))
        acc = acc + jnp.dot(gated.astype(BF16), w_ref[c0:c0 + cw, :], preferred_element_type=F32)
    out_ref[...] = acc


def _out_proj(os, z, g, w, h, hw):
    m, k = z.shape
    n = w.shape[1]
    tm = 256 if m % 256 == 0 else m
    row = lambda i: (i, 0)
    fixed = lambda i: (0, 0)
    return pl.pallas_call(
        functools.partial(_out_proj_kernel, n_o=len(os), hw=hw),
        grid=(m // tm,),
        in_specs=[pl.BlockSpec((tm, k), row) for _ in os] + [
            pl.BlockSpec((tm, k), row), pl.BlockSpec((1, k), fixed),
            pl.BlockSpec((k, n), fixed), pl.BlockSpec((tm, n), row)],
        out_specs=pl.BlockSpec((tm, n), row),
        out_shape=jax.ShapeDtypeStruct((m, n), F32),
        compiler_params=_cparams(("parallel",)),
        name="out_proj",
    )(*os, z, g.reshape(1, k).astype(F32), w.astype(BF16), h)


def _strip_update(s_t, c, v_ext, m_scr, l_scr, acc_scr, cols):
    dv = acc_scr.shape[0]
    m_prev = m_scr[:, cols]
    m_new = jnp.maximum(m_prev, jnp.max(s_t, axis=0, keepdims=True) + c)
    alpha = jnp.exp2(m_prev - m_new)
    p = jnp.exp2(s_t - (m_new - c)).astype(BF16)
    pv = jnp.dot(v_ext, p, preferred_element_type=F32)
    acc_scr[:, cols] = alpha * acc_scr[:, cols] + pv[:dv]
    l_scr[:, cols] = alpha * l_scr[:, cols] + pv[dv:dv + 1]
    m_scr[:, cols] = m_new


def _attn_init(m_scr, l_scr, acc_scr):
    m_scr[...] = jnp.full(m_scr.shape, NEG, F32)
    l_scr[...] = jnp.zeros(l_scr.shape, F32)
    acc_scr[...] = jnp.zeros(acc_scr.shape, F32)


def _with_ones(v_t):
    return jnp.concatenate([v_t, jnp.ones((ONES_ROWS, v_t.shape[1]), BF16)], axis=0)


def _strips(width):
    w = min(STRIP, width)
    return [slice(c0, c0 + w) for c0 in range(0, width, w)]


def _gqa_kernel(qt_ref, k_ref, vt_ref, km_ref, vmt_ref, buf_ref, o_ref, m_scr, l_scr, acc_scr):
    del buf_ref
    ki = pl.program_id(3)
    tq = qt_ref.shape[1]
    heads = [slice(g * B_DH, (g + 1) * B_DH) for g in range(B_GROUP)]

    @pl.when(ki == 0)
    def _():
        _attn_init(m_scr, l_scr, acc_scr)
        key = lax.broadcasted_iota(jnp.int32, (LANES, 1), 0)
        pad = jnp.where(key < N_META, 0.0, NEG)
        vm_ext = _with_ones(vmt_ref[0])
        for g, rows in enumerate(heads):
            for cols in _strips(tq):
                s_t = jnp.dot(km_ref[0], qt_ref[rows, cols], preferred_element_type=F32) + pad
                _strip_update(s_t, 0.0, vm_ext, m_scr, l_scr, acc_scr,
                              slice(g * tq + cols.start, g * tq + cols.stop))

    k = k_ref[...]
    v_ext = _with_ones(vt_ref[...])
    work = [(g, rows, cols) for g, rows in enumerate(heads) for cols in _strips(tq)]
    scores = lambda w: jnp.dot(k, qt_ref[w[1], w[2]], preferred_element_type=F32)
    s_next = scores(work[0])
    for n, (g, rows, cols) in enumerate(work):
        s_t, s_next = s_next, (scores(work[n + 1]) if n + 1 < len(work) else None)
        _strip_update(s_t, 0.0, v_ext, m_scr, l_scr, acc_scr,
                      slice(g * tq + cols.start, g * tq + cols.stop))

    @pl.when(ki == pl.num_programs(3) - 1)
    def _():
        for g, rows in enumerate(heads):
            cols = slice(g * tq, (g + 1) * tq)
            o_ref[:, rows] = (acc_scr[:, cols] / l_scr[:, cols]).T


def _gqa_attention(qt, k, vt, km, vmt, buf, n_seq, n_q, n_k, q_col0, k_row0, tq, tk):
    nq, nk = n_q // tq, n_k // tk
    qb0, kb0 = q_col0 // tq, k_row0 // tk
    r = B_GROUP * tq
    gw = B_GROUP * B_DH
    return pl.pallas_call(
        _gqa_kernel,
        grid=(n_seq, B_KV_HEADS, nq, nk),
        in_specs=[pl.BlockSpec((gw, tq), lambda s, h, i, j: (h, qb0 + s * nq + i)),
                  pl.BlockSpec((tk, B_DH), lambda s, h, i, j: (kb0 + s * nk + j, h)),
                  pl.BlockSpec((B_DH, tk), lambda s, h, i, j: (h, kb0 + s * nk + j)),
                  pl.BlockSpec((1, LANES, B_DH), lambda s, h, i, j: (s, 0, h)),
                  pl.BlockSpec((1, B_DH, LANES), lambda s, h, i, j: (s, h, 0)),
                  pl.BlockSpec(memory_space=pl.ANY)],
        out_specs=pl.BlockSpec((tq, gw), lambda s, h, i, j: (qb0 + s * nq + i, h)),
        out_shape=jax.ShapeDtypeStruct(buf.shape, F32),
        scratch_shapes=[pltpu.VMEM((1, r), F32), pltpu.VMEM((1, r), F32), pltpu.VMEM((B_DH, r), F32)],
        input_output_aliases={5: 0},
        compiler_params=_cparams(("parallel", "parallel", "parallel", "arbitrary")),
        name="gqa_attention",
    )(qt, k, vt, km, vmt, buf)


def _diff_kernel(qt_ref, k_ref, vt_ref, km_ref, vmt_ref, tt_ref, bm_ref, cst_ref, buf_ref, o_ref,
                 m_scr, l_scr, acc_scr, *, meta_q):
    del buf_ref
    qi, ki = pl.program_id(2), pl.program_id(3)
    tq, tk = qt_ref.shape[1], k_ref.shape[0]
    nt = tk // LANES
    qt = qt_ref[...]
    dim = lax.broadcasted_iota(jnp.int32, qt.shape, 0)
    zero = jnp.zeros_like(qt)
    maps = (jnp.where(dim < C_DQK, qt, zero), jnp.where(dim >= C_DQK, qt, zero))
    c_left = cst_ref[0, 0:1, 0:1]
    c_right = cst_ref[0, 1:2, 0:1]
    lam = cst_ref[0, 2:3, 0:1]
    strips = [(mp, cols) for mp in range(2) for cols in _strips(tq)]
    scr = lambda mp, cols: slice(mp * tq + cols.start, mp * tq + cols.stop)

    @pl.when(ki == 0)
    def _():
        _attn_init(m_scr, l_scr, acc_scr)
        vm_ext = _with_ones(vmt_ref[0])
        for mp, cols in strips:
            s_t = jnp.dot(km_ref[0], maps[mp][:, cols], preferred_element_type=F32) + bm_ref[0, 0, :, cols]
            _strip_update(s_t, 0.0, vm_ext, m_scr, l_scr, acc_scr, scr(mp, cols))

    k = k_ref[...]
    v_ext = _with_ones(vt_ref[...])

    def tile_bias(off):
        if off <= -2:
            return jnp.broadcast_to(c_left, (LANES, LANES))
        if off >= 2:
            return jnp.broadcast_to(c_right, (LANES, LANES))
        return tt_ref[0, off + 1]

    def strip_bias(dd, cols):
        q_tiles = range(cols.start // LANES, cols.stop // LANES)
        if meta_q:
            rows = [[tt_ref[0, 0]]] + [[tile_bias(2)] for _ in range(nt - 1)]
        else:
            rows = [[tile_bias(dd * nt + b - a) for a in q_tiles] for b in range(nt)]
        return jnp.concatenate([jnp.concatenate(r, axis=1) for r in rows], axis=0)

    def run(c, bias):
        def scores(mp, cols):
            s_t = jnp.dot(k, maps[mp][:, cols], preferred_element_type=F32)
            return s_t if bias is None else s_t + bias(cols)

        s_next = scores(*strips[0])
        for n, (mp, cols) in enumerate(strips):
            s_t, s_next = s_next, (scores(*strips[n + 1]) if n + 1 < len(strips) else None)
            _strip_update(s_t, c, v_ext, m_scr, l_scr, acc_scr, scr(mp, cols))

    ratio = max(tq // tk, 1)
    d = ki - qi * ratio
    specials = (0,) if meta_q else tuple(range(-1, ratio + 1))
    for dd in specials:
        @pl.when(d == dd)
        def _(dd=dd):
            run(0.0, functools.partial(strip_bias, dd))

    @pl.when((d < specials[0]) | (d > specials[-1]))
    def _():
        run(jnp.where(d < 0, c_left, c_right), None)

    @pl.when(ki == pl.num_programs(3) - 1)
    def _():
        o1 = acc_scr[:, :tq] / l_scr[:, :tq]
        o2 = acc_scr[:, tq:] / l_scr[:, tq:]
        o_ref[...] = (o1 - lam * o2).T


def _diff_attention(qt, k, vt, km, vmt, tt, bm, cst, buf, n_seq, n_q, n_k, q_col0, k_row0, tq, tk, meta_q):
    nq, nk = n_q // tq, n_k // tk
    qb0, kb0 = q_col0 // tq, k_row0 // tk
    r = 2 * tq
    return pl.pallas_call(
        functools.partial(_diff_kernel, meta_q=meta_q),
        grid=(n_seq, C_HEADS, nq, nk),
        in_specs=[pl.BlockSpec((LANES, tq), lambda s, h, i, j: (h, qb0 + s * nq + i)),
                  pl.BlockSpec((tk, LANES), lambda s, h, i, j: (kb0 + s * nk + j, h)),
                  pl.BlockSpec((C_DV, tk), lambda s, h, i, j: (h, kb0 + s * nk + j)),
                  pl.BlockSpec((1, LANES, LANES), lambda s, h, i, j: (s, 0, h)),
                  pl.BlockSpec((1, C_DV, LANES), lambda s, h, i, j: (s, h, 0)),
                  pl.BlockSpec((1, 3, LANES, LANES), lambda s, h, i, j: (h, 0, 0, 0)),
                  pl.BlockSpec((1, 1, LANES, tq), lambda s, h, i, j: (h, jnp.minimum(i, 1), 0, 0)),
                  pl.BlockSpec((1, 8, LANES), lambda s, h, i, j: (h, 0, 0)),
                  pl.BlockSpec(memory_space=pl.ANY)],
        out_specs=pl.BlockSpec((tq, C_DV), lambda s, h, i, j: (qb0 + s * nq + i, h)),
        out_shape=jax.ShapeDtypeStruct(buf.shape, F32),
        scratch_shapes=[pltpu.VMEM((1, r), F32), pltpu.VMEM((1, r), F32), pltpu.VMEM((C_DV, r), F32)],
        input_output_aliases={8: 0},
        compiler_params=_cparams(("parallel", "parallel", "parallel", "arbitrary")),
        name="diff_attention",
    )(qt, k, vt, km, vmt, tt, bm, cst, buf)


def _tri_masks(rev):
    i = lax.broadcasted_iota(jnp.int32, (CHUNK, CHUNK), 0)
    j = lax.broadcasted_iota(jnp.int32, (CHUNK, CHUNK), 1)
    if rev:
        return j >= i, j > i, i, j
    return j <= i, j < i, i, j


def _cum_mat_t(rev):
    t = lax.broadcasted_iota(jnp.int32, (CHUNK, CHUNK), 0)
    j = lax.broadcasted_iota(jnp.int32, (CHUNK, CHUNK), 1)
    return jnp.where((t >= j) if rev else (t <= j), 1.0, 0.0)


def _gdn_kernel(qf_ref, kf_ref, vf_ref, cf_ref, rf_ref, qb_ref, kb_ref, vb_ref, cb_ref, rb_ref,
                s0f_ref, s0b_ref, bf_buf, bb_buf, of_ref, ob_ref, sff_ref, sfb_ref, s_scr):
    del bf_buf, bb_buf
    c = pl.program_id(2)
    nv = 2 * A_HG

    @pl.when(c == 0)
    def _():
        s_scr[0] = s0f_ref[0]
        s_scr[1] = s0b_ref[0]

    streams = ((qf_ref, kf_ref, vf_ref, cf_ref, rf_ref, False), (qb_ref, kb_ref, vb_ref, cb_ref, rb_ref, True))
    units = [(d, e) for d in range(2) for e in range(nv)]
    eye = jnp.where(_tri_masks(False)[0] & _tri_masks(True)[0], 1.0, 0.0)
    q, k, kk, qk, gc_all, gr_all, col, masks = [], [], [], [], [], [], [], []
    for q_ref, k_ref, _, c_ref, r_ref, rev in streams:
        tri, strict, _, _ = _tri_masks(rev)
        masks.append((tri, strict))
        qd = [q_ref[:, j * A_DK:(j + 1) * A_DK] for j in range(A_HG)]
        kd = [k_ref[:, j * A_DK:(j + 1) * A_DK] for j in range(A_HG)]
        q.append(qd)
        k.append(kd)
        kk.append([_dot_nt(x, x) for x in kd])
        qk.append([_dot_nt(x, y) for x, y in zip(qd, kd)])
        cv = c_ref[...]
        col.append(cv)
        gc_all.append(_dot_exact_lhs(jnp.where(tri, 1.0, 0.0), cv))
        gr_all.append(_dot_exact_rhs(r_ref[0, 0], _cum_mat_t(rev)))
    gcb = [jnp.broadcast_to(gc_all[d][:, e:e + 1], (CHUNK, LANES)) for d, e in units]
    beta = [jnp.broadcast_to(col[d][:, nv + e:nv + e + 1], (CHUNK, LANES)) for d, e in units]
    decay = [jnp.exp(jnp.where(masks[d][0], gcb[u][:, :CHUNK] - gr_all[d][e:e + 1, :], -jnp.inf))
             for u, (d, e) in enumerate(units)]
    a = [jnp.where(masks[d][1], beta[u][:, :CHUNK] * kk[d][e // 2] * decay[u], 0.0)
         for u, (d, e) in enumerate(units)]
    x = [eye - au for au in a]
    p = a
    for _ in range(5):
        p = [_dot(pu, pu) for pu in p]
        x = [xu + _dot(xu, pu) for xu, pu in zip(x, p)]
    egc = [jnp.exp(g) for g in gcb]
    v = [streams[d][2][:, e * A_DV:(e + 1) * A_DV] for d, e in units]
    uw = [_dot(x[u], jnp.concatenate([v[u] * beta[u], k[d][e // 2] * (beta[u] * egc[u])], axis=1))
          for u, (d, e) in enumerate(units)]
    s = [s_scr[d, e] for d, e in units]
    r2 = [_dot(jnp.concatenate([uw[u][:, A_DV:], q[d][e // 2] * egc[u]], axis=0), s[u])
          for u, (d, e) in enumerate(units)]
    v_new = [uw[u][:, :A_DV] - r2[u][:CHUNK] for u in range(len(units))]
    o = [r2[u][CHUNK:] + _dot(qk[d][e // 2] * decay[u], v_new[u]) for u, (d, e) in enumerate(units)]
    gtot = [gcb[u][0:1, :] if d else gcb[u][CHUNK - 1:CHUNK, :] for u, (d, e) in enumerate(units)]
    s_new = [s[u] * jnp.exp(gtot[u]) + _dot_tn(k[d][e // 2] * jnp.exp(gtot[u] - gcb[u]), v_new[u])
             for u, (d, e) in enumerate(units)]
    for u, (d, e) in enumerate(units):
        (ob_ref if d else of_ref)[:, e * A_DV:(e + 1) * A_DV] = o[u]
        s_scr[d, e] = s_new[u]

    @pl.when(c == pl.num_programs(2) - 1)
    def _():
        sff_ref[0] = s_scr[0]
        sfb_ref[0] = s_scr[1]


def _gdn_scan(qkv, colf, colb, rowf, rowb, s0f, s0b, buf_f, buf_b, n_seq, n_tok, row0):
    nc = n_tok // CHUNK
    cb0 = row0 // CHUNK
    ng = A_QK_HEADS // A_HG
    nv = 2 * A_HG
    fwd = lambda s, j, c: cb0 + s * nc + c
    bwd = lambda s, j, c: cb0 + s * nc + (nc - 1 - c)
    k_blk, v_blk = A_QK // (A_HG * A_DK), 2 * A_QK // (nv * A_DV)

    def stream_specs(ix):
        return [pl.BlockSpec((CHUNK, A_HG * A_DK), lambda s, j, c: (ix(s, j, c), j)),
                pl.BlockSpec((CHUNK, A_HG * A_DK), lambda s, j, c: (ix(s, j, c), k_blk + j)),
                pl.BlockSpec((CHUNK, nv * A_DV), lambda s, j, c: (ix(s, j, c), v_blk + j)),
                pl.BlockSpec((CHUNK, LANES), lambda s, j, c: (ix(s, j, c), j)),
                pl.BlockSpec((1, 1, nv, CHUNK), lambda s, j, c: (ix(s, j, c), j, 0, 0))]

    st_spec = pl.BlockSpec((1, nv, A_DK, A_DV), lambda s, j, c: (s, j, 0, 0))
    st_shape = jax.ShapeDtypeStruct((n_seq, A_V_HEADS, A_DK, A_DV), F32)
    any_spec = pl.BlockSpec(memory_space=pl.ANY)
    return pl.pallas_call(
        _gdn_kernel,
        grid=(n_seq, ng, nc),
        in_specs=stream_specs(fwd) + stream_specs(bwd) + [st_spec, st_spec, any_spec, any_spec],
        out_specs=[pl.BlockSpec((CHUNK, nv * A_DV), lambda s, j, c: (fwd(s, j, c), j)),
                   pl.BlockSpec((CHUNK, nv * A_DV), lambda s, j, c: (bwd(s, j, c), j)),
                   st_spec, st_spec],
        out_shape=[jax.ShapeDtypeStruct(buf_f.shape, F32), jax.ShapeDtypeStruct(buf_b.shape, F32),
                   st_shape, st_shape],
        scratch_shapes=[pltpu.VMEM((2, nv, A_DK, A_DV), F32)],
        input_output_aliases={12: 0, 13: 1},
        compiler_params=_cparams(("parallel", "parallel", "arbitrary")),
        name="gdn_scan",
    )(qkv, qkv, qkv, colf, rowf, qkv, qkv, qkv, colb, rowb, s0f, s0b, buf_f, buf_b)


def _gla_block_rows(bc, rev):
    n = CHUNK // 16
    zero = jnp.zeros((1, bc.shape[1]), F32)
    if rev:
        nxt = [bc[16 * (b + 1):16 * (b + 1) + 1] for b in range(n - 1)] + [zero]
        own = [bc[16 * b:16 * b + 1] for b in range(n)]
        mid = bc[32:33]
    else:
        nxt = [zero] + [bc[16 * b - 1:16 * b] for b in range(1, n)]
        own = [bc[16 * b + 15:16 * b + 16] for b in range(n)]
        mid = bc[31:32]
    bcast = lambda rows: jnp.concatenate([jnp.broadcast_to(r, (16, bc.shape[1])) for r in rows], axis=0)
    return bcast(nxt), bcast(own), mid


def _gla_unit(q, k, v, lg, st, rev):
    tri, _, i, j = _tri_masks(rev)
    cum_mat = jnp.where(tri, 1.0, 0.0)
    bc = _dot_exact_lhs(cum_mat, lg)
    btot = bc[0:1] if rev else bc[CHUNK - 1:CHUNK]
    ref16, own16, mid = _gla_block_rows(bc, rev)
    ib, jb = jnp.right_shift(i, 4), jnp.right_shift(j, 4)
    qd = q * jnp.exp(bc - ref16)
    a = jnp.where(tri & (ib == jb), _dot_nt(qd, k * jnp.exp(ref16 - bc)), 0.0)
    pair = (ib == jb - 1) if rev else (ib == jb + 1)
    pair = pair & (jnp.bitwise_and(jnp.minimum(ib, jb), 1) == 0)
    a = a + jnp.where(pair, _dot_nt(qd, k * jnp.exp(own16 - bc)), 0.0)
    half = ((i < 32) & (j >= 32)) if rev else ((i >= 32) & (j < 32))
    q32 = q * jnp.exp(jnp.minimum(bc - mid, 0.0))
    k32 = k * jnp.exp(jnp.minimum(mid - bc, 0.0))
    a = a + jnp.where(half, _dot_nt(q32, k32), 0.0)
    o = _dot_nt(q * jnp.exp(bc), st) + _dot(a, v)
    st_new = st * jnp.exp(btot) + _dot_tn(v, k * jnp.exp(btot - bc))
    return o, st_new


def _gla_kernel(qf_ref, kf_ref, vf_ref, lf_ref, qb_ref, kb_ref, vb_ref, lb_ref, w2_ref, gb_ref,
                s0f_ref, s0b_ref, bf_buf, bb_buf, of_ref, ob_ref, sff_ref, sfb_ref, s_scr):
    del bf_buf, bb_buf
    c = pl.program_id(2)

    @pl.when(c == 0)
    def _():
        s_scr[0] = s0f_ref[0, 0]
        s_scr[1] = s0b_ref[0, 0]

    streams = ((qf_ref, kf_ref, vf_ref, lf_ref, of_ref, False),
               (qb_ref, kb_ref, vb_ref, lb_ref, ob_ref, True))
    for d, (q_ref, k_ref, v_ref, l_ref, o_ref, rev) in enumerate(streams):
        x = _dot(l_ref[...], w2_ref[d]) + gb_ref[d]
        lg = (jnp.minimum(x, 0.0) - jnp.log(1.0 + jnp.exp(-jnp.abs(x)))) * (1.0 / GATE_TAU)
        o, s_new = _gla_unit(q_ref[...] * D_DK ** -0.5, k_ref[...], v_ref[...], lg, s_scr[d], rev)
        o_ref[...] = o
        s_scr[d] = s_new

    @pl.when(c == pl.num_programs(2) - 1)
    def _():
        sff_ref[0, 0] = s_scr[0]
        sfb_ref[0, 0] = s_scr[1]


def _gla_scan(p, lr, w2, gb, s0f, s0b, buf_f, buf_b, n_seq, n_tok, row0):
    nc = n_tok // CHUNK
    cb0 = row0 // CHUNK
    fwd = lambda s, h, c: cb0 + s * nc + c
    bwd = lambda s, h, c: cb0 + s * nc + (nc - 1 - c)
    k_blk, v_blk = D_QKW // D_DK, 2 * D_QKW // D_DV

    def stream_specs(ix):
        return [pl.BlockSpec((CHUNK, D_DK), lambda s, h, c: (ix(s, h, c), h)),
                pl.BlockSpec((CHUNK, D_DK), lambda s, h, c: (ix(s, h, c), k_blk + h)),
                pl.BlockSpec((CHUNK, D_DV), lambda s, h, c: (ix(s, h, c), v_blk + h)),
                pl.BlockSpec((CHUNK, 2 * GATE_RANK), lambda s, h, c: (ix(s, h, c), 0))]

    st_spec = pl.BlockSpec((1, 1, D_DV, D_DK), lambda s, h, c: (s, h, 0, 0))
    st_shape = jax.ShapeDtypeStruct((n_seq, D_HEADS, D_DV, D_DK), F32)
    any_spec = pl.BlockSpec(memory_space=pl.ANY)
    return pl.pallas_call(
        _gla_kernel,
        grid=(n_seq, D_HEADS, nc),
        in_specs=stream_specs(fwd) + stream_specs(bwd) + [
            pl.BlockSpec((2, 2 * GATE_RANK, D_DK), lambda s, h, c: (0, 0, h)),
            pl.BlockSpec((2, 1, D_DK), lambda s, h, c: (0, 0, h)),
            st_spec, st_spec, any_spec, any_spec],
        out_specs=[pl.BlockSpec((CHUNK, D_DV), lambda s, h, c: (fwd(s, h, c), h)),
                   pl.BlockSpec((CHUNK, D_DV), lambda s, h, c: (bwd(s, h, c), h)),
                   st_spec, st_spec],
        out_shape=[jax.ShapeDtypeStruct(buf_f.shape, F32), jax.ShapeDtypeStruct(buf_b.shape, F32),
                   st_shape, st_shape],
        scratch_shapes=[pltpu.VMEM((2, D_DV, D_DK), F32)],
        input_output_aliases={12: 0, 13: 1},
        compiler_params=_cparams(("parallel", "parallel", "arbitrary")),
        name="gla_scan",
    )(p, p, p, lr, p, p, p, lr, w2, gb, s0f, s0b, buf_f, buf_b)


def _gdn_prep_kernel(x_ref, prev_ref, next_ref, edge_ref, w_ref, o_ref, *, starts, ends):
    i, j = pl.program_id(0), pl.program_id(1)
    tm, cw = x_ref.shape
    r0 = i * tm
    is_start = functools.reduce(jnp.logical_or, [r0 == s for s in starts])
    is_end = functools.reduce(jnp.logical_or, [r0 + tm == e for e in ends])
    x = x_ref[...]
    prev_row = jnp.where(is_start, edge_ref[0, 7:8, :], prev_ref[7:8, :])
    next_row = jnp.where(is_end, 0.0, next_ref[0:1, :])
    rows = lax.broadcasted_iota(jnp.int32, (tm, 1), 0)
    xp = jnp.where(rows == 0, prev_row, pltpu.roll(x, 1, axis=0))
    xn = jnp.where(rows == tm - 1, next_row, pltpu.roll(x, tm - 1, axis=0))
    y = xp * w_ref[0:1, :] + x * w_ref[1:2, :] + xn * w_ref[2:3, :]
    y = y * _sigmoid(y)
    is_qk = j < 2 * A_QK // cw
    q_scale = jnp.where(j < A_QK // cw, A_DK ** -0.5, 1.0)
    for c0 in range(0, cw, A_DK):
        yh = y[:, c0:c0 + A_DK]
        nrm = yh * (lax.rsqrt(jnp.sum(yh * yh, axis=-1, keepdims=True) + EPS) * q_scale)
        o_ref[:, c0:c0 + A_DK] = jnp.where(is_qk, nrm, yh)


def _gdn_prep(lay, x, edge, conv_w):
    r, c = x.shape
    tm, cw = 512, 512
    starts = [row + s * n for b, n, row, _ in lay.groups for s in range(b)]
    ends = [row + (s + 1) * n for b, n, row, _ in lay.groups for s in range(b)]
    seq_of = lambda i: sum((i * tm >= s).astype(jnp.int32) for s in starts) - 1
    w8 = jnp.pad(conv_w.astype(F32), ((0, 8 - conv_w.shape[0]), (0, 0)))
    return pl.pallas_call(
        functools.partial(_gdn_prep_kernel, starts=starts, ends=ends),
        grid=(r // tm, c // cw),
        in_specs=[pl.BlockSpec((tm, cw), lambda i, j: (i, j)),
                  pl.BlockSpec((8, cw), lambda i, j: (jnp.maximum(i * (tm // 8) - 1, 0), j)),
                  pl.BlockSpec((8, cw), lambda i, j: (jnp.minimum((i + 1) * (tm // 8), r // 8 - 1), j)),
                  pl.BlockSpec((1, 8, cw), lambda i, j: (seq_of(i), 0, j)),
                  pl.BlockSpec((8, cw), lambda i, j: (0, j))],
        out_specs=pl.BlockSpec((tm, cw), lambda i, j: (i, j)),
        out_shape=jax.ShapeDtypeStruct((r, c), F32),
        compiler_params=_cparams(("parallel", "parallel")),
        name="gdn_prep",
    )(x, x, x, edge, w8)


def _gqa_prep_kernel(q_ref, kv_ref, cos_ref, sin_ref, qg_ref, kg_ref, qt_ref, k_ref, vt_ref):
    cos, sin = cos_ref[...], sin_ref[...]

    def norm_rope(x, g):
        xn = x * lax.rsqrt(jnp.mean(x * x, axis=-1, keepdims=True) + EPS) * g
        return xn * cos + pltpu.roll(xn, B_DH // 2, axis=1) * sin

    for hd in range(B_HEADS):
        sl = slice(hd * B_DH, (hd + 1) * B_DH)
        qt_ref[sl, :] = norm_rope(q_ref[:, sl], qg_ref[...]).T.astype(BF16)
    for hd in range(B_KV_HEADS):
        sl = slice(hd * B_DH, (hd + 1) * B_DH)
        k_ref[:, sl] = norm_rope(kv_ref[:, sl], kg_ref[...]).astype(BF16)
        vt_ref[sl, :] = kv_ref[:, B_KVW + hd * B_DH:B_KVW + (hd + 1) * B_DH].T.astype(BF16)


def _pos_block(lay, tm):
    def pos(i):
        r = i * tm
        out = 0
        for b, n, row, _ in lay.groups:
            inside = (r >= row) & (r < row + b * n)
            out = out + jnp.where(inside, ((r - row) % n) // tm, 0)
        return out
    return pos


def _gqa_prep(lay, pq, pkv, cos, sin, qg, kg):
    r = pq.shape[0]
    tm = min(512, min(n for _, n, _, _ in lay.groups))
    pos = _pos_block(lay, tm)
    row = lambda i: (i, 0)
    col = lambda i: (0, i)
    fixed = lambda i: (0, 0)
    return pl.pallas_call(
        _gqa_prep_kernel,
        grid=(r // tm,),
        in_specs=[pl.BlockSpec((tm, B_QW), row), pl.BlockSpec((tm, 2 * B_KVW), row),
                  pl.BlockSpec((tm, B_DH), lambda i: (pos(i), 0)), pl.BlockSpec((tm, B_DH), lambda i: (pos(i), 0)),
                  pl.BlockSpec((1, B_DH), fixed), pl.BlockSpec((1, B_DH), fixed)],
        out_specs=[pl.BlockSpec((B_QW, tm), col), pl.BlockSpec((tm, B_KVW), row), pl.BlockSpec((B_KVW, tm), col)],
        out_shape=[jax.ShapeDtypeStruct((B_QW, r), BF16), jax.ShapeDtypeStruct((r, B_KVW), BF16),
                   jax.ShapeDtypeStruct((B_KVW, r), BF16)],
        compiler_params=_cparams(("parallel",)),
        name="gqa_prep",
    )(pq, pkv, cos, sin, qg.reshape(1, B_DH), kg.reshape(1, B_DH))


def _diff_prep_kernel(q_ref, k_ref, v_ref, qg_ref, kg_ref, qt_ref, ko_ref, vt_ref):
    lo = lax.broadcasted_iota(jnp.int32, (1, LANES), 1) < C_DQK

    def norm_maps(x, g):
        sq = x * x
        s_lo = jnp.sum(jnp.where(lo, sq, 0.0), axis=-1, keepdims=True)
        s_hi = jnp.sum(jnp.where(lo, 0.0, sq), axis=-1, keepdims=True)
        inv = jnp.where(lo, lax.rsqrt(s_lo * (1.0 / C_DQK) + EPS), lax.rsqrt(s_hi * (1.0 / C_DQK) + EPS))
        return x * inv * g

    qt_ref[...] = norm_maps(q_ref[...], qg_ref[...]).T.astype(BF16)
    ko_ref[...] = norm_maps(k_ref[...], kg_ref[...]).astype(BF16)
    vt_ref[...] = v_ref[...].T.astype(BF16)


def _diff_prep(p, qg, kg):
    r = p.shape[0]
    tm = 512 if r % 512 == 0 else LANES
    fixed = lambda i, h: (0, 0)
    return pl.pallas_call(
        _diff_prep_kernel,
        grid=(r // tm, C_HEADS),
        in_specs=[pl.BlockSpec((tm, LANES), lambda i, h: (i, h)),
                  pl.BlockSpec((tm, LANES), lambda i, h: (i, C_HEADS + h)),
                  pl.BlockSpec((tm, LANES), lambda i, h: (i, 2 * C_HEADS + h)),
                  pl.BlockSpec((1, LANES), fixed), pl.BlockSpec((1, LANES), fixed)],
        out_specs=[pl.BlockSpec((LANES, tm), lambda i, h: (h, i)), pl.BlockSpec((tm, LANES), lambda i, h: (i, h)),
                   pl.BlockSpec((LANES, tm), lambda i, h: (h, i))],
        out_shape=[jax.ShapeDtypeStruct((C_QKW, r), BF16), jax.ShapeDtypeStruct((r, C_QKW), BF16),
                   jax.ShapeDtypeStruct((C_VW, r), BF16)],
        compiler_params=_cparams(("parallel", "parallel")),
        name="diff_prep",
    )(p, p, p, qg.reshape(1, LANES), kg.reshape(1, LANES))


class _Layout:
    def __init__(self, shapes):
        self.groups = []
        row, seq = 0, 0
        for b, n in shapes:
            self.groups.append((b, n, row, seq))
            row += b * n
            seq += b
        self.rows, self.n_seq = row, seq

    def to_seqs(self, main, meta):
        out = []
        c = main.shape[-1]
        for b, n, row, seq in self.groups:
            m = meta[seq * N_META:(seq + b) * N_META].reshape(b, N_META, c)
            x = main[row:row + b * n].reshape(b, n, c)
            out.append(jnp.concatenate([m, x], axis=1))
        return out

    def from_seqs(self, seqs):
        main = jnp.concatenate([s[:, N_META:].reshape(-1, s.shape[-1]) for s in seqs], axis=0)
        meta = jnp.concatenate([s[:, :N_META].reshape(-1, s.shape[-1]) for s in seqs], axis=0)
        return main, meta


def _meta_chunk(meta):
    c = meta.shape[-1]
    m = meta.reshape(-1, N_META, c)
    return jnp.pad(m, ((0, 0), (CHUNK - N_META, 0), (0, 0))).reshape(-1, c)


def _meta_unchunk(x):
    c = x.shape[-1]
    return x.reshape(-1, CHUNK, c)[:, CHUNK - N_META:].reshape(-1, c)


def _map_seqs(lay, fn, main, meta):
    return lay.from_seqs([fn(s) for s in lay.to_seqs(main, meta)])


def _run_scan(scan, lay, main_args, meta_args, state_shape, width):
    s_all = lay.n_seq
    zeros = jnp.zeros((s_all,) + state_shape, F32)
    mrows = s_all * CHUNK
    mbuf = jnp.zeros((mrows, width), F32)
    _, _, s_meta, _ = scan(*meta_args, zeros, zeros, mbuf, mbuf, s_all, CHUNK, 0)
    of = jnp.zeros((lay.rows, width), F32)
    ob = jnp.zeros((lay.rows, width), F32)
    finals = []
    for b, n, row, seq in lay.groups:
        of, ob, _, sfb = scan(*main_args, s_meta[seq:seq + b], zeros[seq:seq + b], of, ob, b, n, row)
        finals.append(sfb)
    mof, mob, _, _ = scan(*meta_args, zeros, jnp.concatenate(finals, axis=0), mbuf, mbuf, s_all, CHUNK, 0)
    return (of, ob), (_meta_unchunk(mof), _meta_unchunk(mob))


def _gdn_layer(lay, h, hm, norm_g, w_in, conv_w, a_log, dt_bias, out_g, w_out):
    w_qkv, w_z, w_ab = (w_in[:, :A_CONV_CH], w_in[:, A_CONV_CH:A_CONV_CH + A_VW], w_in[:, A_CONV_CH + A_VW:])
    x, xm = _norm_proj(h, norm_g, w_qkv), _norm_proj(hm, norm_g, w_qkv)
    z, zm = _norm_proj(h, norm_g, w_z), _norm_proj(hm, norm_g, w_z)
    ab, abm = _norm_proj(h, norm_g, w_ab), _norm_proj(hm, norm_g, w_ab)
    s_all = lay.n_seq
    xm3 = xm.reshape(s_all, N_META, A_CONV_CH)
    qkv = _gdn_prep(lay, x, xm3[:, N_META - 8:], conv_w)

    def prep(x):
        prev = jnp.pad(x[:, :-1], ((0, 0), (1, 0), (0, 0)))
        nxt = jnp.pad(x[:, 1:], ((0, 0), (0, 1), (0, 0)))
        y = jax.nn.silu(prev * conv_w[0] + x * conv_w[1] + nxt * conv_w[2])
        b, l, _ = y.shape
        qk = y[..., :2 * A_QK].reshape(b, l, 2 * A_QK_HEADS, A_DK)
        qk = qk * lax.rsqrt(jnp.sum(qk * qk, axis=-1, keepdims=True) + EPS)
        qk = qk.reshape(b, l, 2, A_QK) * jnp.array([A_DK ** -0.5, 1.0], F32)[:, None]
        return jnp.concatenate([qk.reshape(b, l, 2 * A_QK), y[..., 2 * A_QK:]], axis=-1)

    first = x[np.array([row + s * n for b, n, row, _ in lay.groups for s in range(b)])]
    qkvm = prep(jnp.concatenate([xm3, first[:, None]], axis=1))[:, :N_META].reshape(-1, A_CONV_CH)

    def gates(x):
        x = x.reshape(-1, 2, 2, A_V_HEADS)
        g = -jnp.exp(a_log.astype(F32)) * jax.nn.softplus(x[:, 0] + dt_bias.astype(F32))
        return g, jax.nn.sigmoid(x[:, 1])

    ng, nv = A_QK_HEADS // A_HG, 2 * A_HG

    def pack(g, beta, d):
        r = g.shape[0]
        gd, bd = g[:, d].reshape(r, ng, nv), beta[:, d].reshape(r, ng, nv)
        col = jnp.concatenate([gd, bd, jnp.zeros((r, ng, LANES - 2 * nv), F32)], axis=-1).reshape(r, ng * LANES)
        row = gd.reshape(r // CHUNK, CHUNK, ng, nv).transpose(0, 2, 3, 1)
        return col, row

    def scan_args(qkv_rows, g, beta):
        colf, rowf = pack(g, beta, 0)
        colb, rowb = pack(g, beta, 1)
        return (qkv_rows, colf, colb, rowf, rowb)

    g, beta = gates(ab)
    gm, betam = gates(abm)
    pad_gate = lambda t: _meta_chunk(t.reshape(-1, 2 * A_V_HEADS)).reshape(-1, 2, A_V_HEADS)
    (of, ob), (mof, mob) = _run_scan(
        _gdn_scan, lay, scan_args(qkv, g, beta), scan_args(_meta_chunk(qkvm), pad_gate(gm), pad_gate(betam)),
        (A_V_HEADS, A_DK, A_DV), A_VW)
    gain = jnp.tile(out_g.astype(F32), A_V_HEADS)
    return (_out_proj([of, ob], z, gain, w_out, h, A_DV), _out_proj([mof, mob], zm, gain, w_out, hm, A_DV))


def _rope_tables(n_tok):
    rows = n_tok // GRID_W
    row = jnp.repeat(jnp.arange(rows, dtype=F32), GRID_W)
    col = (jnp.arange(rows * GRID_W) % GRID_W).astype(F32)
    inv = ROPE_THETA ** (-jnp.arange(0, ROPE_AXIS, 2, dtype=F32) / ROPE_AXIS)
    ang = jnp.concatenate([row[:, None] * inv, col[:, None] * inv], axis=-1)
    return jnp.cos(ang), jnp.sin(ang)


def _pad_meta(x, n_seq):
    return jnp.pad(x.reshape(n_seq, N_META, -1), ((0, 0), (0, LANES - N_META), (0, 0)))


def _gqa_layer(lay, h, hm, norm_g, w_in, q_g, k_g, w_out):
    kv_end = B_QW + 2 * B_KVW
    pq, pkv, z = (_norm_proj(h, norm_g, w_in[:, :B_QW]), _norm_proj(h, norm_g, w_in[:, B_QW:kv_end]),
                  _norm_proj(h, norm_g, w_in[:, kv_end:]))
    pm = _norm_proj(hm, norm_g, w_in)

    def norm_heads(x, g):
        r = x.shape[0]
        x = x.reshape(r, -1, B_DH)
        return x * lax.rsqrt(jnp.mean(x * x, axis=-1, keepdims=True) + EPS) * g.astype(F32)

    scale = B_DH ** -0.5 * LOG2E
    cos, sin = _rope_tables(max(n for _, n, _, _ in lay.groups))
    qt, k, vt = _gqa_prep(lay, pq, pkv, jnp.concatenate([cos, cos], axis=-1), jnp.concatenate([-sin, sin], axis=-1),
                          q_g.astype(F32) * scale, k_g.astype(F32))
    s_all = lay.n_seq
    qm = (norm_heads(pm[:, :B_QW], q_g) * scale).reshape(-1, B_QW).astype(BF16)
    km = norm_heads(pm[:, B_QW:B_QW + B_KVW], k_g).reshape(-1, B_KVW).astype(BF16)
    vm = pm[:, B_QW + B_KVW:B_QW + 2 * B_KVW].astype(BF16)
    zm = pm[:, B_QW + 2 * B_KVW:]
    qmt = _pad_meta(qm, s_all).reshape(s_all * LANES, B_QW).T
    kmp, vmt = _pad_meta(km, s_all), _pad_meta(vm, s_all).transpose(0, 2, 1)

    o = jnp.zeros((lay.rows, B_QW), F32)
    om = jnp.zeros((s_all * LANES, B_QW), F32)
    for b, n, row, seq in lay.groups:
        tq, tk = min(GQA_TQ, n), min(GQA_TK, n)
        o = _gqa_attention(qt, k, vt, kmp[seq:seq + b], vmt[seq:seq + b], o, b, n, n, row, row, tq, tk)
        om = _gqa_attention(qmt, k, vt, kmp[seq:seq + b], vmt[seq:seq + b], om, b, LANES, n,
                            seq * LANES, row, LANES, tk)
    om = om.reshape(s_all, LANES, B_QW)[:, :N_META].reshape(-1, B_QW)
    ones = jnp.ones((B_QW,), F32)
    return _out_proj([o], z, ones, w_out, h, 0), _out_proj([om], zm, ones, w_out, hm, 0)


def _t5_bucket_np(rel):
    nb = REL_BUCKETS // 2
    max_exact = nb // 2
    n = np.abs(rel)
    steps = nb - max_exact
    large = np.full(n.shape, max_exact, np.int64)
    for kstep in range(1, steps + 1):
        large += (n.astype(np.float64) ** steps >= (max_exact ** steps) * float(REL_MAX_DIST // max_exact) ** kstep)
    large = np.minimum(large, nb - 1)
    return np.where(rel > 0, nb, 0) + np.where(n < max_exact, n, large)


def _diff_layer(lay, h, hm, norm_g, w_in, q_g, k_g, lam, sub_g, w_out, rel_bias, layer_idx):
    n_qkv = 2 * C_QKW + C_VW
    p, z = _norm_proj(h, norm_g, w_in[:, :n_qkv]), _norm_proj(h, norm_g, w_in[:, n_qkv:])
    pm = _norm_proj(hm, norm_g, w_in)

    def norm_maps(x, g):
        r = x.shape[0]
        x = x.reshape(r, -1, C_DQK)
        return (x * lax.rsqrt(jnp.mean(x * x, axis=-1, keepdims=True) + EPS) * g.astype(F32)).reshape(r, -1)

    s_all = lay.n_seq
    scale = C_DQK ** -0.5 * LOG2E
    qt, k, vt = _diff_prep(p, jnp.tile(q_g.astype(F32), 2) * scale, jnp.tile(k_g.astype(F32), 2))
    qm = (norm_maps(pm[:, :C_QKW], q_g) * scale).astype(BF16)
    km = norm_maps(pm[:, C_QKW:2 * C_QKW], k_g).astype(BF16)
    vm = pm[:, 2 * C_QKW:n_qkv].astype(BF16)
    zm = pm[:, n_qkv:]
    qmt = _pad_meta(qm, s_all).reshape(s_all * LANES, C_QKW).T
    kmp, vmt = _pad_meta(km, s_all), _pad_meta(vm, s_all).transpose(0, 2, 1)

    lam = lam.astype(F32)
    lam_init = 0.8 - 0.6 * math.exp(-0.3 * layer_idx)
    lam_full = jnp.exp(jnp.sum(lam[0] * lam[1])) - jnp.exp(jnp.sum(lam[2] * lam[3])) + lam_init
    table = rel_bias.astype(F32).T * LOG2E
    look = lambda rel: table[:, _t5_bucket_np(rel)]
    nb = REL_BUCKETS // 2
    cst = jnp.zeros((C_HEADS, 8, LANES), F32)
    cst = cst.at[:, 0].set(table[:, nb - 1:nb]).at[:, 1].set(table[:, 2 * nb - 1:2 * nb]).at[:, 2].set(lam_full)
    ar = np.arange(LANES)
    key_pad = jnp.asarray(np.where(ar < N_META, 0.0, NEG), F32)[:, None]
    tt = jnp.stack([look(o * LANES + ar[:, None] - ar[None, :]) for o in (-1, 0, 1)], axis=1)
    mq = np.minimum(ar, N_META - 1)
    tt_m = jnp.stack([look(N_META + ar[:, None] - mq[None, :])] * 3, axis=1)
    meta_rows = (ar < N_META)[:, None]
    bm_m = (jnp.where(meta_rows, look(ar[:, None] - mq[None, :]), 0.0) + key_pad)[:, None]

    o = jnp.zeros((lay.rows, C_VW), F32)
    om = jnp.zeros((s_all * LANES, C_VW), F32)
    for b, n, row, seq in lay.groups:
        t, tk = min(DIFF_TQ, n), min(DIFF_TK, n)
        pq = np.arange(t)
        near = jnp.where(meta_rows, look(ar[:, None] - N_META - pq[None, :]), 0.0) + key_pad
        far = jnp.broadcast_to(jnp.where(meta_rows, table[:, nb - 1][:, None, None], 0.0) + key_pad,
                               (C_HEADS, LANES, t))
        bm = jnp.stack([near, far], axis=1)
        o = _diff_attention(qt, k, vt, kmp[seq:seq + b], vmt[seq:seq + b], tt, bm, cst, o,
                            b, n, n, row, row, t, tk, False)
        om = _diff_attention(qmt, k, vt, kmp[seq:seq + b], vmt[seq:seq + b], tt_m, bm_m, cst, om,
                             b, LANES, n, seq * LANES, row, LANES, tk, True)
    om = om.reshape(s_all, LANES, C_VW)[:, :N_META].reshape(-1, C_VW)
    gain = jnp.tile(sub_g.astype(F32), C_HEADS) * (1.0 - lam_init)
    return _out_proj([o], z, gain, w_out, h, C_DV), _out_proj([om], zm, gain, w_out, hm, C_DV)


def _gla_layer(lay, h, hm, norm_g, w_in, gate_w2, gate_b, out_g, w_out):
    n_qkv = 2 * D_QKW + D_VW
    w_qkv, w_z, w_lr = w_in[:, :n_qkv], w_in[:, n_qkv:n_qkv + D_VW], w_in[:, n_qkv + D_VW:]
    p, pm = _norm_proj(h, norm_g, w_qkv), _norm_proj(hm, norm_g, w_qkv)
    z, zm = _norm_proj(h, norm_g, w_z), _norm_proj(hm, norm_g, w_z)
    lr, lrm = _norm_proj(h, norm_g, w_lr), _norm_proj(hm, norm_g, w_lr)
    zero = jnp.zeros((GATE_RANK, D_QKW), F32)
    w2 = jnp.stack([jnp.concatenate([gate_w2[0].astype(F32), zero], axis=0),
                    jnp.concatenate([zero, gate_w2[1].astype(F32)], axis=0)])
    gb = gate_b.astype(F32).reshape(2, 1, D_QKW)
    (of, ob), (mof, mob) = _run_scan(
        _gla_scan, lay, (p, lr, w2, gb), (_meta_chunk(pm), _meta_chunk(lrm), w2, gb), (D_HEADS, D_DV, D_DK), D_VW)
    gain = jnp.tile(out_g.astype(F32), D_HEADS)
    return _out_proj([of, ob], z, gain, w_out, h, D_DV), _out_proj([mof, mob], zm, gain, w_out, hm, D_DV)


def _trunk(xs, meta_tokens, rel_bias, a, b, c, d, depth):
    lay = _Layout([(x.shape[0], x.shape[1]) for x in xs])
    h = jnp.concatenate([x.reshape(-1, D_MODEL) for x in xs], axis=0)
    hm = jnp.tile(meta_tokens.astype(F32), (lay.n_seq, 1))
    for i in range(depth):
        m, j = i % 4, i // 4
        if m == 0:
            h, hm = _gdn_layer(lay, h, hm, *(t[j] for t in a))
        elif m == 1:
            h, hm = _gqa_layer(lay, h, hm, *(t[j] for t in b))
        elif m == 2:
            h, hm = _diff_layer(lay, h, hm, *(t[j] for t in c), rel_bias, i)
        else:
            h, hm = _gla_layer(lay, h, hm, *(t[j] for t in d))
    return [h[row:row + bsz * n].reshape(bsz, n, D_MODEL) for bsz, n, row, _ in lay.groups]


def kernel(x_prompt, x_sample, meta_tokens, rel_bias, a_norm, a_w_in, a_conv, a_a_log, a_dt_bias, a_out_norm, a_w_out, b_norm, b_w_in, b_q_norm, b_k_norm, b_w_out, c_norm, c_w_in, c_q_norm, c_k_norm, c_lambda, c_sub_norm, c_w_out, d_norm, d_w_in, d_gate_w2, d_gate_b, d_out_norm, d_w_out):
    y_prompt, y_sample = _trunk(
        [x_prompt, x_sample], meta_tokens, rel_bias,
        (a_norm, a_w_in, a_conv, a_a_log, a_dt_bias, a_out_norm, a_w_out),
        (b_norm, b_w_in, b_q_norm, b_k_norm, b_w_out),
        (c_norm, c_w_in, c_q_norm, c_k_norm, c_lambda, c_sub_norm, c_w_out),
        (d_norm, d_w_in, d_gate_w2, d_gate_b, d_out_norm, d_w_out), 4)
    return (y_prompt, y_sample)
```

```python
import functools
import math

import numpy as np
import jax
import jax.numpy as jnp
from jax import lax
from jax.experimental import pallas as pl
from jax.experimental.pallas import tpu as pltpu

F32 = jnp.float32
BF16 = jnp.bfloat16

D_MODEL = 1024
N_META = 16
CHUNK = 64
GRID_W = 64
EPS = 1e-6
LANES = 128
VMEM_LIMIT = 56 * 1024 * 1024
NEG = -1e30
LOG2E = math.log2(math.e)
STRIP = 1024
ONES_ROWS = 16
PROJ_TN_MAX = 2048
GQA_TQ, GQA_TK = 1024, 1024
DIFF_TQ, DIFF_TK = 2048, 1024

A_QK_HEADS, A_V_HEADS, A_DK, A_DV = 8, 16, 128, 128
A_QK, A_VW = A_QK_HEADS * A_DK, A_V_HEADS * A_DV
A_CONV_CH = 2 * A_QK + A_VW
A_HG = 8
B_HEADS, B_KV_HEADS, B_DH = 8, 2, 128
B_GROUP = B_HEADS // B_KV_HEADS
B_QW, B_KVW = B_HEADS * B_DH, B_KV_HEADS * B_DH
ROPE_AXIS = B_DH // 2
ROPE_THETA = 10000.0
C_HEADS, C_DQK, C_DV = 8, 64, 128
C_QKW, C_VW = C_HEADS * 2 * C_DQK, C_HEADS * C_DV
REL_BUCKETS, REL_MAX_DIST = 32, 128
D_HEADS, D_DK, D_DV = 4, 128, 256
D_QKW, D_VW = D_HEADS * D_DK, D_HEADS * D_DV
GATE_RANK, GATE_TAU = 16, 16.0


def _cparams(sem):
    return pltpu.CompilerParams(dimension_semantics=sem, vmem_limit_bytes=VMEM_LIMIT)


def _sigmoid(x):
    return 1.0 / (1.0 + jnp.exp(-x))


def _dot(a, b):
    return jnp.dot(a.astype(BF16), b.astype(BF16), preferred_element_type=F32)


def _dot_nt(a, b):
    return lax.dot_general(a.astype(BF16), b.astype(BF16), (((1,), (1,)), ((), ())),
                           preferred_element_type=F32)


def _dot_tn(a, b):
    return lax.dot_general(a.astype(BF16), b.astype(BF16), (((0,), (0,)), ((), ())),
                           preferred_element_type=F32)


def _split3(x):
    x1 = x.astype(BF16)
    r1 = x - x1.astype(F32)
    x2 = r1.astype(BF16)
    x3 = (r1 - x2.astype(F32)).astype(BF16)
    return x1, x2, x3


def _dot_exact_lhs(m01, x):
    m = m01.astype(BF16)
    x1, x2, x3 = _split3(x)
    return (jnp.dot(m, x1, preferred_element_type=F32) + jnp.dot(m, x2, preferred_element_type=F32)
            + jnp.dot(m, x3, preferred_element_type=F32))


def _dot_exact_rhs(x, m01):
    m = m01.astype(BF16)
    x1, x2, x3 = _split3(x)
    return (jnp.dot(x1, m, preferred_element_type=F32) + jnp.dot(x2, m, preferred_element_type=F32)
            + jnp.dot(x3, m, preferred_element_type=F32))


def _norm_proj_kernel(x_ref, g_ref, w_ref, o_ref, xn_ref):
    @pl.when(pl.program_id(1) == 0)
    def _():
        x = x_ref[...]
        ms = jnp.mean(x * x, axis=-1, keepdims=True)
        xn_ref[...] = (x * lax.rsqrt(ms + EPS) * g_ref[...]).astype(BF16)

    o_ref[...] = jnp.dot(xn_ref[...], w_ref[...], preferred_element_type=F32)


def _norm_proj(x, g, w):
    m, k = x.shape
    n = w.shape[1]
    tm = 512 if m % 512 == 0 else m
    tn = max([t for t in range(256, PROJ_TN_MAX + 1, 256) if n % t == 0], default=n)
    return pl.pallas_call(
        _norm_proj_kernel,
        grid=(m // tm, n // tn),
        in_specs=[pl.BlockSpec((tm, k), lambda i, j: (i, 0)),
                  pl.BlockSpec((1, k), lambda i, j: (0, 0)),
                  pl.BlockSpec((k, tn), lambda i, j: (0, j))],
        out_specs=pl.BlockSpec((tm, tn), lambda i, j: (i, j)),
        out_shape=jax.ShapeDtypeStruct((m, n), F32),
        scratch_shapes=[pltpu.VMEM((tm, k), BF16)],
        compiler_params=_cparams(("parallel", "arbitrary")),
        name="norm_proj",
    )(x, g.reshape(1, k).astype(F32), w.astype(BF16))


def _out_proj_kernel(*refs, n_o, hw):
    o_refs = refs[:n_o]
    z_ref, g_ref, w_ref, h_ref, out_ref = refs[n_o:]
    k = z_ref.shape[1]
    cw = hw if hw else 256
    acc = h_ref[...]
    for c0 in range(0, k, cw):
        o = o_refs[0][:, c0:c0 + cw]
        for r in o_refs[1:]:
            o = o + r[:, c0:c0 + cw]
        if hw:
            ms = jnp.mean(o * o, axis=-1, keepdims=True)
            o = o * lax.rsqrt(ms + EPS) * g_ref[:, c0:c0 + cw]
        z = z_ref[:, c0:c0 + cw]
        gated = o * (z * _sigmoid(z))
        acc = acc + jnp.dot(gated.astype(BF16), w_ref[c0:c0 + cw, :], preferred_element_type=F32)
    out_ref[...] = acc


def _out_proj(os, z, g, w, h, hw):
    m, k = z.shape
    n = w.shape[1]
    tm = (512 if k <= 1024 else 256) if m % 512 == 0 else m
    row = lambda i: (i, 0)
    fixed = lambda i: (0, 0)
    return pl.pallas_call(
        functools.partial(_out_proj_kernel, n_o=len(os), hw=hw),
        grid=(m // tm,),
        in_specs=[pl.BlockSpec((tm, k), row) for _ in os] + [
            pl.BlockSpec((tm, k), row), pl.BlockSpec((1, k), fixed),
            pl.BlockSpec((k, n), fixed), pl.BlockSpec((tm, n), row)],
        out_specs=pl.BlockSpec((tm, n), row),
        out_shape=jax.ShapeDtypeStruct((m, n), F32),
        compiler_params=_cparams(("parallel",)),
        name="out_proj",
    )(*os, z, g.reshape(1, k).astype(F32), w.astype(BF16), h)


def _strip_update(s_t, c, v_ext, m_scr, l_scr, acc_scr, cols):
    dv = acc_scr.shape[0]
    m_prev = m_scr[:, cols]
    m_new = jnp.maximum(m_prev, jnp.max(s_t, axis=0, keepdims=True) + c)
    alpha = jnp.exp2(m_prev - m_new)
    p = jnp.exp2(s_t - (m_new - c)).astype(BF16)
    pv = jnp.dot(v_ext, p, preferred_element_type=F32)
    acc_scr[:, cols] = alpha * acc_scr[:, cols] + pv[:dv]
    l_scr[:, cols] = alpha * l_scr[:, cols] + pv[dv:dv + 1]
    m_scr[:, cols] = m_new


def _attn_init(m_scr, l_scr, acc_scr):
    m_scr[...] = jnp.full(m_scr.shape, NEG, F32)
    l_scr[...] = jnp.zeros(l_scr.shape, F32)
    acc_scr[...] = jnp.zeros(acc_scr.shape, F32)


def _with_ones(v_t):
    return jnp.concatenate([v_t, jnp.ones((ONES_ROWS, v_t.shape[1]), BF16)], axis=0)


def _strips(width):
    w = min(STRIP, width)
    return [slice(c0, c0 + w) for c0 in range(0, width, w)]


def _gqa_kernel(qt_ref, k_ref, vt_ref, km_ref, vmt_ref, buf_ref, o_ref, m_scr, l_scr, acc_scr):
    del buf_ref
    ki = pl.program_id(3)
    tq = qt_ref.shape[1]
    heads = [slice(g * B_DH, (g + 1) * B_DH) for g in range(B_GROUP)]

    @pl.when(ki == 0)
    def _():
        _attn_init(m_scr, l_scr, acc_scr)
        key = lax.broadcasted_iota(jnp.int32, (LANES, 1), 0)
        pad = jnp.where(key < N_META, 0.0, NEG)
        vm_ext = _with_ones(vmt_ref[0])
        for g, rows in enumerate(heads):
            for cols in _strips(tq):
                s_t = jnp.dot(km_ref[0], qt_ref[rows, cols], preferred_element_type=F32) + pad
                _strip_update(s_t, 0.0, vm_ext, m_scr, l_scr, acc_scr,
                              slice(g * tq + cols.start, g * tq + cols.stop))

    k = k_ref[...]
    v_ext = _with_ones(vt_ref[...])
    work = [(g, rows, cols) for g, rows in enumerate(heads) for cols in _strips(tq)]
    scores = lambda w: jnp.dot(k, qt_ref[w[1], w[2]], preferred_element_type=F32)
    s_next = scores(work[0])
    for n, (g, rows, cols) in enumerate(work):
        s_t, s_next = s_next, (scores(work[n + 1]) if n + 1 < len(work) else None)
        _strip_update(s_t, 0.0, v_ext, m_scr, l_scr, acc_scr,
                      slice(g * tq + cols.start, g * tq + cols.stop))

    @pl.when(ki == pl.num_programs(3) - 1)
    def _():
        for g, rows in enumerate(heads):
            cols = slice(g * tq, (g + 1) * tq)
            o_ref[:, rows] = (acc_scr[:, cols] / l_scr[:, cols]).T


def _gqa_attention(qt, k, vt, km, vmt, buf, n_seq, n_q, n_k, q_col0, k_row0, tq, tk):
    nq, nk = n_q // tq, n_k // tk
    qb0, kb0 = q_col0 // tq, k_row0 // tk
    r = B_GROUP * tq
    gw = B_GROUP * B_DH
    return pl.pallas_call(
        _gqa_kernel,
        grid=(n_seq, B_KV_HEADS, nq, nk),
        in_specs=[pl.BlockSpec((gw, tq), lambda s, h, i, j: (h, qb0 + s * nq + i)),
                  pl.BlockSpec((tk, B_DH), lambda s, h, i, j: (kb0 + s * nk + j, h)),
                  pl.BlockSpec((B_DH, tk), lambda s, h, i, j: (h, kb0 + s * nk + j)),
                  pl.BlockSpec((1, LANES, B_DH), lambda s, h, i, j: (s, 0, h)),
                  pl.BlockSpec((1, B_DH, LANES), lambda s, h, i, j: (s, h, 0)),
                  pl.BlockSpec(memory_space=pl.ANY)],
        out_specs=pl.BlockSpec((tq, gw), lambda s, h, i, j: (qb0 + s * nq + i, h)),
        out_shape=jax.ShapeDtypeStruct(buf.shape, F32),
        scratch_shapes=[pltpu.VMEM((1, r), F32), pltpu.VMEM((1, r), F32), pltpu.VMEM((B_DH, r), F32)],
        input_output_aliases={5: 0},
        compiler_params=_cparams(("parallel", "parallel", "parallel", "arbitrary")),
        name="gqa_attention",
    )(qt, k, vt, km, vmt, buf)


def _diff_kernel(qt_ref, k_ref, vt_ref, km_ref, vmt_ref, tt_ref, bm_ref, cst_ref, buf_ref, o_ref,
                 m_scr, l_scr, acc_scr, *, meta_q):
    del buf_ref
    qi, ki = pl.program_id(2), pl.program_id(3)
    tq, tk = qt_ref.shape[1], k_ref.shape[0]
    nt = tk // LANES
    qt = qt_ref[...]
    dim = lax.broadcasted_iota(jnp.int32, qt.shape, 0)
    zero = jnp.zeros_like(qt)
    maps = (jnp.where(dim < C_DQK, qt, zero), jnp.where(dim >= C_DQK, qt, zero))
    c_left = cst_ref[0, 0:1, 0:1]
    c_right = cst_ref[0, 1:2, 0:1]
    lam = cst_ref[0, 2:3, 0:1]
    strips = [(mp, cols) for mp in range(2) for cols in _strips(tq)]
    scr = lambda mp, cols: slice(mp * tq + cols.start, mp * tq + cols.stop)

    @pl.when(ki == 0)
    def _():
        _attn_init(m_scr, l_scr, acc_scr)
        vm_ext = _with_ones(vmt_ref[0])
        for mp, cols in strips:
            s_t = jnp.dot(km_ref[0], maps[mp][:, cols], preferred_element_type=F32) + bm_ref[0, 0, :, cols]
            _strip_update(s_t, 0.0, vm_ext, m_scr, l_scr, acc_scr, scr(mp, cols))

    k = k_ref[...]
    v_ext = _with_ones(vt_ref[...])

    def tile_bias(off):
        if off <= -2:
            return jnp.broadcast_to(c_left, (LANES, LANES))
        if off >= 2:
            return jnp.broadcast_to(c_right, (LANES, LANES))
        return tt_ref[0, off + 1]

    def strip_bias(dd, cols):
        q_tiles = range(cols.start // LANES, cols.stop // LANES)
        if meta_q:
            rows = [[tt_ref[0, 0]]] + [[tile_bias(2)] for _ in range(nt - 1)]
        else:
            rows = [[tile_bias(dd * nt + b - a) for a in q_tiles] for b in range(nt)]
        return jnp.concatenate([jnp.concatenate(r, axis=1) for r in rows], axis=0)

    def run(c, bias):
        def scores(mp, cols):
            s_t = jnp.dot(k, maps[mp][:, cols], preferred_element_type=F32)
            return s_t if bias is None else s_t + bias(cols)

        s_next = scores(*strips[0])
        for n, (mp, cols) in enumerate(strips):
            s_t, s_next = s_next, (scores(*strips[n + 1]) if n + 1 < len(strips) else None)
            _strip_update(s_t, c, v_ext, m_scr, l_scr, acc_scr, scr(mp, cols))

    ratio = max(tq // tk, 1)
    d = ki - qi * ratio
    specials = (0,) if meta_q else tuple(range(-1, ratio + 1))
    for dd in specials:
        @pl.when(d == dd)
        def _(dd=dd):
            run(0.0, functools.partial(strip_bias, dd))

    @pl.when((d < specials[0]) | (d > specials[-1]))
    def _():
        run(jnp.where(d < 0, c_left, c_right), None)

    @pl.when(ki == pl.num_programs(3) - 1)
    def _():
        o1 = acc_scr[:, :tq] / l_scr[:, :tq]
        o2 = acc_scr[:, tq:] / l_scr[:, tq:]
        o_ref[...] = (o1 - lam * o2).T


def _diff_attention(qt, k, vt, km, vmt, tt, bm, cst, buf, n_seq, n_q, n_k, q_col0, k_row0, tq, tk, meta_q):
    nq, nk = n_q // tq, n_k // tk
    qb0, kb0 = q_col0 // tq, k_row0 // tk
    r = 2 * tq
    return pl.pallas_call(
        functools.partial(_diff_kernel, meta_q=meta_q),
        grid=(n_seq, C_HEADS, nq, nk),
        in_specs=[pl.BlockSpec((LANES, tq), lambda s, h, i, j: (h, qb0 + s * nq + i)),
                  pl.BlockSpec((tk, LANES), lambda s, h, i, j: (kb0 + s * nk + j, h)),
                  pl.BlockSpec((C_DV, tk), lambda s, h, i, j: (h, kb0 + s * nk + j)),
                  pl.BlockSpec((1, LANES, LANES), lambda s, h, i, j: (s, 0, h)),
                  pl.BlockSpec((1, C_DV, LANES), lambda s, h, i, j: (s, h, 0)),
                  pl.BlockSpec((1, 3, LANES, LANES), lambda s, h, i, j: (h, 0, 0, 0)),
                  pl.BlockSpec((1, 1, LANES, tq), lambda s, h, i, j: (h, jnp.minimum(i, 1), 0, 0)),
                  pl.BlockSpec((1, 8, LANES), lambda s, h, i, j: (h, 0, 0)),
                  pl.BlockSpec(memory_space=pl.ANY)],
        out_specs=pl.BlockSpec((tq, C_DV), lambda s, h, i, j: (qb0 + s * nq + i, h)),
        out_shape=jax.ShapeDtypeStruct(buf.shape, F32),
        scratch_shapes=[pltpu.VMEM((1, r), F32), pltpu.VMEM((1, r), F32), pltpu.VMEM((C_DV, r), F32)],
        input_output_aliases={8: 0},
        compiler_params=_cparams(("parallel", "parallel", "parallel", "arbitrary")),
        name="diff_attention",
    )(qt, k, vt, km, vmt, tt, bm, cst, buf)


def _tri_masks(rev):
    i = lax.broadcasted_iota(jnp.int32, (CHUNK, CHUNK), 0)
    j = lax.broadcasted_iota(jnp.int32, (CHUNK, CHUNK), 1)
    if rev:
        return j >= i, j > i, i, j
    return j <= i, j < i, i, j


def _cum_mat_t(rev):
    t = lax.broadcasted_iota(jnp.int32, (CHUNK, CHUNK), 0)
    j = lax.broadcasted_iota(jnp.int32, (CHUNK, CHUNK), 1)
    return jnp.where((t >= j) if rev else (t <= j), 1.0, 0.0)


def _gdn_kernel(qf_ref, kf_ref, vf_ref, cf_ref, rf_ref, qb_ref, kb_ref, vb_ref, cb_ref, rb_ref,
                s0f_ref, s0b_ref, bf_buf, bb_buf, of_ref, ob_ref, sff_ref, sfb_ref, s_scr):
    del bf_buf, bb_buf
    c = pl.program_id(2)
    nv = 2 * A_HG

    @pl.when(c == 0)
    def _():
        s_scr[0] = s0f_ref[0]
        s_scr[1] = s0b_ref[0]

    streams = ((qf_ref, kf_ref, vf_ref, cf_ref, rf_ref, False), (qb_ref, kb_ref, vb_ref, cb_ref, rb_ref, True))
    units = [(d, e) for d in range(2) for e in range(nv)]
    eye = jnp.where(_tri_masks(False)[0] & _tri_masks(True)[0], 1.0, 0.0)
    q, k, kk, qk, gc_all, gr_all, col, masks = [], [], [], [], [], [], [], []
    for q_ref, k_ref, _, c_ref, r_ref, rev in streams:
        tri, strict, _, _ = _tri_masks(rev)
        masks.append((tri, strict))
        qd = [q_ref[:, j * A_DK:(j + 1) * A_DK] for j in range(A_HG)]
        kd = [k_ref[:, j * A_DK:(j + 1) * A_DK] for j in range(A_HG)]
        q.append(qd)
        k.append(kd)
        kk.append([_dot_nt(x, x) for x in kd])
        qk.append([_dot_nt(x, y) for x, y in zip(qd, kd)])
        cv = c_ref[...]
        col.append(cv)
        gc_all.append(_dot_exact_lhs(jnp.where(tri, 1.0, 0.0), cv))
        gr_all.append(_dot_exact_rhs(r_ref[0, 0], _cum_mat_t(rev)))
    gcb = [jnp.broadcast_to(gc_all[d][:, e:e + 1], (CHUNK, LANES)) for d, e in units]
    beta = [jnp.broadcast_to(col[d][:, nv + e:nv + e + 1], (CHUNK, LANES)) for d, e in units]
    decay = [jnp.exp(jnp.where(masks[d][0], gcb[u][:, :CHUNK] - gr_all[d][e:e + 1, :], -jnp.inf))
             for u, (d, e) in enumerate(units)]
    a = [jnp.where(masks[d][1], beta[u][:, :CHUNK] * kk[d][e // 2] * decay[u], 0.0)
         for u, (d, e) in enumerate(units)]
    x = [eye - au for au in a]
    p = a
    for _ in range(5):
        p = [_dot(pu, pu) for pu in p]
        x = [xu + _dot(xu, pu) for xu, pu in zip(x, p)]
    egc = [jnp.exp(g) for g in gcb]
    v = [streams[d][2][:, e * A_DV:(e + 1) * A_DV] for d, e in units]
    uw = [_dot(x[u], jnp.concatenate([v[u] * beta[u], k[d][e // 2] * (beta[u] * egc[u])], axis=1))
          for u, (d, e) in enumerate(units)]
    s = [s_scr[d, e] for d, e in units]
    r2 = [_dot(jnp.concatenate([uw[u][:, A_DV:], q[d][e // 2] * egc[u]], axis=0), s[u])
          for u, (d, e) in enumerate(units)]
    v_new = [uw[u][:, :A_DV] - r2[u][:CHUNK] for u in range(len(units))]
    o = [r2[u][CHUNK:] + _dot(qk[d][e // 2] * decay[u], v_new[u]) for u, (d, e) in enumerate(units)]
    gtot = [gcb[u][0:1, :] if d else gcb[u][CHUNK - 1:CHUNK, :] for u, (d, e) in enumerate(units)]
    s_new = [s[u] * jnp.exp(gtot[u]) + _dot_tn(k[d][e // 2] * jnp.exp(gtot[u] - gcb[u]), v_new[u])
             for u, (d, e) in enumerate(units)]
    for u, (d, e) in enumerate(units):
        (ob_ref if d else of_ref)[:, e * A_DV:(e + 1) * A_DV] = o[u]
        s_scr[d, e] = s_new[u]

    @pl.when(c == pl.num_programs(2) - 1)
    def _():
        sff_ref[0] = s_scr[0]
        sfb_ref[0] = s_scr[1]


def _gdn_scan(qkv, colf, colb, rowf, rowb, s0f, s0b, buf_f, buf_b, n_seq, n_tok, row0):
    nc = n_tok // CHUNK
    cb0 = row0 // CHUNK
    ng = A_QK_HEADS // A_HG
    nv = 2 * A_HG
    fwd = lambda s, j, c: cb0 + s * nc + c
    bwd = lambda s, j, c: cb0 + s * nc + (nc - 1 - c)
    k_blk, v_blk = A_QK // (A_HG * A_DK), 2 * A_QK // (nv * A_DV)

    def stream_specs(ix):
        return [pl.BlockSpec((CHUNK, A_HG * A_DK), lambda s, j, c: (ix(s, j, c), j)),
                pl.BlockSpec((CHUNK, A_HG * A_DK), lambda s, j, c: (ix(s, j, c), k_blk + j)),
                pl.BlockSpec((CHUNK, nv * A_DV), lambda s, j, c: (ix(s, j, c), v_blk + j)),
                pl.BlockSpec((CHUNK, LANES), lambda s, j, c: (ix(s, j, c), j)),
                pl.BlockSpec((1, 1, nv, CHUNK), lambda s, j, c: (ix(s, j, c), j, 0, 0))]

    st_spec = pl.BlockSpec((1, nv, A_DK, A_DV), lambda s, j, c: (s, j, 0, 0))
    st_shape = jax.ShapeDtypeStruct((n_seq, A_V_HEADS, A_DK, A_DV), F32)
    any_spec = pl.BlockSpec(memory_space=pl.ANY)
    return pl.pallas_call(
        _gdn_kernel,
        grid=(n_seq, ng, nc),
        in_specs=stream_specs(fwd) + stream_specs(bwd) + [st_spec, st_spec, any_spec, any_spec],
        out_specs=[pl.BlockSpec((CHUNK, nv * A_DV), lambda s, j, c: (fwd(s, j, c), j)),
                   pl.BlockSpec((CHUNK, nv * A_DV), lambda s, j, c: (bwd(s, j, c), j)),
                   st_spec, st_spec],
        out_shape=[jax.ShapeDtypeStruct(buf_f.shape, F32), jax.ShapeDtypeStruct(buf_b.shape, F32),
                   st_shape, st_shape],
        scratch_shapes=[pltpu.VMEM((2, nv, A_DK, A_DV), F32)],
        input_output_aliases={12: 0, 13: 1},
        compiler_params=_cparams(("parallel", "parallel", "arbitrary")),
        name="gdn_scan",
    )(qkv, qkv, qkv, colf, rowf, qkv, qkv, qkv, colb, rowb, s0f, s0b, buf_f, buf_b)


def _gla_block_rows(bc, rev):
    n = CHUNK // 16
    zero = jnp.zeros((1, bc.shape[1]), F32)
    if rev:
        nxt = [bc[16 * (b + 1):16 * (b + 1) + 1] for b in range(n - 1)] + [zero]
        own = [bc[16 * b:16 * b + 1] for b in range(n)]
        mid = bc[32:33]
    else:
        nxt = [zero] + [bc[16 * b - 1:16 * b] for b in range(1, n)]
        own = [bc[16 * b + 15:16 * b + 16] for b in range(n)]
        mid = bc[31:32]
    bcast = lambda rows: jnp.concatenate([jnp.broadcast_to(r, (16, bc.shape[1])) for r in rows], axis=0)
    return bcast(nxt), bcast(own), mid


def _gla_kernel(qf_ref, kf_ref, vf_ref, lf_ref, qb_ref, kb_ref, vb_ref, lb_ref, w2_ref, gb_ref,
                s0f_ref, s0b_ref, bf_buf, bb_buf, of_ref, ob_ref, sff_ref, sfb_ref, s_scr):
    del bf_buf, bb_buf
    c = pl.program_id(1)

    @pl.when(c == 0)
    def _():
        s_scr[0] = s0f_ref[0]
        s_scr[1] = s0b_ref[0]

    streams = ((qf_ref, kf_ref, vf_ref, lf_ref, False), (qb_ref, kb_ref, vb_ref, lb_ref, True))
    heads = [slice(hd * D_DK, (hd + 1) * D_DK) for hd in range(D_HEADS)]
    units = [(d, hd) for d in range(2) for hd in range(D_HEADS)]
    ops = []
    for d, (q_ref, k_ref, v_ref, l_ref, rev) in enumerate(streams):
        tri, _, i, j = _tri_masks(rev)
        ib, jb = jnp.right_shift(i, 4), jnp.right_shift(j, 4)
        pair = ((ib == jb - 1) if rev else (ib == jb + 1)) & (jnp.bitwise_and(jnp.minimum(ib, jb), 1) == 0)
        half = ((i < 32) & (j >= 32)) if rev else ((i >= 32) & (j < 32))
        x = _dot(l_ref[...], w2_ref[d]) + gb_ref[d]
        lg = (jnp.minimum(x, 0.0) - jnp.log(1.0 + jnp.exp(-jnp.abs(x)))) * (1.0 / GATE_TAU)
        bc = _dot_exact_lhs(jnp.where(tri, 1.0, 0.0), lg)
        btot = bc[0:1] if rev else bc[CHUNK - 1:CHUNK]
        ref16, own16, mid = _gla_block_rows(bc, rev)
        q, k = q_ref[...] * D_DK ** -0.5, k_ref[...]
        qd = (q * jnp.exp(bc - ref16)).astype(BF16)
        q32 = (q * jnp.exp(jnp.minimum(bc - mid, 0.0))).astype(BF16)
        k32 = (k * jnp.exp(jnp.minimum(mid - bc, 0.0))).astype(BF16)
        ops.append(dict(
            masks=(tri & (ib == jb), pair, half), qd=qd, q32=q32, k32=k32,
            kdiag=(k * jnp.exp(ref16 - bc)).astype(BF16), k16=(k * jnp.exp(own16 - bc)).astype(BF16),
            qg=(q * jnp.exp(bc)).astype(BF16), kd=(k * jnp.exp(btot - bc)).astype(BF16),
            etot=jnp.exp(btot), v=v_ref[...].astype(BF16)))
    a1 = [_dot_nt(ops[d]['qd'][:, heads[hd]], ops[d]['kdiag'][:, heads[hd]]) for d, hd in units]
    a2 = [_dot_nt(ops[d]['qd'][:, heads[hd]], ops[d]['k16'][:, heads[hd]]) for d, hd in units]
    a3 = [_dot_nt(ops[d]['q32'][:, heads[hd]], ops[d]['k32'][:, heads[hd]]) for d, hd in units]
    a = [jnp.where(ops[d]['masks'][0], a1[u], 0.0) + jnp.where(ops[d]['masks'][1], a2[u], 0.0)
         + jnp.where(ops[d]['masks'][2], a3[u], 0.0) for u, (d, hd) in enumerate(units)]
    st = [s_scr[d, hd] for d, hd in units]
    vs = [ops[d]['v'][:, hd * D_DV:(hd + 1) * D_DV] for d, hd in units]
    o = [_dot_nt(ops[d]['qg'][:, heads[hd]], st[u]) + _dot(a[u], vs[u]) for u, (d, hd) in enumerate(units)]
    st_new = [st[u] * ops[d]['etot'][:, heads[hd]] + _dot_tn(vs[u], ops[d]['kd'][:, heads[hd]])
              for u, (d, hd) in enumerate(units)]
    for u, (d, hd) in enumerate(units):
        (ob_ref if d else of_ref)[:, hd * D_DV:(hd + 1) * D_DV] = o[u]
        s_scr[d, hd] = st_new[u]

    @pl.when(c == pl.num_programs(1) - 1)
    def _():
        sff_ref[0] = s_scr[0]
        sfb_ref[0] = s_scr[1]


def _gla_scan(p, lr, w2, gb, s0f, s0b, buf_f, buf_b, n_seq, n_tok, row0):
    nc = n_tok // CHUNK
    cb0 = row0 // CHUNK
    fwd = lambda s, c: cb0 + s * nc + c
    bwd = lambda s, c: cb0 + s * nc + (nc - 1 - c)

    def stream_specs(ix):
        return [pl.BlockSpec((CHUNK, D_QKW), lambda s, c: (ix(s, c), 0)),
                pl.BlockSpec((CHUNK, D_QKW), lambda s, c: (ix(s, c), 1)),
                pl.BlockSpec((CHUNK, D_VW), lambda s, c: (ix(s, c), 2 * D_QKW // D_VW)),
                pl.BlockSpec((CHUNK, 2 * GATE_RANK), lambda s, c: (ix(s, c), 0))]

    st_spec = pl.BlockSpec((1, D_HEADS, D_DV, D_DK), lambda s, c: (s, 0, 0, 0))
    st_shape = jax.ShapeDtypeStruct((n_seq, D_HEADS, D_DV, D_DK), F32)
    any_spec = pl.BlockSpec(memory_space=pl.ANY)
    return pl.pallas_call(
        _gla_kernel,
        grid=(n_seq, nc),
        in_specs=stream_specs(fwd) + stream_specs(bwd) + [
            pl.BlockSpec((2, 2 * GATE_RANK, D_QKW), lambda s, c: (0, 0, 0)),
            pl.BlockSpec((2, 1, D_QKW), lambda s, c: (0, 0, 0)),
            st_spec, st_spec, any_spec, any_spec],
        out_specs=[pl.BlockSpec((CHUNK, D_VW), lambda s, c: (fwd(s, c), 0)),
                   pl.BlockSpec((CHUNK, D_VW), lambda s, c: (bwd(s, c), 0)),
                   st_spec, st_spec],
        out_shape=[jax.ShapeDtypeStruct(buf_f.shape, F32), jax.ShapeDtypeStruct(buf_b.shape, F32),
                   st_shape, st_shape],
        scratch_shapes=[pltpu.VMEM((2, D_HEADS, D_DV, D_DK), F32)],
        input_output_aliases={12: 0, 13: 1},
        compiler_params=_cparams(("parallel", "arbitrary")),
        name="gla_scan",
    )(p, p, p, lr, p, p, p, lr, w2, gb, s0f, s0b, buf_f, buf_b)


def _gdn_prep_kernel(x_ref, prev_ref, next_ref, edge_ref, w_ref, o_ref, *, starts, ends):
    i, j = pl.program_id(0), pl.program_id(1)
    tm, cw = x_ref.shape
    r0 = i * tm
    is_start = functools.reduce(jnp.logical_or, [r0 == s for s in starts])
    is_end = functools.reduce(jnp.logical_or, [r0 + tm == e for e in ends])
    x = x_ref[...]
    prev_row = jnp.where(is_start, edge_ref[0, 7:8, :], prev_ref[7:8, :])
    next_row = jnp.where(is_end, 0.0, next_ref[0:1, :])
    rows = lax.broadcasted_iota(jnp.int32, (tm, 1), 0)
    xp = jnp.where(rows == 0, prev_row, pltpu.roll(x, 1, axis=0))
    xn = jnp.where(rows == tm - 1, next_row, pltpu.roll(x, tm - 1, axis=0))
    y = xp * w_ref[0:1, :] + x * w_ref[1:2, :] + xn * w_ref[2:3, :]
    y = y * _sigmoid(y)
    is_qk = j < 2 * A_QK // cw
    q_scale = jnp.where(j < A_QK // cw, A_DK ** -0.5, 1.0)
    for c0 in range(0, cw, A_DK):
        yh = y[:, c0:c0 + A_DK]
        nrm = yh * (lax.rsqrt(jnp.sum(yh * yh, axis=-1, keepdims=True) + EPS) * q_scale)
        o_ref[:, c0:c0 + A_DK] = jnp.where(is_qk, nrm, yh)


def _gdn_prep(lay, x, edge, conv_w):
    r, c = x.shape
    tm, cw = 512, 512
    starts = [row + s * n for b, n, row, _ in lay.groups for s in range(b)]
    ends = [row + (s + 1) * n for b, n, row, _ in lay.groups for s in range(b)]
    seq_of = lambda i: sum((i * tm >= s).astype(jnp.int32) for s in starts) - 1
    w8 = jnp.pad(conv_w.astype(F32), ((0, 8 - conv_w.shape[0]), (0, 0)))
    return pl.pallas_call(
        functools.partial(_gdn_prep_kernel, starts=starts, ends=ends),
        grid=(r // tm, c // cw),
        in_specs=[pl.BlockSpec((tm, cw), lambda i, j: (i, j)),
                  pl.BlockSpec((8, cw), lambda i, j: (jnp.maximum(i * (tm // 8) - 1, 0), j)),
                  pl.BlockSpec((8, cw), lambda i, j: (jnp.minimum((i + 1) * (tm // 8), r // 8 - 1), j)),
                  pl.BlockSpec((1, 8, cw), lambda i, j: (seq_of(i), 0, j)),
                  pl.BlockSpec((8, cw), lambda i, j: (0, j))],
        out_specs=pl.BlockSpec((tm, cw), lambda i, j: (i, j)),
        out_shape=jax.ShapeDtypeStruct((r, c), F32),
        compiler_params=_cparams(("parallel", "parallel")),
        name="gdn_prep",
    )(x, x, x, edge, w8)


def _gqa_prep_kernel(q_ref, kv_ref, cos_ref, sin_ref, qg_ref, kg_ref, qt_ref, k_ref, vt_ref):
    cos, sin = cos_ref[...], sin_ref[...]

    def norm_rope(x, g):
        xn = x * lax.rsqrt(jnp.mean(x * x, axis=-1, keepdims=True) + EPS) * g
        return xn * cos + pltpu.roll(xn, B_DH // 2, axis=1) * sin

    for hd in range(B_HEADS):
        sl = slice(hd * B_DH, (hd + 1) * B_DH)
        qt_ref[sl, :] = norm_rope(q_ref[:, sl], qg_ref[...]).T.astype(BF16)
    for hd in range(B_KV_HEADS):
        sl = slice(hd * B_DH, (hd + 1) * B_DH)
        k_ref[:, sl] = norm_rope(kv_ref[:, sl], kg_ref[...]).astype(BF16)
        vt_ref[sl, :] = kv_ref[:, B_KVW + hd * B_DH:B_KVW + (hd + 1) * B_DH].T.astype(BF16)


def _pos_block(lay, tm):
    def pos(i):
        r = i * tm
        out = 0
        for b, n, row, _ in lay.groups:
            inside = (r >= row) & (r < row + b * n)
            out = out + jnp.where(inside, ((r - row) % n) // tm, 0)
        return out
    return pos


def _gqa_prep(lay, pq, pkv, cos, sin, qg, kg):
    r = pq.shape[0]
    tm = min(512, min(n for _, n, _, _ in lay.groups))
    pos = _pos_block(lay, tm)
    row = lambda i: (i, 0)
    col = lambda i: (0, i)
    fixed = lambda i: (0, 0)
    return pl.pallas_call(
        _gqa_prep_kernel,
        grid=(r // tm,),
        in_specs=[pl.BlockSpec((tm, B_QW), row), pl.BlockSpec((tm, 2 * B_KVW), row),
                  pl.BlockSpec((tm, B_DH), lambda i: (pos(i), 0)), pl.BlockSpec((tm, B_DH), lambda i: (pos(i), 0)),
                  pl.BlockSpec((1, B_DH), fixed), pl.BlockSpec((1, B_DH), fixed)],
        out_specs=[pl.BlockSpec((B_QW, tm), col), pl.BlockSpec((tm, B_KVW), row), pl.BlockSpec((B_KVW, tm), col)],
        out_shape=[jax.ShapeDtypeStruct((B_QW, r), BF16), jax.ShapeDtypeStruct((r, B_KVW), BF16),
                   jax.ShapeDtypeStruct((B_KVW, r), BF16)],
        compiler_params=_cparams(("parallel",)),
        name="gqa_prep",
    )(pq, pkv, cos, sin, qg.reshape(1, B_DH), kg.reshape(1, B_DH))


def _diff_prep_kernel(q_ref, k_ref, v_ref, qg_ref, kg_ref, qt_ref, ko_ref, vt_ref):
    lo = lax.broadcasted_iota(jnp.int32, (1, LANES), 1) < C_DQK

    def norm_maps(x, g):
        sq = x * x
        s_lo = jnp.sum(jnp.where(lo, sq, 0.0), axis=-1, keepdims=True)
        s_hi = jnp.sum(jnp.where(lo, 0.0, sq), axis=-1, keepdims=True)
        inv = jnp.where(lo, lax.rsqrt(s_lo * (1.0 / C_DQK) + EPS), lax.rsqrt(s_hi * (1.0 / C_DQK) + EPS))
        return x * inv * g

    qt_ref[...] = norm_maps(q_ref[...], qg_ref[...]).T.astype(BF16)
    ko_ref[...] = norm_maps(k_ref[...], kg_ref[...]).astype(BF16)
    vt_ref[...] = v_ref[...].T.astype(BF16)


def _diff_prep(p, qg, kg):
    r = p.shape[0]
    tm = 512 if r % 512 == 0 else LANES
    fixed = lambda i, h: (0, 0)
    return pl.pallas_call(
        _diff_prep_kernel,
        grid=(r // tm, C_HEADS),
        in_specs=[pl.BlockSpec((tm, LANES), lambda i, h: (i, h)),
                  pl.BlockSpec((tm, LANES), lambda i, h: (i, C_HEADS + h)),
                  pl.BlockSpec((tm, LANES), lambda i, h: (i, 2 * C_HEADS + h)),
                  pl.BlockSpec((1, LANES), fixed), pl.BlockSpec((1, LANES), fixed)],
        out_specs=[pl.BlockSpec((LANES, tm), lambda i, h: (h, i)), pl.BlockSpec((tm, LANES), lambda i, h: (i, h)),
                   pl.BlockSpec((LANES, tm), lambda i, h: (h, i))],
        out_shape=[jax.ShapeDtypeStruct((C_QKW, r), BF16), jax.ShapeDtypeStruct((r, C_QKW), BF16),
                   jax.ShapeDtypeStruct((C_VW, r), BF16)],
        compiler_params=_cparams(("parallel", "parallel")),
        name="diff_prep",
    )(p, p, p, qg.reshape(1, LANES), kg.reshape(1, LANES))


class _Layout:
    def __init__(self, shapes):
        self.groups = []
        row, seq = 0, 0
        for b, n in shapes:
            self.groups.append((b, n, row, seq))
            row += b * n
            seq += b
        self.rows, self.n_seq = row, seq

    def to_seqs(self, main, meta):
        out = []
        c = main.shape[-1]
        for b, n, row, seq in self.groups:
            m = meta[seq * N_META:(seq + b) * N_META].reshape(b, N_META, c)
            x = main[row:row + b * n].reshape(b, n, c)
            out.append(jnp.concatenate([m, x], axis=1))
        return out

    def from_seqs(self, seqs):
        main = jnp.concatenate([s[:, N_META:].reshape(-1, s.shape[-1]) for s in seqs], axis=0)
        meta = jnp.concatenate([s[:, :N_META].reshape(-1, s.shape[-1]) for s in seqs], axis=0)
        return main, meta


def _meta_chunk(meta):
    c = meta.shape[-1]
    m = meta.reshape(-1, N_META, c)
    return jnp.pad(m, ((0, 0), (CHUNK - N_META, 0), (0, 0))).reshape(-1, c)


def _meta_unchunk(x):
    c = x.shape[-1]
    return x.reshape(-1, CHUNK, c)[:, CHUNK - N_META:].reshape(-1, c)


def _map_seqs(lay, fn, main, meta):
    return lay.from_seqs([fn(s) for s in lay.to_seqs(main, meta)])


def _run_scan(scan, lay, main_args, meta_args, state_shape, width):
    s_all = lay.n_seq
    zeros = jnp.zeros((s_all,) + state_shape, F32)
    mrows = s_all * CHUNK
    mbuf = jnp.zeros((mrows, width), F32)
    _, _, s_meta, _ = scan(*meta_args, zeros, zeros, mbuf, mbuf, s_all, CHUNK, 0)
    of = jnp.zeros((lay.rows, width), F32)
    ob = jnp.zeros((lay.rows, width), F32)
    finals = []
    for b, n, row, seq in lay.groups:
        of, ob, _, sfb = scan(*main_args, s_meta[seq:seq + b], zeros[seq:seq + b], of, ob, b, n, row)
        finals.append(sfb)
    mof, mob, _, _ = scan(*meta_args, zeros, jnp.concatenate(finals, axis=0), mbuf, mbuf, s_all, CHUNK, 0)
    return (of, ob), (_meta_unchunk(mof), _meta_unchunk(mob))


def _gdn_layer(lay, h, hm, norm_g, w_in, conv_w, a_log, dt_bias, out_g, w_out):
    w_qkv, w_z, w_ab = (w_in[:, :A_CONV_CH], w_in[:, A_CONV_CH:A_CONV_CH + A_VW], w_in[:, A_CONV_CH + A_VW:])
    x, xm = _norm_proj(h, norm_g, w_qkv), _norm_proj(hm, norm_g, w_qkv)
    z, zm = _norm_proj(h, norm_g, w_z), _norm_proj(hm, norm_g, w_z)
    ab, abm = _norm_proj(h, norm_g, w_ab), _norm_proj(hm, norm_g, w_ab)
    s_all = lay.n_seq
    xm3 = xm.reshape(s_all, N_META, A_CONV_CH)
    qkv = _gdn_prep(lay, x, xm3[:, N_META - 8:], conv_w)

    def prep(x):
        prev = jnp.pad(x[:, :-1], ((0, 0), (1, 0), (0, 0)))
        nxt = jnp.pad(x[:, 1:], ((0, 0), (0, 1), (0, 0)))
        y = jax.nn.silu(prev * conv_w[0] + x * conv_w[1] + nxt * conv_w[2])
        b, l, _ = y.shape
        qk = y[..., :2 * A_QK].reshape(b, l, 2 * A_QK_HEADS, A_DK)
        qk = qk * lax.rsqrt(jnp.sum(qk * qk, axis=-1, keepdims=True) + EPS)
        qk = qk.reshape(b, l, 2, A_QK) * jnp.array([A_DK ** -0.5, 1.0], F32)[:, None]
        return jnp.concatenate([qk.reshape(b, l, 2 * A_QK), y[..., 2 * A_QK:]], axis=-1)

    first = x[np.array([row + s * n for b, n, row, _ in lay.groups for s in range(b)])]
    qkvm = prep(jnp.concatenate([xm3, first[:, None]], axis=1))[:, :N_META].reshape(-1, A_CONV_CH)

    def gates(x):
        x = x.reshape(-1, 2, 2, A_V_HEADS)
        g = -jnp.exp(a_log.astype(F32)) * jax.nn.softplus(x[:, 0] + dt_bias.astype(F32))
        return g, jax.nn.sigmoid(x[:, 1])

    ng, nv = A_QK_HEADS // A_HG, 2 * A_HG

    def pack(g, beta, d):
        r = g.shape[0]
        gd, bd = g[:, d].reshape(r, ng, nv), beta[:, d].reshape(r, ng, nv)
        col = jnp.concatenate([gd, bd, jnp.zeros((r, ng, LANES - 2 * nv), F32)], axis=-1).reshape(r, ng * LANES)
        row = gd.reshape(r // CHUNK, CHUNK, ng, nv).transpose(0, 2, 3, 1)
        return col, row

    def scan_args(qkv_rows, g, beta):
        colf, rowf = pack(g, beta, 0)
        colb, rowb = pack(g, beta, 1)
        return (qkv_rows, colf, colb, rowf, rowb)

    g, beta = gates(ab)
    gm, betam = gates(abm)
    pad_gate = lambda t: _meta_chunk(t.reshape(-1, 2 * A_V_HEADS)).reshape(-1, 2, A_V_HEADS)
    (of, ob), (mof, mob) = _run_scan(
        _gdn_scan, lay, scan_args(qkv, g, beta), scan_args(_meta_chunk(qkvm), pad_gate(gm), pad_gate(betam)),
        (A_V_HEADS, A_DK, A_DV), A_VW)
    gain = jnp.tile(out_g.astype(F32), A_V_HEADS)
    return (_out_proj([of, ob], z, gain, w_out, h, A_DV), _out_proj([mof, mob], zm, gain, w_out, hm, A_DV))


def _rope_tables(n_tok):
    rows = n_tok // GRID_W
    row = jnp.repeat(jnp.arange(rows, dtype=F32), GRID_W)
    col = (jnp.arange(rows * GRID_W) % GRID_W).astype(F32)
    inv = ROPE_THETA ** (-jnp.arange(0, ROPE_AXIS, 2, dtype=F32) / ROPE_AXIS)
    ang = jnp.concatenate([row[:, None] * inv, col[:, None] * inv], axis=-1)
    return jnp.cos(ang), jnp.sin(ang)


def _pad_meta(x, n_seq):
    return jnp.pad(x.reshape(n_seq, N_META, -1), ((0, 0), (0, LANES - N_META), (0, 0)))


def _gqa_layer(lay, h, hm, norm_g, w_in, q_g, k_g, w_out):
    kv_end = B_QW + 2 * B_KVW
    pq, pkv, z = (_norm_proj(h, norm_g, w_in[:, :B_QW]), _norm_proj(h, norm_g, w_in[:, B_QW:kv_end]),
                  _norm_proj(h, norm_g, w_in[:, kv_end:]))
    pm = _norm_proj(hm, norm_g, w_in)

    def norm_heads(x, g):
        r = x.shape[0]
        x = x.reshape(r, -1, B_DH)
        return x * lax.rsqrt(jnp.mean(x * x, axis=-1, keepdims=True) + EPS) * g.astype(F32)

    scale = B_DH ** -0.5 * LOG2E
    cos, sin = _rope_tables(max(n for _, n, _, _ in lay.groups))
    qt, k, vt = _gqa_prep(lay, pq, pkv, jnp.concatenate([cos, cos], axis=-1), jnp.concatenate([-sin, sin], axis=-1),
                          q_g.astype(F32) * scale, k_g.astype(F32))
    s_all = lay.n_seq
    qm = (norm_heads(pm[:, :B_QW], q_g) * scale).reshape(-1, B_QW).astype(BF16)
    km = norm_heads(pm[:, B_QW:B_QW + B_KVW], k_g).reshape(-1, B_KVW).astype(BF16)
    vm = pm[:, B_QW + B_KVW:B_QW + 2 * B_KVW].astype(BF16)
    zm = pm[:, B_QW + 2 * B_KVW:]
    qmt = _pad_meta(qm, s_all).reshape(s_all * LANES, B_QW).T
    kmp, vmt = _pad_meta(km, s_all), _pad_meta(vm, s_all).transpose(0, 2, 1)

    o = jnp.zeros((lay.rows, B_QW), F32)
    om = jnp.zeros((s_all * LANES, B_QW), F32)
    for b, n, row, seq in lay.groups:
        tq, tk = min(GQA_TQ, n), min(GQA_TK, n)
        o = _gqa_attention(qt, k, vt, kmp[seq:seq + b], vmt[seq:seq + b], o, b, n, n, row, row, tq, tk)
        om = _gqa_attention(qmt, k, vt, kmp[seq:seq + b], vmt[seq:seq + b], om, b, LANES, n,
                            seq * LANES, row, LANES, tk)
    om = om.reshape(s_all, LANES, B_QW)[:, :N_META].reshape(-1, B_QW)
    ones = jnp.ones((B_QW,), F32)
    return _out_proj([o], z, ones, w_out, h, 0), _out_proj([om], zm, ones, w_out, hm, 0)


def _t5_bucket_np(rel):
    nb = REL_BUCKETS // 2
    max_exact = nb // 2
    n = np.abs(rel)
    steps = nb - max_exact
    large = np.full(n.shape, max_exact, np.int64)
    for kstep in range(1, steps + 1):
        large += (n.astype(np.float64) ** steps >= (max_exact ** steps) * float(REL_MAX_DIST // max_exact) ** kstep)
    large = np.minimum(large, nb - 1)
    return np.where(rel > 0, nb, 0) + np.where(n < max_exact, n, large)


def _diff_layer(lay, h, hm, norm_g, w_in, q_g, k_g, lam, sub_g, w_out, rel_bias, layer_idx):
    n_qkv = 2 * C_QKW + C_VW
    p, z = _norm_proj(h, norm_g, w_in[:, :n_qkv]), _norm_proj(h, norm_g, w_in[:, n_qkv:])
    pm = _norm_proj(hm, norm_g, w_in)

    def norm_maps(x, g):
        r = x.shape[0]
        x = x.reshape(r, -1, C_DQK)
        return (x * lax.rsqrt(jnp.mean(x * x, axis=-1, keepdims=True) + EPS) * g.astype(F32)).reshape(r, -1)

    s_all = lay.n_seq
    scale = C_DQK ** -0.5 * LOG2E
    qt, k, vt = _diff_prep(p, jnp.tile(q_g.astype(F32), 2) * scale, jnp.tile(k_g.astype(F32), 2))
    qm = (norm_maps(pm[:, :C_QKW], q_g) * scale).astype(BF16)
    km = norm_maps(pm[:, C_QKW:2 * C_QKW], k_g).astype(BF16)
    vm = pm[:, 2 * C_QKW:n_qkv].astype(BF16)
    zm = pm[:, n_qkv:]
    qmt = _pad_meta(qm, s_all).reshape(s_all * LANES, C_QKW).T
    kmp, vmt = _pad_meta(km, s_all), _pad_meta(vm, s_all).transpose(0, 2, 1)

    lam = lam.astype(F32)
    lam_init = 0.8 - 0.6 * math.exp(-0.3 * layer_idx)
    lam_full = jnp.exp(jnp.sum(lam[0] * lam[1])) - jnp.exp(jnp.sum(lam[2] * lam[3])) + lam_init
    table = rel_bias.astype(F32).T * LOG2E
    look = lambda rel: table[:, _t5_bucket_np(rel)]
    nb = REL_BUCKETS // 2
    cst = jnp.zeros((C_HEADS, 8, LANES), F32)
    cst = cst.at[:, 0].set(table[:, nb - 1:nb]).at[:, 1].set(table[:, 2 * nb - 1:2 * nb]).at[:, 2].set(lam_full)
    ar = np.arange(LANES)
    key_pad = jnp.asarray(np.where(ar < N_META, 0.0, NEG), F32)[:, None]
    tt = jnp.stack([look(o * LANES + ar[:, None] - ar[None, :]) for o in (-1, 0, 1)], axis=1)
    mq = np.minimum(ar, N_META - 1)
    tt_m = jnp.stack([look(N_META + ar[:, None] - mq[None, :])] * 3, axis=1)
    meta_rows = (ar < N_META)[:, None]
    bm_m = (jnp.where(meta_rows, look(ar[:, None] - mq[None, :]), 0.0) + key_pad)[:, None]

    o = jnp.zeros((lay.rows, C_VW), F32)
    om = jnp.zeros((s_all * LANES, C_VW), F32)
    for b, n, row, seq in lay.groups:
        t, tk = min(DIFF_TQ, n), min(DIFF_TK, n)
        pq = np.arange(t)
        near = look(ar[:N_META, None] - N_META - pq[None, :])
        far = jnp.broadcast_to(table[:, nb - 1][:, None, None], near.shape)
        pad_rows = jnp.full((C_HEADS, 2, LANES - N_META, t), NEG, F32)
        bm = jnp.concatenate([jnp.stack([near, far], axis=1), pad_rows], axis=2)
        o = _diff_attention(qt, k, vt, kmp[seq:seq + b], vmt[seq:seq + b], tt, bm, cst, o,
                            b, n, n, row, row, t, tk, False)
        om = _diff_attention(qmt, k, vt, kmp[seq:seq + b], vmt[seq:seq + b], tt_m, bm_m, cst, om,
                             b, LANES, n, seq * LANES, row, LANES, tk, True)
    om = om.reshape(s_all, LANES, C_VW)[:, :N_META].reshape(-1, C_VW)
    gain = jnp.tile(sub_g.astype(F32), C_HEADS) * (1.0 - lam_init)
    return _out_proj([o], z, gain, w_out, h, C_DV), _out_proj([om], zm, gain, w_out, hm, C_DV)


def _gla_layer(lay, h, hm, norm_g, w_in, gate_w2, gate_b, out_g, w_out):
    n_qkv = 2 * D_QKW + D_VW
    w_qkv, w_z, w_lr = w_in[:, :n_qkv], w_in[:, n_qkv:n_qkv + D_VW], w_in[:, n_qkv + D_VW:]
    p, pm = _norm_proj(h, norm_g, w_qkv), _norm_proj(hm, norm_g, w_qkv)
    z, zm = _norm_proj(h, norm_g, w_z), _norm_proj(hm, norm_g, w_z)
    lr, lrm = _norm_proj(h, norm_g, w_lr), _norm_proj(hm, norm_g, w_lr)
    zero = jnp.zeros((GATE_RANK, D_QKW), F32)
    w2 = jnp.stack([jnp.concatenate([gate_w2[0].astype(F32), zero], axis=0),
                    jnp.concatenate([zero, gate_w2[1].astype(F32)], axis=0)])
    gb = gate_b.astype(F32).reshape(2, 1, D_QKW)
    (of, ob), (mof, mob) = _run_scan(
        _gla_scan, lay, (p, lr, w2, gb), (_meta_chunk(pm), _meta_chunk(lrm), w2, gb), (D_HEADS, D_DV, D_DK), D_VW)
    gain = jnp.tile(out_g.astype(F32), D_HEADS)
    return _out_proj([of, ob], z, gain, w_out, h, D_DV), _out_proj([mof, mob], zm, gain, w_out, hm, D_DV)


def _trunk(xs, meta_tokens, rel_bias, a, b, c, d, depth):
    lay = _Layout([(x.shape[0], x.shape[1]) for x in xs])
    h = jnp.concatenate([x.reshape(-1, D_MODEL) for x in xs], axis=0)
    hm = jnp.tile(meta_tokens.astype(F32), (lay.n_seq, 1))
    for i in range(depth):
        m, j = i % 4, i // 4
        if m == 0:
            h, hm = _gdn_layer(lay, h, hm, *(t[j] for t in a))
        elif m == 1:
            h, hm = _gqa_layer(lay, h, hm, *(t[j] for t in b))
        elif m == 2:
            h, hm = _diff_layer(lay, h, hm, *(t[j] for t in c), rel_bias, i)
        else:
            h, hm = _gla_layer(lay, h, hm, *(t[j] for t in d))
    return [h[row:row + bsz * n].reshape(bsz, n, D_MODEL) for bsz, n, row, _ in lay.groups]


def kernel(x_prompt, x_sample, meta_tokens, rel_bias, a_norm, a_w_in, a_conv, a_a_log, a_dt_bias, a_out_norm, a_w_out, b_norm, b_w_in, b_q_norm, b_k_norm, b_w_out, c_norm, c_w_in, c_q_norm, c_k_norm, c_lambda, c_sub_norm, c_w_out, d_norm, d_w_in, d_gate_w2, d_gate_b, d_out_norm, d_w_out):
    y_prompt, y_sample = _trunk(
        [x_prompt, x_sample], meta_tokens, rel_bias,
        (a_norm, a_w_in, a_conv, a_a_log, a_dt_bias, a_out_norm, a_w_out),
        (b_norm, b_w_in, b_q_norm, b_k_norm, b_w_out),
        (c_norm, c_w_in, c_q_norm, c_k_norm, c_lambda, c_sub_norm, c_w_out),
        (d_norm, d_w_in, d_gate_w2, d_gate_b, d_out_norm, d_w_out), 4)
    return (y_prompt, y_sample)
```

```python
import functools
import math

import numpy as np
import jax
import jax.numpy as jnp
from jax import lax
from jax.experimental import pallas as pl
from jax.experimental.pallas import tpu as pltpu

F32 = jnp.float32
BF16 = jnp.bfloat16

D_MODEL = 1024
N_META = 16
CHUNK = 64
GRID_W = 64
EPS = 1e-6
LANES = 128
VMEM_LIMIT = 56 * 1024 * 1024
NEG = -1e30
LOG2E = math.log2(math.e)
STRIP = 1024
ONES_ROWS = 16
PROJ_TN_MAX = 2048
GQA_TQ, GQA_TK = 1024, 2048
DIFF_TQ, DIFF_TK = 2048, 2048

A_QK_HEADS, A_V_HEADS, A_DK, A_DV = 8, 16, 128, 128
A_QK, A_VW = A_QK_HEADS * A_DK, A_V_HEADS * A_DV
A_CONV_CH = 2 * A_QK + A_VW
A_HG = 8
B_HEADS, B_KV_HEADS, B_DH = 8, 2, 128
B_GROUP = B_HEADS // B_KV_HEADS
B_QW, B_KVW = B_HEADS * B_DH, B_KV_HEADS * B_DH
ROPE_AXIS = B_DH // 2
ROPE_THETA = 10000.0
C_HEADS, C_DQK, C_DV = 8, 64, 128
C_QKW, C_VW = C_HEADS * 2 * C_DQK, C_HEADS * C_DV
REL_BUCKETS, REL_MAX_DIST = 32, 128
D_HEADS, D_DK, D_DV = 4, 128, 256
D_QKW, D_VW = D_HEADS * D_DK, D_HEADS * D_DV
GATE_RANK, GATE_TAU = 16, 16.0


def _cparams(sem):
    return pltpu.CompilerParams(dimension_semantics=sem, vmem_limit_bytes=VMEM_LIMIT)


def _sigmoid(x):
    return 1.0 / (1.0 + jnp.exp(-x))


def _dot(a, b):
    return jnp.dot(a.astype(BF16), b.astype(BF16), preferred_element_type=F32)


def _dot_nt(a, b):
    return lax.dot_general(a.astype(BF16), b.astype(BF16), (((1,), (1,)), ((), ())),
                           preferred_element_type=F32)


def _dot_tn(a, b):
    return lax.dot_general(a.astype(BF16), b.astype(BF16), (((0,), (0,)), ((), ())),
                           preferred_element_type=F32)


def _split3(x):
    x1 = x.astype(BF16)
    r1 = x - x1.astype(F32)
    x2 = r1.astype(BF16)
    x3 = (r1 - x2.astype(F32)).astype(BF16)
    return x1, x2, x3


def _dot_exact_lhs(m01, x):
    m = m01.astype(BF16)
    x1, x2, x3 = _split3(x)
    return (jnp.dot(m, x1, preferred_element_type=F32) + jnp.dot(m, x2, preferred_element_type=F32)
            + jnp.dot(m, x3, preferred_element_type=F32))


def _dot_exact_rhs(x, m01):
    m = m01.astype(BF16)
    x1, x2, x3 = _split3(x)
    return (jnp.dot(x1, m, preferred_element_type=F32) + jnp.dot(x2, m, preferred_element_type=F32)
            + jnp.dot(x3, m, preferred_element_type=F32))


def _norm_proj_kernel(x_ref, g_ref, w_ref, o_ref):
    x = x_ref[...]
    ms = jnp.mean(x * x, axis=-1, keepdims=True)
    xn = (x * lax.rsqrt(ms + EPS) * g_ref[...]).astype(BF16)
    o_ref[...] = jnp.dot(xn, w_ref[...], preferred_element_type=F32)


def _norm_proj(x, g, w):
    m, k = x.shape
    n = w.shape[1]
    tm = 512 if m % 512 == 0 else m
    tn = max([t for t in range(256, PROJ_TN_MAX + 1, 256) if n % t == 0], default=n)
    return pl.pallas_call(
        _norm_proj_kernel,
        grid=(n // tn, m // tm),
        in_specs=[pl.BlockSpec((tm, k), lambda j, i: (i, 0)),
                  pl.BlockSpec((1, k), lambda j, i: (0, 0)),
                  pl.BlockSpec((k, tn), lambda j, i: (0, j))],
        out_specs=pl.BlockSpec((tm, tn), lambda j, i: (i, j)),
        out_shape=jax.ShapeDtypeStruct((m, n), F32),
        compiler_params=_cparams(("parallel", "parallel")),
        name="norm_proj",
    )(x, g.reshape(1, k).astype(F32), w.astype(BF16))


def _out_proj_kernel(*refs, n_o, hw):
    o_refs = refs[:n_o]
    z_ref, g_ref, w_ref, h_ref, out_ref = refs[n_o:]
    k = z_ref.shape[1]
    cw = hw if hw else 256
    acc = h_ref[...]
    for c0 in range(0, k, cw):
        o = o_refs[0][:, c0:c0 + cw]
        for r in o_refs[1:]:
            o = o + r[:, c0:c0 + cw]
        if hw:
            ms = jnp.mean(o * o, axis=-1, keepdims=True)
            o = o * lax.rsqrt(ms + EPS) * g_ref[:, c0:c0 + cw]
        z = z_ref[:, c0:c0 + cw]
        gated = o * (z * _sigmoid(z))
        acc = acc + jnp.dot(gated.astype(BF16), w_ref[c0:c0 + cw, :], preferred_element_type=F32)
    out_ref[...] = acc


def _out_proj(os, z, g, w, h, hw):
    m, k = z.shape
    n = w.shape[1]
    tm = (512 if k <= 1024 else 256) if m % 512 == 0 else m
    row = lambda i: (i, 0)
    fixed = lambda i: (0, 0)
    return pl.pallas_call(
        functools.partial(_out_proj_kernel, n_o=len(os), hw=hw),
        grid=(m // tm,),
        in_specs=[pl.BlockSpec((tm, k), row) for _ in os] + [
            pl.BlockSpec((tm, k), row), pl.BlockSpec((1, k), fixed),
            pl.BlockSpec((k, n), fixed), pl.BlockSpec((tm, n), row)],
        out_specs=pl.BlockSpec((tm, n), row),
        out_shape=jax.ShapeDtypeStruct((m, n), F32),
        compiler_params=_cparams(("parallel",)),
        name="out_proj",
    )(*os, z, g.reshape(1, k).astype(F32), w.astype(BF16), h)


def _strip_update(s_t, c, v_ext, m_scr, l_scr, acc_scr, cols):
    p, alpha = _strip_softmax(s_t, c, m_scr, cols)
    _strip_pv(p, alpha, v_ext, l_scr, acc_scr, cols)


def _strip_softmax(s_t, c, m_scr, cols):
    m_prev = m_scr[:, cols]
    m_new = jnp.maximum(m_prev, jnp.max(s_t, axis=0, keepdims=True) + c)
    m_scr[:, cols] = m_new
    return jnp.exp2(s_t - (m_new - c)).astype(BF16), jnp.exp2(m_prev - m_new)


def _strip_pv(p, alpha, v_ext, l_scr, acc_scr, cols):
    dv = acc_scr.shape[0]
    pv = jnp.dot(v_ext, p, preferred_element_type=F32)
    acc_scr[:, cols] = alpha * acc_scr[:, cols] + pv[:dv]
    l_scr[:, cols] = alpha * l_scr[:, cols] + pv[dv:dv + 1]


def _run_strips(score_fns, cols_list, c, v_ext, m_scr, l_scr, acc_scr):
    s_next = score_fns[0]()
    pending = None
    for n, cols in enumerate(cols_list):
        s_t, s_next = s_next, (score_fns[n + 1]() if n + 1 < len(score_fns) else None)
        if pending is not None:
            _strip_pv(pending[0], pending[1], v_ext, l_scr, acc_scr, pending[2])
        p, alpha = _strip_softmax(s_t, c, m_scr, cols)
        pending = (p, alpha, cols)
    _strip_pv(pending[0], pending[1], v_ext, l_scr, acc_scr, pending[2])


def _attn_init(m_scr, l_scr, acc_scr):
    m_scr[...] = jnp.full(m_scr.shape, NEG, F32)
    l_scr[...] = jnp.zeros(l_scr.shape, F32)
    acc_scr[...] = jnp.zeros(acc_scr.shape, F32)


def _with_ones(v_t):
    return jnp.concatenate([v_t, jnp.ones((ONES_ROWS, v_t.shape[1]), BF16)], axis=0)


def _strips(width):
    w = min(STRIP, width)
    return [slice(c0, c0 + w) for c0 in range(0, width, w)]


def _gqa_kernel(qt_ref, k_ref, vt_ref, km_ref, vmt_ref, buf_ref, o_ref, m_scr, l_scr, acc_scr):
    del buf_ref
    ki = pl.program_id(3)
    tq = qt_ref.shape[1]
    heads = [slice(g * B_DH, (g + 1) * B_DH) for g in range(B_GROUP)]

    @pl.when(ki == 0)
    def _():
        _attn_init(m_scr, l_scr, acc_scr)
        key = lax.broadcasted_iota(jnp.int32, (LANES, 1), 0)
        pad = jnp.where(key < N_META, 0.0, NEG)
        vm_ext = _with_ones(vmt_ref[0])
        for g, rows in enumerate(heads):
            for cols in _strips(tq):
                s_t = jnp.dot(km_ref[0], qt_ref[rows, cols], preferred_element_type=F32) + pad
                _strip_update(s_t, 0.0, vm_ext, m_scr, l_scr, acc_scr,
                              slice(g * tq + cols.start, g * tq + cols.stop))

    k = k_ref[...]
    v_ext = _with_ones(vt_ref[...])
    work = [(g, rows, cols) for g, rows in enumerate(heads) for cols in _strips(tq)]
    scr_cols = [slice(g * tq + cols.start, g * tq + cols.stop) for g, _, cols in work]
    _run_strips([functools.partial(lambda rows, cols: jnp.dot(k, qt_ref[rows, cols], preferred_element_type=F32),
                                   rows, cols) for _, rows, cols in work],
                scr_cols, 0.0, v_ext, m_scr, l_scr, acc_scr)

    @pl.when(ki == pl.num_programs(3) - 1)
    def _():
        for g, rows in enumerate(heads):
            cols = slice(g * tq, (g + 1) * tq)
            o_ref[:, rows] = (acc_scr[:, cols] / l_scr[:, cols]).T


def _gqa_attention(qt, k, vt, km, vmt, buf, n_seq, n_q, n_k, q_col0, k_row0, tq, tk):
    nq, nk = n_q // tq, n_k // tk
    qb0, kb0 = q_col0 // tq, k_row0 // tk
    r = B_GROUP * tq
    gw = B_GROUP * B_DH
    return pl.pallas_call(
        _gqa_kernel,
        grid=(n_seq, B_KV_HEADS, nq, nk),
        in_specs=[pl.BlockSpec((gw, tq), lambda s, h, i, j: (h, qb0 + s * nq + i)),
                  pl.BlockSpec((tk, B_DH), lambda s, h, i, j: (kb0 + s * nk + j, h)),
                  pl.BlockSpec((B_DH, tk), lambda s, h, i, j: (h, kb0 + s * nk + j)),
                  pl.BlockSpec((1, LANES, B_DH), lambda s, h, i, j: (s, 0, h)),
                  pl.BlockSpec((1, B_DH, LANES), lambda s, h, i, j: (s, h, 0)),
                  pl.BlockSpec(memory_space=pl.ANY)],
        out_specs=pl.BlockSpec((tq, gw), lambda s, h, i, j: (qb0 + s * nq + i, h)),
        out_shape=jax.ShapeDtypeStruct(buf.shape, F32),
        scratch_shapes=[pltpu.VMEM((1, r), F32), pltpu.VMEM((1, r), F32), pltpu.VMEM((B_DH, r), F32)],
        input_output_aliases={5: 0},
        compiler_params=_cparams(("parallel", "parallel", "parallel", "arbitrary")),
        name="gqa_attention",
    )(qt, k, vt, km, vmt, buf)


def _diff_kernel(qt_ref, k_ref, vt_ref, km_ref, vmt_ref, tt_ref, bm_ref, cst_ref, buf_ref, o_ref,
                 m_scr, l_scr, acc_scr, *, meta_q):
    del buf_ref
    qi, ki = pl.program_id(2), pl.program_id(3)
    tq, tk = qt_ref.shape[1], k_ref.shape[0]
    nt = tk // LANES
    qt = qt_ref[...]
    dim = lax.broadcasted_iota(jnp.int32, qt.shape, 0)
    zero = jnp.zeros_like(qt)
    maps = (jnp.where(dim < C_DQK, qt, zero), jnp.where(dim >= C_DQK, qt, zero))
    c_left = cst_ref[0, 0:1, 0:1]
    c_right = cst_ref[0, 1:2, 0:1]
    lam = cst_ref[0, 2:3, 0:1]
    strips = [(mp, cols) for mp in range(2) for cols in _strips(tq)]
    scr = lambda mp, cols: slice(mp * tq + cols.start, mp * tq + cols.stop)

    @pl.when(ki == 0)
    def _():
        _attn_init(m_scr, l_scr, acc_scr)
        vm_ext = _with_ones(vmt_ref[0])
        for mp, cols in strips:
            s_t = jnp.dot(km_ref[0], maps[mp][:, cols], preferred_element_type=F32) + bm_ref[0, 0, :, cols]
            _strip_update(s_t, 0.0, vm_ext, m_scr, l_scr, acc_scr, scr(mp, cols))

    k = k_ref[...]
    v_ext = _with_ones(vt_ref[...])

    def tile_bias(off):
        if off <= -2:
            return jnp.broadcast_to(c_left, (LANES, LANES))
        if off >= 2:
            return jnp.broadcast_to(c_right, (LANES, LANES))
        return tt_ref[0, off + 1]

    def strip_bias(dd, cols):
        q_tiles = range(cols.start // LANES, cols.stop // LANES)
        if meta_q:
            rows = [[tt_ref[0, 0]]] + [[tile_bias(2)] for _ in range(nt - 1)]
        else:
            rows = [[tile_bias(dd * nt + b - a) for a in q_tiles] for b in range(nt)]
        return jnp.concatenate([jnp.concatenate(r, axis=1) for r in rows], axis=0)

    def run(c, bias):
        def scores(mp, cols):
            s_t = jnp.dot(k, maps[mp][:, cols], preferred_element_type=F32)
            return s_t if bias is None else s_t + bias(cols)

        _run_strips([functools.partial(scores, mp, cols) for mp, cols in strips],
                    [scr(mp, cols) for mp, cols in strips], c, v_ext, m_scr, l_scr, acc_scr)

    ratio = max(tq // tk, 1)
    d = ki - qi * ratio
    specials = (0,) if meta_q else tuple(range(-1, ratio + 1))
    for dd in specials:
        @pl.when(d == dd)
        def _(dd=dd):
            run(0.0, functools.partial(strip_bias, dd))

    @pl.when((d < specials[0]) | (d > specials[-1]))
    def _():
        run(jnp.where(d < 0, c_left, c_right), None)

    @pl.when(ki == pl.num_programs(3) - 1)
    def _():
        o1 = acc_scr[:, :tq] / l_scr[:, :tq]
        o2 = acc_scr[:, tq:] / l_scr[:, tq:]
        o_ref[...] = (o1 - lam * o2).T


def _diff_attention(qt, k, vt, km, vmt, tt, bm, cst, buf, n_seq, n_q, n_k, q_col0, k_row0, tq, tk, meta_q):
    nq, nk = n_q // tq, n_k // tk
    qb0, kb0 = q_col0 // tq, k_row0 // tk
    r = 2 * tq
    return pl.pallas_call(
        functools.partial(_diff_kernel, meta_q=meta_q),
        grid=(n_seq, C_HEADS, nq, nk),
        in_specs=[pl.BlockSpec((LANES, tq), lambda s, h, i, j: (h, qb0 + s * nq + i)),
                  pl.BlockSpec((tk, LANES), lambda s, h, i, j: (kb0 + s * nk + j, h)),
                  pl.BlockSpec((C_DV, tk), lambda s, h, i, j: (h, kb0 + s * nk + j)),
                  pl.BlockSpec((1, LANES, LANES), lambda s, h, i, j: (s, 0, h)),
                  pl.BlockSpec((1, C_DV, LANES), lambda s, h, i, j: (s, h, 0)),
                  pl.BlockSpec((1, 3, LANES, LANES), lambda s, h, i, j: (h, 0, 0, 0)),
                  pl.BlockSpec((1, 1, LANES, tq), lambda s, h, i, j: (h, jnp.minimum(i, 1), 0, 0)),
                  pl.BlockSpec((1, 8, LANES), lambda s, h, i, j: (h, 0, 0)),
                  pl.BlockSpec(memory_space=pl.ANY)],
        out_specs=pl.BlockSpec((tq, C_DV), lambda s, h, i, j: (qb0 + s * nq + i, h)),
        out_shape=jax.ShapeDtypeStruct(buf.shape, F32),
        scratch_shapes=[pltpu.VMEM((1, r), F32), pltpu.VMEM((1, r), F32), pltpu.VMEM((C_DV, r), F32)],
        input_output_aliases={8: 0},
        compiler_params=_cparams(("parallel", "parallel", "parallel", "arbitrary")),
        name="diff_attention",
    )(qt, k, vt, km, vmt, tt, bm, cst, buf)


def _tri_masks(rev):
    i = lax.broadcasted_iota(jnp.int32, (CHUNK, CHUNK), 0)
    j = lax.broadcasted_iota(jnp.int32, (CHUNK, CHUNK), 1)
    if rev:
        return j >= i, j > i, i, j
    return j <= i, j < i, i, j


def _cum_mat_t(rev):
    t = lax.broadcasted_iota(jnp.int32, (CHUNK, CHUNK), 0)
    j = lax.broadcasted_iota(jnp.int32, (CHUNK, CHUNK), 1)
    return jnp.where((t >= j) if rev else (t <= j), 1.0, 0.0)


def _gdn_kernel(qf_ref, kf_ref, vf_ref, cf_ref, rf_ref, qb_ref, kb_ref, vb_ref, cb_ref, rb_ref,
                s0f_ref, s0b_ref, bf_buf, bb_buf, of_ref, ob_ref, sff_ref, sfb_ref, s_scr):
    del bf_buf, bb_buf
    c = pl.program_id(2)
    nv = 2 * A_HG

    @pl.when(c == 0)
    def _():
        s_scr[0] = s0f_ref[0]
        s_scr[1] = s0b_ref[0]

    streams = ((qf_ref, kf_ref, vf_ref, cf_ref, rf_ref, False), (qb_ref, kb_ref, vb_ref, cb_ref, rb_ref, True))
    units = [(d, e) for d in range(2) for e in range(nv)]
    eye = jnp.where(_tri_masks(False)[0] & _tri_masks(True)[0], 1.0, 0.0)
    q, k, kk, qk, gc_all, gr_all, col, masks = [], [], [], [], [], [], [], []
    for q_ref, k_ref, _, c_ref, r_ref, rev in streams:
        tri, strict, _, _ = _tri_masks(rev)
        masks.append((tri, strict))
        qd = [q_ref[:, j * A_DK:(j + 1) * A_DK] for j in range(A_HG)]
        kd = [k_ref[:, j * A_DK:(j + 1) * A_DK] for j in range(A_HG)]
        q.append(qd)
        k.append(kd)
        kk.append([_dot_nt(x, x) for x in kd])
        qk.append([_dot_nt(x, y) for x, y in zip(qd, kd)])
        cv = c_ref[...]
        col.append(cv)
        gc_all.append(_dot_exact_lhs(jnp.where(tri, 1.0, 0.0), cv))
        gr_all.append(_dot_exact_rhs(r_ref[0, 0], _cum_mat_t(rev)))
    gcb = [jnp.broadcast_to(gc_all[d][:, e:e + 1], (CHUNK, LANES)) for d, e in units]
    beta = [jnp.broadcast_to(col[d][:, nv + e:nv + e + 1], (CHUNK, LANES)) for d, e in units]
    decay = [jnp.exp(jnp.where(masks[d][0], gcb[u][:, :CHUNK] - gr_all[d][e:e + 1, :], -jnp.inf))
             for u, (d, e) in enumerate(units)]
    a = [jnp.where(masks[d][1], beta[u][:, :CHUNK] * kk[d][e // 2] * decay[u], 0.0)
         for u, (d, e) in enumerate(units)]
    x = [eye - au for au in a]
    p = a
    for _ in range(5):
        p = [_dot(pu, pu) for pu in p]
        x = [xu + _dot(xu, pu) for xu, pu in zip(x, p)]
    egc = [jnp.exp(g) for g in gcb]
    v = [streams[d][2][:, e * A_DV:(e + 1) * A_DV] for d, e in units]
    uw = [_dot(x[u], jnp.concatenate([v[u] * beta[u], k[d][e // 2] * (beta[u] * egc[u])], axis=1))
          for u, (d, e) in enumerate(units)]
    s = [s_scr[d, e] for d, e in units]
    r2 = [_dot(jnp.concatenate([uw[u][:, A_DV:], q[d][e // 2] * egc[u]], axis=0), s[u])
          for u, (d, e) in enumerate(units)]
    v_new = [uw[u][:, :A_DV] - r2[u][:CHUNK] for u in range(len(units))]
    o = [r2[u][CHUNK:] + _dot(qk[d][e // 2] * decay[u], v_new[u]) for u, (d, e) in enumerate(units)]
    gtot = [gcb[u][0:1, :] if d else gcb[u][CHUNK - 1:CHUNK, :] for u, (d, e) in enumerate(units)]
    s_new = [s[u] * jnp.exp(gtot[u]) + _dot_tn(k[d][e // 2] * jnp.exp(gtot[u] - gcb[u]), v_new[u])
             for u, (d, e) in enumerate(units)]
    for u, (d, e) in enumerate(units):
        (ob_ref if d else of_ref)[:, e * A_DV:(e + 1) * A_DV] = o[u]
        s_scr[d, e] = s_new[u]

    @pl.when(c == pl.num_programs(2) - 1)
    def _():
        sff_ref[0] = s_scr[0]
        sfb_ref[0] = s_scr[1]


def _gdn_scan(qkv, colf, colb, rowf, rowb, s0f, s0b, buf_f, buf_b, n_seq, n_tok, row0):
    nc = n_tok // CHUNK
    cb0 = row0 // CHUNK
    ng = A_QK_HEADS // A_HG
    nv = 2 * A_HG
    fwd = lambda s, j, c: cb0 + s * nc + c
    bwd = lambda s, j, c: cb0 + s * nc + (nc - 1 - c)
    k_blk, v_blk = A_QK // (A_HG * A_DK), 2 * A_QK // (nv * A_DV)

    def stream_specs(ix):
        return [pl.BlockSpec((CHUNK, A_HG * A_DK), lambda s, j, c: (ix(s, j, c), j)),
                pl.BlockSpec((CHUNK, A_HG * A_DK), lambda s, j, c: (ix(s, j, c), k_blk + j)),
                pl.BlockSpec((CHUNK, nv * A_DV), lambda s, j, c: (ix(s, j, c), v_blk + j)),
                pl.BlockSpec((CHUNK, LANES), lambda s, j, c: (ix(s, j, c), j)),
                pl.BlockSpec((1, 1, nv, CHUNK), lambda s, j, c: (ix(s, j, c), j, 0, 0))]

    st_spec = pl.BlockSpec((1, nv, A_DK, A_DV), lambda s, j, c: (s, j, 0, 0))
    st_shape = jax.ShapeDtypeStruct((n_seq, A_V_HEADS, A_DK, A_DV), F32)
    any_spec = pl.BlockSpec(memory_space=pl.ANY)
    return pl.pallas_call(
        _gdn_kernel,
        grid=(n_seq, ng, nc),
        in_specs=stream_specs(fwd) + stream_specs(bwd) + [st_spec, st_spec, any_spec, any_spec],
        out_specs=[pl.BlockSpec((CHUNK, nv * A_DV), lambda s, j, c: (fwd(s, j, c), j)),
                   pl.BlockSpec((CHUNK, nv * A_DV), lambda s, j, c: (bwd(s, j, c), j)),
                   st_spec, st_spec],
        out_shape=[jax.ShapeDtypeStruct(buf_f.shape, F32), jax.ShapeDtypeStruct(buf_b.shape, F32),
                   st_shape, st_shape],
        scratch_shapes=[pltpu.VMEM((2, nv, A_DK, A_DV), F32)],
        input_output_aliases={12: 0, 13: 1},
        compiler_params=_cparams(("parallel", "parallel", "arbitrary")),
        name="gdn_scan",
    )(qkv, qkv, qkv, colf, rowf, qkv, qkv, qkv, colb, rowb, s0f, s0b, buf_f, buf_b)


GLA_LEVELS = (5, 4, 3, 2, 1, 0)
GLA_SAFE_EXP = 60.0


def _gla_level(i, j, lv, rev):
    sup = lv + 1
    base = jnp.left_shift(jnp.right_shift(i, sup), sup)
    ref_row = base + (1 << lv) - (0 if rev else 1)
    same = jnp.right_shift(i, sup) == jnp.right_shift(j, sup)
    i_late = jnp.bitwise_and(jnp.right_shift(i, lv), 1) == (0 if rev else 1)
    j_early = jnp.bitwise_and(jnp.right_shift(j, lv), 1) == (1 if rev else 0)
    return jnp.where(j == ref_row, 1.0, 0.0), same & i_late & j_early


def _rows_bcast(x, rows, span):
    zero = jnp.zeros((1, x.shape[1]), x.dtype)
    picked = [x[r:r + 1] if 0 <= r < x.shape[0] else zero for r in rows]
    return jnp.concatenate([jnp.broadcast_to(r, (span, x.shape[1])) for r in picked], axis=0)


def _gla_block16(i, j, rev):
    blk = jnp.right_shift(i, 4)
    ref_row = jnp.left_shift(blk + 1, 4) if rev else jnp.left_shift(blk, 4) - 1
    tri = (j >= i) if rev else (j <= i)
    return jnp.where(j == ref_row, 1.0, 0.0), tri & (blk == jnp.right_shift(j, 4))


def _gla_pair(q, k, bc, ref, key_side_nonpositive):
    ek = ref - bc
    if key_side_nonpositive:
        ek = jnp.minimum(ek, 0.0)
    return (q * jnp.exp(jnp.minimum(bc - ref, 0.0))).astype(BF16), (k * jnp.exp(ek)).astype(BF16)


def _gla_kernel(qf_ref, kf_ref, vf_ref, lf_ref, qb_ref, kb_ref, vb_ref, lb_ref, w2_ref, gb_ref,
                s0f_ref, s0b_ref, bf_buf, bb_buf, of_ref, ob_ref, sff_ref, sfb_ref, s_scr, a_scr):
    del bf_buf, bb_buf
    c = pl.program_id(1)

    @pl.when(c == 0)
    def _():
        s_scr[0] = s0f_ref[0]
        s_scr[1] = s0b_ref[0]

    streams = ((qf_ref, kf_ref, vf_ref, lf_ref, False), (qb_ref, kb_ref, vb_ref, lb_ref, True))
    heads = [slice(hd * D_DK, (hd + 1) * D_DK) for hd in range(D_HEADS)]
    units = [(d, hd) for d in range(2) for hd in range(D_HEADS)]
    ops = []
    for d, (q_ref, k_ref, v_ref, l_ref, rev) in enumerate(streams):
        tri, _, i, j = _tri_masks(rev)
        x = _dot(l_ref[...], w2_ref[d]) + gb_ref[d]
        lg = (jnp.minimum(x, 0.0) - jnp.log(1.0 + jnp.exp(-jnp.abs(x)))) * (1.0 / GATE_TAU)
        bc = _dot_exact_lhs(jnp.where(tri, 1.0, 0.0), lg)
        btot = bc[0:1] if rev else bc[CHUNK - 1:CHUNK]
        q, k = q_ref[...] * D_DK ** -0.5, k_ref[...]
        def level(lv, q=q, k=k, bc=bc, i=i, j=j, rev=rev):
            pick, mask = _gla_level(i, j, lv, rev)
            if (1 << lv) >= 8:
                span = 2 << lv
                ref = _rows_bcast(bc, [b + (1 << lv) - (0 if rev else 1) for b in range(0, CHUNK, span)], span)
            else:
                ref = _dot_exact_lhs(pick, bc)
            return (mask,) + _gla_pair(q, k, bc, ref, True)

        _, mask16 = _gla_block16(i, j, rev)
        ref16 = _rows_bcast(bc, [b + 16 if rev else b - 1 for b in range(0, CHUNK, 16)], 16)
        ops.append(dict(level=level, coarse=[level(lv) for lv in GLA_LEVELS[:2]],
                        block16=(mask16,) + _gla_pair(q, k, bc, ref16, False), risk=jnp.max(ref16 - bc),
                        diag=(i == j, q.astype(BF16), k.astype(BF16)),
                        qg=(q * jnp.exp(bc)).astype(BF16), kd=(k * jnp.exp(btot - bc)).astype(BF16),
                        etot=jnp.exp(btot), v=v_ref[...].astype(BF16)))

    def term(pair, hd):
        mask, qe, ke = pair
        return jnp.where(mask, _dot_nt(qe[:, heads[hd]], ke[:, heads[hd]]), 0.0)

    a_coarse = [term(ops[d]['coarse'][0], hd) + term(ops[d]['coarse'][1], hd) for d, hd in units]
    risk = jnp.maximum(ops[0]['risk'], ops[1]['risk'])

    @pl.when(risk <= GLA_SAFE_EXP)
    def _():
        for u, (d, hd) in enumerate(units):
            a_scr[u] = a_coarse[u] + term(ops[d]['block16'], hd)

    @pl.when(risk > GLA_SAFE_EXP)
    def _():
        fine = [[ops[d]['diag']] + [ops[d]['level'](lv) for lv in GLA_LEVELS[2:]] for d in range(2)]
        for u, (d, hd) in enumerate(units):
            a_scr[u] = a_coarse[u] + functools.reduce(lambda x, y: x + y, [term(pr, hd) for pr in fine[d]])

    a = [a_scr[u] for u in range(len(units))]
    st = [s_scr[d, hd] for d, hd in units]
    vs = [ops[d]['v'][:, hd * D_DV:(hd + 1) * D_DV] for d, hd in units]
    o = [_dot_nt(ops[d]['qg'][:, heads[hd]], st[u]) + _dot(a[u], vs[u]) for u, (d, hd) in enumerate(units)]
    st_new = [st[u] * ops[d]['etot'][:, heads[hd]] + _dot_tn(vs[u], ops[d]['kd'][:, heads[hd]])
              for u, (d, hd) in enumerate(units)]
    for u, (d, hd) in enumerate(units):
        (ob_ref if d else of_ref)[:, hd * D_DV:(hd + 1) * D_DV] = o[u]
        s_scr[d, hd] = st_new[u]

    @pl.when(c == pl.num_programs(1) - 1)
    def _():
        sff_ref[0] = s_scr[0]
        sfb_ref[0] = s_scr[1]


def _gla_scan(p, lr, w2, gb, s0f, s0b, buf_f, buf_b, n_seq, n_tok, row0):
    nc = n_tok // CHUNK
    cb0 = row0 // CHUNK
    fwd = lambda s, c: cb0 + s * nc + c
    bwd = lambda s, c: cb0 + s * nc + (nc - 1 - c)

    def stream_specs(ix):
        return [pl.BlockSpec((CHUNK, D_QKW), lambda s, c: (ix(s, c), 0)),
                pl.BlockSpec((CHUNK, D_QKW), lambda s, c: (ix(s, c), 1)),
                pl.BlockSpec((CHUNK, D_VW), lambda s, c: (ix(s, c), 2 * D_QKW // D_VW)),
                pl.BlockSpec((CHUNK, 2 * GATE_RANK), lambda s, c: (ix(s, c), 0))]

    st_spec = pl.BlockSpec((1, D_HEADS, D_DV, D_DK), lambda s, c: (s, 0, 0, 0))
    st_shape = jax.ShapeDtypeStruct((n_seq, D_HEADS, D_DV, D_DK), F32)
    any_spec = pl.BlockSpec(memory_space=pl.ANY)
    return pl.pallas_call(
        _gla_kernel,
        grid=(n_seq, nc),
        in_specs=stream_specs(fwd) + stream_specs(bwd) + [
            pl.BlockSpec((2, 2 * GATE_RANK, D_QKW), lambda s, c: (0, 0, 0)),
            pl.BlockSpec((2, 1, D_QKW), lambda s, c: (0, 0, 0)),
            st_spec, st_spec, any_spec, any_spec],
        out_specs=[pl.BlockSpec((CHUNK, D_VW), lambda s, c: (fwd(s, c), 0)),
                   pl.BlockSpec((CHUNK, D_VW), lambda s, c: (bwd(s, c), 0)),
                   st_spec, st_spec],
        out_shape=[jax.ShapeDtypeStruct(buf_f.shape, F32), jax.ShapeDtypeStruct(buf_b.shape, F32),
                   st_shape, st_shape],
        scratch_shapes=[pltpu.VMEM((2, D_HEADS, D_DV, D_DK), F32), pltpu.VMEM((2 * D_HEADS, CHUNK, CHUNK), F32)],
        input_output_aliases={12: 0, 13: 1},
        compiler_params=_cparams(("parallel", "arbitrary")),
        name="gla_scan",
    )(p, p, p, lr, p, p, p, lr, w2, gb, s0f, s0b, buf_f, buf_b)


def _gdn_prep_kernel(x_ref, prev_ref, next_ref, edge_ref, w_ref, o_ref, *, starts, ends):
    i, j = pl.program_id(0), pl.program_id(1)
    tm, cw = x_ref.shape
    r0 = i * tm
    is_start = functools.reduce(jnp.logical_or, [r0 == s for s in starts])
    is_end = functools.reduce(jnp.logical_or, [r0 + tm == e for e in ends])
    x = x_ref[...]
    prev_row = jnp.where(is_start, edge_ref[0, 7:8, :], prev_ref[7:8, :])
    next_row = jnp.where(is_end, 0.0, next_ref[0:1, :])
    rows = lax.broadcasted_iota(jnp.int32, (tm, 1), 0)
    xp = jnp.where(rows == 0, prev_row, pltpu.roll(x, 1, axis=0))
    xn = jnp.where(rows == tm - 1, next_row, pltpu.roll(x, tm - 1, axis=0))
    y = xp * w_ref[0:1, :] + x * w_ref[1:2, :] + xn * w_ref[2:3, :]
    y = y * _sigmoid(y)
    is_qk = j < 2 * A_QK // cw
    q_scale = jnp.where(j < A_QK // cw, A_DK ** -0.5, 1.0)
    for c0 in range(0, cw, A_DK):
        yh = y[:, c0:c0 + A_DK]
        nrm = yh * (lax.rsqrt(jnp.sum(yh * yh, axis=-1, keepdims=True) + EPS) * q_scale)
        o_ref[:, c0:c0 + A_DK] = jnp.where(is_qk, nrm, yh)


def _gdn_prep(lay, x, edge, conv_w):
    r, c = x.shape
    tm, cw = 512, 512
    starts = [row + s * n for b, n, row, _ in lay.groups for s in range(b)]
    ends = [row + (s + 1) * n for b, n, row, _ in lay.groups for s in range(b)]
    seq_of = lambda i: sum((i * tm >= s).astype(jnp.int32) for s in starts) - 1
    w8 = jnp.pad(conv_w.astype(F32), ((0, 8 - conv_w.shape[0]), (0, 0)))
    return pl.pallas_call(
        functools.partial(_gdn_prep_kernel, starts=starts, ends=ends),
        grid=(r // tm, c // cw),
        in_specs=[pl.BlockSpec((tm, cw), lambda i, j: (i, j)),
                  pl.BlockSpec((8, cw), lambda i, j: (jnp.maximum(i * (tm // 8) - 1, 0), j)),
                  pl.BlockSpec((8, cw), lambda i, j: (jnp.minimum((i + 1) * (tm // 8), r // 8 - 1), j)),
                  pl.BlockSpec((1, 8, cw), lambda i, j: (seq_of(i), 0, j)),
                  pl.BlockSpec((8, cw), lambda i, j: (0, j))],
        out_specs=pl.BlockSpec((tm, cw), lambda i, j: (i, j)),
        out_shape=jax.ShapeDtypeStruct((r, c), F32),
        compiler_params=_cparams(("parallel", "parallel")),
        name="gdn_prep",
    )(x, x, x, edge, w8)


def _gqa_prep_kernel(q_ref, kv_ref, cos_ref, sin_ref, qg_ref, kg_ref, qt_ref, k_ref, vt_ref):
    cos, sin = cos_ref[...], sin_ref[...]

    def norm_rope(x, g):
        xn = x * lax.rsqrt(jnp.mean(x * x, axis=-1, keepdims=True) + EPS) * g
        return xn * cos + pltpu.roll(xn, B_DH // 2, axis=1) * sin

    for hd in range(B_HEADS):
        sl = slice(hd * B_DH, (hd + 1) * B_DH)
        qt_ref[sl, :] = norm_rope(q_ref[:, sl], qg_ref[...]).T.astype(BF16)
    for hd in range(B_KV_HEADS):
        sl = slice(hd * B_DH, (hd + 1) * B_DH)
        k_ref[:, sl] = norm_rope(kv_ref[:, sl], kg_ref[...]).astype(BF16)
        vt_ref[sl, :] = kv_ref[:, B_KVW + hd * B_DH:B_KVW + (hd + 1) * B_DH].T.astype(BF16)


def _pos_block(lay, tm):
    def pos(i):
        r = i * tm
        out = 0
        for b, n, row, _ in lay.groups:
            inside = (r >= row) & (r < row + b * n)
            out = out + jnp.where(inside, ((r - row) % n) // tm, 0)
        return out
    return pos


def _gqa_prep(lay, pq, pkv, cos, sin, qg, kg):
    r = pq.shape[0]
    tm = min(512, min(n for _, n, _, _ in lay.groups))
    pos = _pos_block(lay, tm)
    row = lambda i: (i, 0)
    col = lambda i: (0, i)
    fixed = lambda i: (0, 0)
    return pl.pallas_call(
        _gqa_prep_kernel,
        grid=(r // tm,),
        in_specs=[pl.BlockSpec((tm, B_QW), row), pl.BlockSpec((tm, 2 * B_KVW), row),
                  pl.BlockSpec((tm, B_DH), lambda i: (pos(i), 0)), pl.BlockSpec((tm, B_DH), lambda i: (pos(i), 0)),
                  pl.BlockSpec((1, B_DH), fixed), pl.BlockSpec((1, B_DH), fixed)],
        out_specs=[pl.BlockSpec((B_QW, tm), col), pl.BlockSpec((tm, B_KVW), row), pl.BlockSpec((B_KVW, tm), col)],
        out_shape=[jax.ShapeDtypeStruct((B_QW, r), BF16), jax.ShapeDtypeStruct((r, B_KVW), BF16),
                   jax.ShapeDtypeStruct((B_KVW, r), BF16)],
        compiler_params=_cparams(("parallel",)),
        name="gqa_prep",
    )(pq, pkv, cos, sin, qg.reshape(1, B_DH), kg.reshape(1, B_DH))


def _diff_prep_kernel(q_ref, k_ref, v_ref, qg_ref, kg_ref, qt_ref, ko_ref, vt_ref):
    lo = lax.broadcasted_iota(jnp.int32, (1, LANES), 1) < C_DQK

    def norm_maps(x, g):
        sq = x * x
        s_lo = jnp.sum(jnp.where(lo, sq, 0.0), axis=-1, keepdims=True)
        s_hi = jnp.sum(jnp.where(lo, 0.0, sq), axis=-1, keepdims=True)
        inv = jnp.where(lo, lax.rsqrt(s_lo * (1.0 / C_DQK) + EPS), lax.rsqrt(s_hi * (1.0 / C_DQK) + EPS))
        return x * inv * g

    qt_ref[...] = norm_maps(q_ref[...], qg_ref[...]).T.astype(BF16)
    ko_ref[...] = norm_maps(k_ref[...], kg_ref[...]).astype(BF16)
    vt_ref[...] = v_ref[...].T.astype(BF16)


def _diff_prep(p, qg, kg):
    r = p.shape[0]
    tm = 512 if r % 512 == 0 else LANES
    fixed = lambda i, h: (0, 0)
    return pl.pallas_call(
        _diff_prep_kernel,
        grid=(r // tm, C_HEADS),
        in_specs=[pl.BlockSpec((tm, LANES), lambda i, h: (i, h)),
                  pl.BlockSpec((tm, LANES), lambda i, h: (i, C_HEADS + h)),
                  pl.BlockSpec((tm, LANES), lambda i, h: (i, 2 * C_HEADS + h)),
                  pl.BlockSpec((1, LANES), fixed), pl.BlockSpec((1, LANES), fixed)],
        out_specs=[pl.BlockSpec((LANES, tm), lambda i, h: (h, i)), pl.BlockSpec((tm, LANES), lambda i, h: (i, h)),
                   pl.BlockSpec((LANES, tm), lambda i, h: (h, i))],
        out_shape=[jax.ShapeDtypeStruct((C_QKW, r), BF16), jax.ShapeDtypeStruct((r, C_QKW), BF16),
                   jax.ShapeDtypeStruct((C_VW, r), BF16)],
        compiler_params=_cparams(("parallel", "parallel")),
        name="diff_prep",
    )(p, p, p, qg.reshape(1, LANES), kg.reshape(1, LANES))


class _Layout:
    def __init__(self, shapes):
        self.groups = []
        row, seq = 0, 0
        for b, n in shapes:
            self.groups.append((b, n, row, seq))
            row += b * n
            seq += b
        self.rows, self.n_seq = row, seq

    def to_seqs(self, main, meta):
        out = []
        c = main.shape[-1]
        for b, n, row, seq in self.groups:
            m = meta[seq * N_META:(seq + b) * N_META].reshape(b, N_META, c)
            x = main[row:row + b * n].reshape(b, n, c)
            out.append(jnp.concatenate([m, x], axis=1))
        return out

    def from_seqs(self, seqs):
        main = jnp.concatenate([s[:, N_META:].reshape(-1, s.shape[-1]) for s in seqs], axis=0)
        meta = jnp.concatenate([s[:, :N_META].reshape(-1, s.shape[-1]) for s in seqs], axis=0)
        return main, meta


def _meta_chunk(meta):
    c = meta.shape[-1]
    m = meta.reshape(-1, N_META, c)
    return jnp.pad(m, ((0, 0), (CHUNK - N_META, 0), (0, 0))).reshape(-1, c)


def _meta_unchunk(x):
    c = x.shape[-1]
    return x.reshape(-1, CHUNK, c)[:, CHUNK - N_META:].reshape(-1, c)


def _map_seqs(lay, fn, main, meta):
    return lay.from_seqs([fn(s) for s in lay.to_seqs(main, meta)])


def _run_scan(scan, lay, main_args, meta_args, state_shape, width):
    s_all = lay.n_seq
    zeros = jnp.zeros((s_all,) + state_shape, F32)
    mrows = s_all * CHUNK
    mbuf = jnp.zeros((mrows, width), F32)
    _, _, s_meta, _ = scan(*meta_args, zeros, zeros, mbuf, mbuf, s_all, CHUNK, 0)
    of = jnp.zeros((lay.rows, width), F32)
    ob = jnp.zeros((lay.rows, width), F32)
    finals = []
    for b, n, row, seq in lay.groups:
        of, ob, _, sfb = scan(*main_args, s_meta[seq:seq + b], zeros[seq:seq + b], of, ob, b, n, row)
        finals.append(sfb)
    mof, mob, _, _ = scan(*meta_args, zeros, jnp.concatenate(finals, axis=0), mbuf, mbuf, s_all, CHUNK, 0)
    return (of, ob), (_meta_unchunk(mof), _meta_unchunk(mob))


def _gdn_layer(lay, h, hm, norm_g, w_in, conv_w, a_log, dt_bias, out_g, w_out):
    w_qkv, w_z, w_ab = (w_in[:, :A_CONV_CH], w_in[:, A_CONV_CH:A_CONV_CH + A_VW], w_in[:, A_CONV_CH + A_VW:])
    x, xm = _norm_proj(h, norm_g, w_qkv), _norm_proj(hm, norm_g, w_qkv)
    z, zm = _norm_proj(h, norm_g, w_z), _norm_proj(hm, norm_g, w_z)
    ab, abm = _norm_proj(h, norm_g, w_ab), _norm_proj(hm, norm_g, w_ab)
    s_all = lay.n_seq
    xm3 = xm.reshape(s_all, N_META, A_CONV_CH)
    qkv = _gdn_prep(lay, x, xm3[:, N_META - 8:], conv_w)

    def prep(x):
        prev = jnp.pad(x[:, :-1], ((0, 0), (1, 0), (0, 0)))
        nxt = jnp.pad(x[:, 1:], ((0, 0), (0, 1), (0, 0)))
        y = jax.nn.silu(prev * conv_w[0] + x * conv_w[1] + nxt * conv_w[2])
        b, l, _ = y.shape
        qk = y[..., :2 * A_QK].reshape(b, l, 2 * A_QK_HEADS, A_DK)
        qk = qk * lax.rsqrt(jnp.sum(qk * qk, axis=-1, keepdims=True) + EPS)
        qk = qk.reshape(b, l, 2, A_QK) * jnp.array([A_DK ** -0.5, 1.0], F32)[:, None]
        return jnp.concatenate([qk.reshape(b, l, 2 * A_QK), y[..., 2 * A_QK:]], axis=-1)

    first = x[np.array([row + s * n for b, n, row, _ in lay.groups for s in range(b)])]
    qkvm = prep(jnp.concatenate([xm3, first[:, None]], axis=1))[:, :N_META].reshape(-1, A_CONV_CH)

    def gates(x):
        x = x.reshape(-1, 2, 2, A_V_HEADS)
        g = -jnp.exp(a_log.astype(F32)) * jax.nn.softplus(x[:, 0] + dt_bias.astype(F32))
        return g, jax.nn.sigmoid(x[:, 1])

    ng, nv = A_QK_HEADS // A_HG, 2 * A_HG

    def pack(g, beta, d):
        r = g.shape[0]
        gd, bd = g[:, d].reshape(r, ng, nv), beta[:, d].reshape(r, ng, nv)
        col = jnp.concatenate([gd, bd, jnp.zeros((r, ng, LANES - 2 * nv), F32)], axis=-1).reshape(r, ng * LANES)
        row = gd.reshape(r // CHUNK, CHUNK, ng, nv).transpose(0, 2, 3, 1)
        return col, row

    def scan_args(qkv_rows, g, beta):
        colf, rowf = pack(g, beta, 0)
        colb, rowb = pack(g, beta, 1)
        return (qkv_rows, colf, colb, rowf, rowb)

    g, beta = gates(ab)
    gm, betam = gates(abm)
    pad_gate = lambda t: _meta_chunk(t.reshape(-1, 2 * A_V_HEADS)).reshape(-1, 2, A_V_HEADS)
    (of, ob), (mof, mob) = _run_scan(
        _gdn_scan, lay, scan_args(qkv, g, beta), scan_args(_meta_chunk(qkvm), pad_gate(gm), pad_gate(betam)),
        (A_V_HEADS, A_DK, A_DV), A_VW)
    gain = jnp.tile(out_g.astype(F32), A_V_HEADS)
    return (_out_proj([of, ob], z, gain, w_out, h, A_DV), _out_proj([mof, mob], zm, gain, w_out, hm, A_DV))


def _rope_tables(n_tok):
    rows = n_tok // GRID_W
    row = jnp.repeat(jnp.arange(rows, dtype=F32), GRID_W)
    col = (jnp.arange(rows * GRID_W) % GRID_W).astype(F32)
    inv = ROPE_THETA ** (-jnp.arange(0, ROPE_AXIS, 2, dtype=F32) / ROPE_AXIS)
    ang = jnp.concatenate([row[:, None] * inv, col[:, None] * inv], axis=-1)
    return jnp.cos(ang), jnp.sin(ang)


def _pad_meta(x, n_seq):
    return jnp.pad(x.reshape(n_seq, N_META, -1), ((0, 0), (0, LANES - N_META), (0, 0)))


def _gqa_layer(lay, h, hm, norm_g, w_in, q_g, k_g, w_out):
    kv_end = B_QW + 2 * B_KVW
    pq, pkv, z = (_norm_proj(h, norm_g, w_in[:, :B_QW]), _norm_proj(h, norm_g, w_in[:, B_QW:kv_end]),
                  _norm_proj(h, norm_g, w_in[:, kv_end:]))
    pm = _norm_proj(hm, norm_g, w_in)

    def norm_heads(x, g):
        r = x.shape[0]
        x = x.reshape(r, -1, B_DH)
        return x * lax.rsqrt(jnp.mean(x * x, axis=-1, keepdims=True) + EPS) * g.astype(F32)

    scale = B_DH ** -0.5 * LOG2E
    cos, sin = _rope_tables(max(n for _, n, _, _ in lay.groups))
    qt, k, vt = _gqa_prep(lay, pq, pkv, jnp.concatenate([cos, cos], axis=-1), jnp.concatenate([-sin, sin], axis=-1),
                          q_g.astype(F32) * scale, k_g.astype(F32))
    s_all = lay.n_seq
    qm = (norm_heads(pm[:, :B_QW], q_g) * scale).reshape(-1, B_QW).astype(BF16)
    km = norm_heads(pm[:, B_QW:B_QW + B_KVW], k_g).reshape(-1, B_KVW).astype(BF16)
    vm = pm[:, B_QW + B_KVW:B_QW + 2 * B_KVW].astype(BF16)
    zm = pm[:, B_QW + 2 * B_KVW:]
    qmt = _pad_meta(qm, s_all).reshape(s_all * LANES, B_QW).T
    kmp, vmt = _pad_meta(km, s_all), _pad_meta(vm, s_all).transpose(0, 2, 1)

    o = jnp.zeros((lay.rows, B_QW), F32)
    om = jnp.zeros((s_all * LANES, B_QW), F32)
    for b, n, row, seq in lay.groups:
        tq, tk = min(GQA_TQ, n), min(GQA_TK, n)
        o = _gqa_attention(qt, k, vt, kmp[seq:seq + b], vmt[seq:seq + b], o, b, n, n, row, row, tq, tk)
        om = _gqa_attention(qmt, k, vt, kmp[seq:seq + b], vmt[seq:seq + b], om, b, LANES, n,
                            seq * LANES, row, LANES, tk)
    om = om.reshape(s_all, LANES, B_QW)[:, :N_META].reshape(-1, B_QW)
    ones = jnp.ones((B_QW,), F32)
    return _out_proj([o], z, ones, w_out, h, 0), _out_proj([om], zm, ones, w_out, hm, 0)


def _t5_bucket_np(rel):
    nb = REL_BUCKETS // 2
    max_exact = nb // 2
    n = np.abs(rel)
    steps = nb - max_exact
    large = np.full(n.shape, max_exact, np.int64)
    for kstep in range(1, steps + 1):
        large += (n.astype(np.float64) ** steps >= (max_exact ** steps) * float(REL_MAX_DIST // max_exact) ** kstep)
    large = np.minimum(large, nb - 1)
    return np.where(rel > 0, nb, 0) + np.where(n < max_exact, n, large)


def _diff_layer(lay, h, hm, norm_g, w_in, q_g, k_g, lam, sub_g, w_out, rel_bias, layer_idx):
    n_qkv = 2 * C_QKW + C_VW
    p, z = _norm_proj(h, norm_g, w_in[:, :n_qkv]), _norm_proj(h, norm_g, w_in[:, n_qkv:])
    pm = _norm_proj(hm, norm_g, w_in)

    def norm_maps(x, g):
        r = x.shape[0]
        x = x.reshape(r, -1, C_DQK)
        return (x * lax.rsqrt(jnp.mean(x * x, axis=-1, keepdims=True) + EPS) * g.astype(F32)).reshape(r, -1)

    s_all = lay.n_seq
    scale = C_DQK ** -0.5 * LOG2E
    qt, k, vt = _diff_prep(p, jnp.tile(q_g.astype(F32), 2) * scale, jnp.tile(k_g.astype(F32), 2))
    qm = (norm_maps(pm[:, :C_QKW], q_g) * scale).astype(BF16)
    km = norm_maps(pm[:, C_QKW:2 * C_QKW], k_g).astype(BF16)
    vm = pm[:, 2 * C_QKW:n_qkv].astype(BF16)
    zm = pm[:, n_qkv:]
    qmt = _pad_meta(qm, s_all).reshape(s_all * LANES, C_QKW).T
    kmp, vmt = _pad_meta(km, s_all), _pad_meta(vm, s_all).transpose(0, 2, 1)

    lam = lam.astype(F32)
    lam_init = 0.8 - 0.6 * math.exp(-0.3 * layer_idx)
    lam_full = jnp.exp(jnp.sum(lam[0] * lam[1])) - jnp.exp(jnp.sum(lam[2] * lam[3])) + lam_init
    table = rel_bias.astype(F32).T * LOG2E
    look = lambda rel: table[:, _t5_bucket_np(rel)]
    nb = REL_BUCKETS // 2
    cst = jnp.zeros((C_HEADS, 8, LANES), F32)
    cst = cst.at[:, 0].set(table[:, nb - 1:nb]).at[:, 1].set(table[:, 2 * nb - 1:2 * nb]).at[:, 2].set(lam_full)
    ar = np.arange(LANES)
    key_pad = jnp.asarray(np.where(ar < N_META, 0.0, NEG), F32)[:, None]
    tt = jnp.stack([look(o * LANES + ar[:, None] - ar[None, :]) for o in (-1, 0, 1)], axis=1)
    mq = np.minimum(ar, N_META - 1)
    tt_m = jnp.stack([look(N_META + ar[:, None] - mq[None, :])] * 3, axis=1)
    meta_rows = (ar < N_META)[:, None]
    bm_m = (jnp.where(meta_rows, look(ar[:, None] - mq[None, :]), 0.0) + key_pad)[:, None]

    o = jnp.zeros((lay.rows, C_VW), F32)
    om = jnp.zeros((s_all * LANES, C_VW), F32)
    for b, n, row, seq in lay.groups:
        t, tk = min(DIFF_TQ, n), min(DIFF_TK, n)
        pq = np.arange(t)
        near = look(ar[:N_META, None] - N_META - pq[None, :])
        far = jnp.broadcast_to(table[:, nb - 1][:, None, None], near.shape)
        pad_rows = jnp.full((C_HEADS, 2, LANES - N_META, t), NEG, F32)
        bm = jnp.concatenate([jnp.stack([near, far], axis=1), pad_rows], axis=2)
        o = _diff_attention(qt, k, vt, kmp[seq:seq + b], vmt[seq:seq + b], tt, bm, cst, o,
                            b, n, n, row, row, t, tk, False)
        om = _diff_attention(qmt, k, vt, kmp[seq:seq + b], vmt[seq:seq + b], tt_m, bm_m, cst, om,
                             b, LANES, n, seq * LANES, row, LANES, tk, True)
    om = om.reshape(s_all, LANES, C_VW)[:, :N_META].reshape(-1, C_VW)
    gain = jnp.tile(sub_g.astype(F32), C_HEADS) * (1.0 - lam_init)
    return _out_proj([o], z, gain, w_out, h, C_DV), _out_proj([om], zm, gain, w_out, hm, C_DV)


def _gla_layer(lay, h, hm, norm_g, w_in, gate_w2, gate_b, out_g, w_out):
    n_qkv = 2 * D_QKW + D_VW
    w_qkv, w_z, w_lr = w_in[:, :n_qkv], w_in[:, n_qkv:n_qkv + D_VW], w_in[:, n_qkv + D_VW:]
    p, pm = _norm_proj(h, norm_g, w_qkv), _norm_proj(hm, norm_g, w_qkv)
    z, zm = _norm_proj(h, norm_g, w_z), _norm_proj(hm, norm_g, w_z)
    lr, lrm = _norm_proj(h, norm_g, w_lr), _norm_proj(hm, norm_g, w_lr)
    zero = jnp.zeros((GATE_RANK, D_QKW), F32)
    w2 = jnp.stack([jnp.concatenate([gate_w2[0].astype(F32), zero], axis=0),
                    jnp.concatenate([zero, gate_w2[1].astype(F32)], axis=0)])
    gb = gate_b.astype(F32).reshape(2, 1, D_QKW)
    (of, ob), (mof, mob) = _run_scan(
        _gla_scan, lay, (p, lr, w2, gb), (_meta_chunk(pm), _meta_chunk(lrm), w2, gb), (D_HEADS, D_DV, D_DK), D_VW)
    gain = jnp.tile(out_g.astype(F32), D_HEADS)
    return _out_proj([of, ob], z, gain, w_out, h, D_DV), _out_proj([mof, mob], zm, gain, w_out, hm, D_DV)


def _trunk(xs, meta_tokens, rel_bias, a, b, c, d, depth):
    lay = _Layout([(x.shape[0], x.shape[1]) for x in xs])
    h = jnp.concatenate([x.reshape(-1, D_MODEL) for x in xs], axis=0)
    hm = jnp.tile(meta_tokens.astype(F32), (lay.n_seq, 1))
    for i in range(depth):
        m, j = i % 4, i // 4
        if m == 0:
            h, hm = _gdn_layer(lay, h, hm, *(t[j] for t in a))
        elif m == 1:
            h, hm = _gqa_layer(lay, h, hm, *(t[j] for t in b))
        elif m == 2:
            h, hm = _diff_layer(lay, h, hm, *(t[j] for t in c), rel_bias, i)
        else:
            h, hm = _gla_layer(lay, h, hm, *(t[j] for t in d))
    return [h[row:row + bsz * n].reshape(bsz, n, D_MODEL) for bsz, n, row, _ in lay.groups]


def kernel(x_prompt, x_sample, meta_tokens, rel_bias, a_norm, a_w_in, a_conv, a_a_log, a_dt_bias, a_out_norm, a_w_out, b_norm, b_w_in, b_q_norm, b_k_norm, b_w_out, c_norm, c_w_in, c_q_norm, c_k_norm, c_lambda, c_sub_norm, c_w_out, d_norm, d_w_in, d_gate_w2, d_gate_b, d_out_norm, d_w_out):
    y_prompt, y_sample = _trunk(
        [x_prompt, x_sample], meta_tokens, rel_bias,
        (a_norm, a_w_in, a_conv, a_a_log, a_dt_bias, a_out_norm, a_w_out),
        (b_norm, b_w_in, b_q_norm, b_k_norm, b_w_out),
        (c_norm, c_w_in, c_q_norm, c_k_norm, c_lambda, c_sub_norm, c_w_out),
        (d_norm, d_w_in, d_gate_w2, d_gate_b, d_out_norm, d_w_out), 4)
    return (y_prompt, y_sample)
```

```python
import functools
import math

import numpy as np
import jax
import jax.numpy as jnp
from jax import lax
from jax.experimental import pallas as pl
from jax.experimental.pallas import tpu as pltpu

F32 = jnp.float32
BF16 = jnp.bfloat16

D_MODEL = 1024
N_META = 16
CHUNK = 64
GRID_W = 64
EPS = 1e-6
LANES = 128
VMEM_LIMIT = 56 * 1024 * 1024
NEG = -1e30
LOG2E = math.log2(math.e)
STRIP = 1024
ONES_ROWS = 16
PROJ_TN_MAX = 2048
GQA_TQ, GQA_TK = 1024, 2048
DIFF_TQ, DIFF_TK = 2048, 1024

A_QK_HEADS, A_V_HEADS, A_DK, A_DV = 8, 16, 128, 128
A_QK, A_VW = A_QK_HEADS * A_DK, A_V_HEADS * A_DV
A_CONV_CH = 2 * A_QK + A_VW
A_HG = 8
B_HEADS, B_KV_HEADS, B_DH = 8, 2, 128
B_GROUP = B_HEADS // B_KV_HEADS
B_QW, B_KVW = B_HEADS * B_DH, B_KV_HEADS * B_DH
ROPE_AXIS = B_DH // 2
ROPE_THETA = 10000.0
C_HEADS, C_DQK, C_DV = 8, 64, 128
C_QKW, C_VW = C_HEADS * 2 * C_DQK, C_HEADS * C_DV
REL_BUCKETS, REL_MAX_DIST = 32, 128
D_HEADS, D_DK, D_DV = 4, 128, 256
D_QKW, D_VW = D_HEADS * D_DK, D_HEADS * D_DV
GATE_RANK, GATE_TAU = 16, 16.0


def _cparams(sem):
    return pltpu.CompilerParams(dimension_semantics=sem, vmem_limit_bytes=VMEM_LIMIT)


def _sigmoid(x):
    return 1.0 / (1.0 + jnp.exp(-x))


def _dot(a, b):
    return jnp.dot(a.astype(BF16), b.astype(BF16), preferred_element_type=F32)


def _dot_nt(a, b):
    return lax.dot_general(a.astype(BF16), b.astype(BF16), (((1,), (1,)), ((), ())),
                           preferred_element_type=F32)


def _dot_tn(a, b):
    return lax.dot_general(a.astype(BF16), b.astype(BF16), (((0,), (0,)), ((), ())),
                           preferred_element_type=F32)


def _split3(x):
    x1 = x.astype(BF16)
    r1 = x - x1.astype(F32)
    x2 = r1.astype(BF16)
    x3 = (r1 - x2.astype(F32)).astype(BF16)
    return x1, x2, x3


def _dot_exact_lhs(m01, x):
    m = m01.astype(BF16)
    x1, x2, x3 = _split3(x)
    return (jnp.dot(m, x1, preferred_element_type=F32) + jnp.dot(m, x2, preferred_element_type=F32)
            + jnp.dot(m, x3, preferred_element_type=F32))


def _dot_exact_rhs(x, m01):
    m = m01.astype(BF16)
    x1, x2, x3 = _split3(x)
    return (jnp.dot(x1, m, preferred_element_type=F32) + jnp.dot(x2, m, preferred_element_type=F32)
            + jnp.dot(x3, m, preferred_element_type=F32))


def _norm_proj_kernel(x_ref, g_ref, w_ref, o_ref):
    x = x_ref[...]
    ms = jnp.mean(x * x, axis=-1, keepdims=True)
    xn = (x * lax.rsqrt(ms + EPS) * g_ref[...]).astype(BF16)
    o_ref[...] = jnp.dot(xn, w_ref[...], preferred_element_type=F32)


def _norm_proj(x, g, w):
    m, k = x.shape
    n = w.shape[1]
    tm = 512 if m % 512 == 0 else m
    tn = max([t for t in range(256, PROJ_TN_MAX + 1, 256) if n % t == 0], default=n)
    return pl.pallas_call(
        _norm_proj_kernel,
        grid=(n // tn, m // tm),
        in_specs=[pl.BlockSpec((tm, k), lambda j, i: (i, 0)),
                  pl.BlockSpec((1, k), lambda j, i: (0, 0)),
                  pl.BlockSpec((k, tn), lambda j, i: (0, j))],
        out_specs=pl.BlockSpec((tm, tn), lambda j, i: (i, j)),
        out_shape=jax.ShapeDtypeStruct((m, n), F32),
        compiler_params=_cparams(("parallel", "parallel")),
        name="norm_proj",
    )(x, g.reshape(1, k).astype(F32), w.astype(BF16))


def _out_proj_kernel(*refs, n_o, hw):
    o_refs = refs[:n_o]
    z_ref, g_ref, w_ref, h_ref, out_ref = refs[n_o:]
    k = z_ref.shape[1]
    cw = hw if hw else 256
    acc = h_ref[...]
    for c0 in range(0, k, cw):
        o = o_refs[0][:, c0:c0 + cw]
        for r in o_refs[1:]:
            o = o + r[:, c0:c0 + cw]
        if hw:
            ms = jnp.mean(o * o, axis=-1, keepdims=True)
            o = o * lax.rsqrt(ms + EPS) * g_ref[:, c0:c0 + cw]
        z = z_ref[:, c0:c0 + cw]
        gated = o * (z * _sigmoid(z))
        acc = acc + jnp.dot(gated.astype(BF16), w_ref[c0:c0 + cw, :], preferred_element_type=F32)
    out_ref[...] = acc


def _out_proj(os, z, g, w, h, hw):
    m, k = z.shape
    n = w.shape[1]
    tm = (512 if k <= 1024 else 256) if m % 512 == 0 else m
    row = lambda i: (i, 0)
    fixed = lambda i: (0, 0)
    return pl.pallas_call(
        functools.partial(_out_proj_kernel, n_o=len(os), hw=hw),
        grid=(m // tm,),
        in_specs=[pl.BlockSpec((tm, k), row) for _ in os] + [
            pl.BlockSpec((tm, k), row), pl.BlockSpec((1, k), fixed),
            pl.BlockSpec((k, n), fixed), pl.BlockSpec((tm, n), row)],
        out_specs=pl.BlockSpec((tm, n), row),
        out_shape=jax.ShapeDtypeStruct((m, n), F32),
        compiler_params=_cparams(("parallel",)),
        name="out_proj",
    )(*os, z, g.reshape(1, k).astype(F32), w.astype(BF16), h)


def _strip_update(s_t, c, v_ext, m_scr, l_scr, acc_scr, cols):
    p, alpha = _strip_softmax(s_t, c, m_scr, cols)
    _strip_pv(p, alpha, v_ext, l_scr, acc_scr, cols)


def _strip_softmax(s_t, c, m_scr, cols):
    m_prev = m_scr[:, cols]
    m_new = jnp.maximum(m_prev, jnp.max(s_t, axis=0, keepdims=True) + c)
    m_scr[:, cols] = m_new
    return jnp.exp2(s_t - (m_new - c)).astype(BF16), jnp.exp2(m_prev - m_new)


def _strip_pv(p, alpha, v_ext, l_scr, acc_scr, cols):
    dv = acc_scr.shape[0]
    pv = jnp.dot(v_ext, p, preferred_element_type=F32)
    acc_scr[:, cols] = alpha * acc_scr[:, cols] + pv[:dv]
    l_scr[:, cols] = alpha * l_scr[:, cols] + pv[dv:dv + 1]


def _run_strips(score_fns, cols_list, c, v_ext, m_scr, l_scr, acc_scr):
    s_next = score_fns[0]()
    pending = None
    for n, cols in enumerate(cols_list):
        s_t, s_next = s_next, (score_fns[n + 1]() if n + 1 < len(score_fns) else None)
        if pending is not None:
            _strip_pv(pending[0], pending[1], v_ext, l_scr, acc_scr, pending[2])
        p, alpha = _strip_softmax(s_t, c, m_scr, cols)
        pending = (p, alpha, cols)
    _strip_pv(pending[0], pending[1], v_ext, l_scr, acc_scr, pending[2])


def _attn_init(m_scr, l_scr, acc_scr):
    m_scr[...] = jnp.full(m_scr.shape, NEG, F32)
    l_scr[...] = jnp.zeros(l_scr.shape, F32)
    acc_scr[...] = jnp.zeros(acc_scr.shape, F32)


def _with_ones(v_t):
    return jnp.concatenate([v_t, jnp.ones((ONES_ROWS, v_t.shape[1]), BF16)], axis=0)


def _strips(width):
    w = min(STRIP, width)
    return [slice(c0, c0 + w) for c0 in range(0, width, w)]


def _gqa_kernel(qt_ref, k_ref, vt_ref, km_ref, vmt_ref, buf_ref, o_ref, m_scr, l_scr, acc_scr):
    del buf_ref
    ki = pl.program_id(3)
    tq = qt_ref.shape[1]
    heads = [slice(g * B_DH, (g + 1) * B_DH) for g in range(B_GROUP)]

    @pl.when(ki == 0)
    def _():
        _attn_init(m_scr, l_scr, acc_scr)
        key = lax.broadcasted_iota(jnp.int32, (LANES, 1), 0)
        pad = jnp.where(key < N_META, 0.0, NEG)
        vm_ext = _with_ones(vmt_ref[0])
        for g, rows in enumerate(heads):
            for cols in _strips(tq):
                s_t = jnp.dot(km_ref[0], qt_ref[rows, cols], preferred_element_type=F32) + pad
                _strip_update(s_t, 0.0, vm_ext, m_scr, l_scr, acc_scr,
                              slice(g * tq + cols.start, g * tq + cols.stop))

    k = k_ref[...]
    v_ext = _with_ones(vt_ref[...])
    work = [(g, rows, cols) for g, rows in enumerate(heads) for cols in _strips(tq)]
    scr_cols = [slice(g * tq + cols.start, g * tq + cols.stop) for g, _, cols in work]
    _run_strips([functools.partial(lambda rows, cols: jnp.dot(k, qt_ref[rows, cols], preferred_element_type=F32),
                                   rows, cols) for _, rows, cols in work],
                scr_cols, 0.0, v_ext, m_scr, l_scr, acc_scr)

    @pl.when(ki == pl.num_programs(3) - 1)
    def _():
        for g, rows in enumerate(heads):
            cols = slice(g * tq, (g + 1) * tq)
            o_ref[:, rows] = (acc_scr[:, cols] / l_scr[:, cols]).T


def _gqa_attention(qt, k, vt, km, vmt, buf, n_seq, n_q, n_k, q_col0, k_row0, tq, tk):
    nq, nk = n_q // tq, n_k // tk
    qb0, kb0 = q_col0 // tq, k_row0 // tk
    r = B_GROUP * tq
    gw = B_GROUP * B_DH
    return pl.pallas_call(
        _gqa_kernel,
        grid=(n_seq, B_KV_HEADS, nq, nk),
        in_specs=[pl.BlockSpec((gw, tq), lambda s, h, i, j: (h, qb0 + s * nq + i)),
                  pl.BlockSpec((tk, B_DH), lambda s, h, i, j: (kb0 + s * nk + j, h)),
                  pl.BlockSpec((B_DH, tk), lambda s, h, i, j: (h, kb0 + s * nk + j)),
                  pl.BlockSpec((1, LANES, B_DH), lambda s, h, i, j: (s, 0, h)),
                  pl.BlockSpec((1, B_DH, LANES), lambda s, h, i, j: (s, h, 0)),
                  pl.BlockSpec(memory_space=pl.ANY)],
        out_specs=pl.BlockSpec((tq, gw), lambda s, h, i, j: (qb0 + s * nq + i, h)),
        out_shape=jax.ShapeDtypeStruct(buf.shape, F32),
        scratch_shapes=[pltpu.VMEM((1, r), F32), pltpu.VMEM((1, r), F32), pltpu.VMEM((B_DH, r), F32)],
        input_output_aliases={5: 0},
        compiler_params=_cparams(("parallel", "parallel", "parallel", "arbitrary")),
        name="gqa_attention",
    )(qt, k, vt, km, vmt, buf)


def _diff_kernel(qt_ref, k_ref, vt_ref, km_ref, vmt_ref, tt_ref, bm_ref, cst_ref, buf_ref, o_ref,
                 m_scr, l_scr, acc_scr, *, meta_q):
    del buf_ref
    qi, ki = pl.program_id(2), pl.program_id(3)
    tq, tk = qt_ref.shape[1], k_ref.shape[0]
    nt = tk // LANES
    qt = qt_ref[...]
    dim = lax.broadcasted_iota(jnp.int32, qt.shape, 0)
    zero = jnp.zeros_like(qt)
    maps = (jnp.where(dim < C_DQK, qt, zero), jnp.where(dim >= C_DQK, qt, zero))
    c_left = cst_ref[0, 0:1, 0:1]
    c_right = cst_ref[0, 1:2, 0:1]
    lam = cst_ref[0, 2:3, 0:1]
    strips = [(mp, cols) for mp in range(2) for cols in _strips(tq)]
    scr = lambda mp, cols: slice(mp * tq + cols.start, mp * tq + cols.stop)

    @pl.when(ki == 0)
    def _():
        _attn_init(m_scr, l_scr, acc_scr)
        vm_ext = _with_ones(vmt_ref[0])
        for mp, cols in strips:
            s_t = jnp.dot(km_ref[0], maps[mp][:, cols], preferred_element_type=F32) + bm_ref[0, 0, :, cols]
            _strip_update(s_t, 0.0, vm_ext, m_scr, l_scr, acc_scr, scr(mp, cols))

    k = k_ref[...]
    v_ext = _with_ones(vt_ref[...])

    def tile_bias(off):
        if off <= -2:
            return jnp.broadcast_to(c_left, (LANES, LANES))
        if off >= 2:
            return jnp.broadcast_to(c_right, (LANES, LANES))
        return tt_ref[0, off + 1]

    def strip_bias(dd, cols):
        q_tiles = range(cols.start // LANES, cols.stop // LANES)
        if meta_q:
            rows = [[tt_ref[0, 0]]] + [[tile_bias(2)] for _ in range(nt - 1)]
        else:
            rows = [[tile_bias(dd * nt + b - a) for a in q_tiles] for b in range(nt)]
        return jnp.concatenate([jnp.concatenate(r, axis=1) for r in rows], axis=0)

    def run(c, bias):
        def scores(mp, cols):
            s_t = jnp.dot(k, maps[mp][:, cols], preferred_element_type=F32)
            return s_t if bias is None else s_t + bias(cols)

        _run_strips([functools.partial(scores, mp, cols) for mp, cols in strips],
                    [scr(mp, cols) for mp, cols in strips], c, v_ext, m_scr, l_scr, acc_scr)

    ratio = max(tq // tk, 1)
    d = ki - qi * ratio
    specials = (0,) if meta_q else tuple(range(-1, ratio + 1))
    for dd in specials:
        @pl.when(d == dd)
        def _(dd=dd):
            run(0.0, functools.partial(strip_bias, dd))

    @pl.when((d < specials[0]) | (d > specials[-1]))
    def _():
        run(jnp.where(d < 0, c_left, c_right), None)

    @pl.when(ki == pl.num_programs(3) - 1)
    def _():
        o1 = acc_scr[:, :tq] / l_scr[:, :tq]
        o2 = acc_scr[:, tq:] / l_scr[:, tq:]
        o_ref[...] = (o1 - lam * o2).T


def _diff_attention(qt, k, vt, km, vmt, tt, bm, cst, buf, n_seq, n_q, n_k, q_col0, k_row0, tq, tk, meta_q):
    nq, nk = n_q // tq, n_k // tk
    qb0, kb0 = q_col0 // tq, k_row0 // tk
    r = 2 * tq
    return pl.pallas_call(
        functools.partial(_diff_kernel, meta_q=meta_q),
        grid=(n_seq, C_HEADS, nq, nk),
        in_specs=[pl.BlockSpec((LANES, tq), lambda s, h, i, j: (h, qb0 + s * nq + i)),
                  pl.BlockSpec((tk, LANES), lambda s, h, i, j: (kb0 + s * nk + j, h)),
                  pl.BlockSpec((C_DV, tk), lambda s, h, i, j: (h, kb0 + s * nk + j)),
                  pl.BlockSpec((1, LANES, LANES), lambda s, h, i, j: (s, 0, h)),
                  pl.BlockSpec((1, C_DV, LANES), lambda s, h, i, j: (s, h, 0)),
                  pl.BlockSpec((1, 3, LANES, LANES), lambda s, h, i, j: (h, 0, 0, 0)),
                  pl.BlockSpec((1, 1, LANES, tq), lambda s, h, i, j: (h, jnp.minimum(i, 1), 0, 0)),
                  pl.BlockSpec((1, 8, LANES), lambda s, h, i, j: (h, 0, 0)),
                  pl.BlockSpec(memory_space=pl.ANY)],
        out_specs=pl.BlockSpec((tq, C_DV), lambda s, h, i, j: (qb0 + s * nq + i, h)),
        out_shape=jax.ShapeDtypeStruct(buf.shape, F32),
        scratch_shapes=[pltpu.VMEM((1, r), F32), pltpu.VMEM((1, r), F32), pltpu.VMEM((C_DV, r), F32)],
        input_output_aliases={8: 0},
        compiler_params=_cparams(("parallel", "parallel", "parallel", "arbitrary")),
        name="diff_attention",
    )(qt, k, vt, km, vmt, tt, bm, cst, buf)


def _tri_masks(rev):
    i = lax.broadcasted_iota(jnp.int32, (CHUNK, CHUNK), 0)
    j = lax.broadcasted_iota(jnp.int32, (CHUNK, CHUNK), 1)
    if rev:
        return j >= i, j > i, i, j
    return j <= i, j < i, i, j


def _cum_mat_t(rev):
    t = lax.broadcasted_iota(jnp.int32, (CHUNK, CHUNK), 0)
    j = lax.broadcasted_iota(jnp.int32, (CHUNK, CHUNK), 1)
    return jnp.where((t >= j) if rev else (t <= j), 1.0, 0.0)


def _gdn_kernel(qf_ref, kf_ref, vf_ref, cf_ref, rf_ref, qb_ref, kb_ref, vb_ref, cb_ref, rb_ref,
                s0f_ref, s0b_ref, bf_buf, bb_buf, of_ref, ob_ref, sff_ref, sfb_ref, s_scr):
    del bf_buf, bb_buf
    c = pl.program_id(2)
    nv = 2 * A_HG

    @pl.when(c == 0)
    def _():
        s_scr[0] = s0f_ref[0]
        s_scr[1] = s0b_ref[0]

    streams = ((qf_ref, kf_ref, vf_ref, cf_ref, rf_ref, False), (qb_ref, kb_ref, vb_ref, cb_ref, rb_ref, True))
    units = [(d, e) for d in range(2) for e in range(nv)]
    eye = jnp.where(_tri_masks(False)[0] & _tri_masks(True)[0], 1.0, 0.0)
    q, k, kk, qk, gc_all, gr_all, col, masks = [], [], [], [], [], [], [], []
    for q_ref, k_ref, _, c_ref, r_ref, rev in streams:
        tri, strict, _, _ = _tri_masks(rev)
        masks.append((tri, strict))
        qd = [q_ref[:, j * A_DK:(j + 1) * A_DK] for j in range(A_HG)]
        kd = [k_ref[:, j * A_DK:(j + 1) * A_DK] for j in range(A_HG)]
        q.append(qd)
        k.append(kd)
        kk.append([_dot_nt(x, x) for x in kd])
        qk.append([_dot_nt(x, y) for x, y in zip(qd, kd)])
        cv = c_ref[...]
        col.append(cv)
        gc_all.append(_dot_exact_lhs(jnp.where(tri, 1.0, 0.0), cv))
        gr_all.append(_dot_exact_rhs(r_ref[0, 0], _cum_mat_t(rev)))
    gcb = [jnp.broadcast_to(gc_all[d][:, e:e + 1], (CHUNK, LANES)) for d, e in units]
    beta = [jnp.broadcast_to(col[d][:, nv + e:nv + e + 1], (CHUNK, LANES)) for d, e in units]
    decay = [jnp.exp(jnp.where(masks[d][0], gcb[u][:, :CHUNK] - gr_all[d][e:e + 1, :], -jnp.inf))
             for u, (d, e) in enumerate(units)]
    a = [jnp.where(masks[d][1], beta[u][:, :CHUNK] * kk[d][e // 2] * decay[u], 0.0)
         for u, (d, e) in enumerate(units)]
    x = [eye - au for au in a]
    p = a
    for _ in range(5):
        p = [_dot(pu, pu) for pu in p]
        x = [xu + _dot(xu, pu) for xu, pu in zip(x, p)]
    egc = [jnp.exp(g) for g in gcb]
    v = [streams[d][2][:, e * A_DV:(e + 1) * A_DV] for d, e in units]
    uw = [_dot(x[u], jnp.concatenate([v[u] * beta[u], k[d][e // 2] * (beta[u] * egc[u])], axis=1))
          for u, (d, e) in enumerate(units)]
    s = [s_scr[d, e] for d, e in units]
    r2 = [_dot(jnp.concatenate([uw[u][:, A_DV:], q[d][e // 2] * egc[u]], axis=0), s[u])
          for u, (d, e) in enumerate(units)]
    v_new = [uw[u][:, :A_DV] - r2[u][:CHUNK] for u in range(len(units))]
    o = [r2[u][CHUNK:] + _dot(qk[d][e // 2] * decay[u], v_new[u]) for u, (d, e) in enumerate(units)]
    gtot = [gcb[u][0:1, :] if d else gcb[u][CHUNK - 1:CHUNK, :] for u, (d, e) in enumerate(units)]
    s_new = [s[u] * jnp.exp(gtot[u]) + _dot_tn(k[d][e // 2] * jnp.exp(gtot[u] - gcb[u]), v_new[u])
             for u, (d, e) in enumerate(units)]
    for u, (d, e) in enumerate(units):
        (ob_ref if d else of_ref)[:, e * A_DV:(e + 1) * A_DV] = o[u]
        s_scr[d, e] = s_new[u]

    @pl.when(c == pl.num_programs(2) - 1)
    def _():
        sff_ref[0] = s_scr[0]
        sfb_ref[0] = s_scr[1]


def _gdn_scan(qkv, colf, colb, rowf, rowb, s0f, s0b, buf_f, buf_b, n_seq, n_tok, row0):
    nc = n_tok // CHUNK
    cb0 = row0 // CHUNK
    ng = A_QK_HEADS // A_HG
    nv = 2 * A_HG
    fwd = lambda s, j, c: cb0 + s * nc + c
    bwd = lambda s, j, c: cb0 + s * nc + (nc - 1 - c)
    k_blk, v_blk = A_QK // (A_HG * A_DK), 2 * A_QK // (nv * A_DV)

    def stream_specs(ix):
        return [pl.BlockSpec((CHUNK, A_HG * A_DK), lambda s, j, c: (ix(s, j, c), j)),
                pl.BlockSpec((CHUNK, A_HG * A_DK), lambda s, j, c: (ix(s, j, c), k_blk + j)),
                pl.BlockSpec((CHUNK, nv * A_DV), lambda s, j, c: (ix(s, j, c), v_blk + j)),
                pl.BlockSpec((CHUNK, LANES), lambda s, j, c: (ix(s, j, c), j)),
                pl.BlockSpec((1, 1, nv, CHUNK), lambda s, j, c: (ix(s, j, c), j, 0, 0))]

    st_spec = pl.BlockSpec((1, nv, A_DK, A_DV), lambda s, j, c: (s, j, 0, 0))
    st_shape = jax.ShapeDtypeStruct((n_seq, A_V_HEADS, A_DK, A_DV), F32)
    any_spec = pl.BlockSpec(memory_space=pl.ANY)
    return pl.pallas_call(
        _gdn_kernel,
        grid=(n_seq, ng, nc),
        in_specs=stream_specs(fwd) + stream_specs(bwd) + [st_spec, st_spec, any_spec, any_spec],
        out_specs=[pl.BlockSpec((CHUNK, nv * A_DV), lambda s, j, c: (fwd(s, j, c), j)),
                   pl.BlockSpec((CHUNK, nv * A_DV), lambda s, j, c: (bwd(s, j, c), j)),
                   st_spec, st_spec],
        out_shape=[jax.ShapeDtypeStruct(buf_f.shape, F32), jax.ShapeDtypeStruct(buf_b.shape, F32),
                   st_shape, st_shape],
        scratch_shapes=[pltpu.VMEM((2, nv, A_DK, A_DV), F32)],
        input_output_aliases={12: 0, 13: 1},
        compiler_params=_cparams(("parallel", "parallel", "arbitrary")),
        name="gdn_scan",
    )(qkv, qkv, qkv, colf, rowf, qkv, qkv, qkv, colb, rowb, s0f, s0b, buf_f, buf_b)


GLA_LEVELS = (5, 4, 3, 2, 1, 0)
GLA_SAFE_EXP = 60.0


def _gla_level(i, j, lv, rev):
    sup = lv + 1
    base = jnp.left_shift(jnp.right_shift(i, sup), sup)
    ref_row = base + (1 << lv) - (0 if rev else 1)
    same = jnp.right_shift(i, sup) == jnp.right_shift(j, sup)
    i_late = jnp.bitwise_and(jnp.right_shift(i, lv), 1) == (0 if rev else 1)
    j_early = jnp.bitwise_and(jnp.right_shift(j, lv), 1) == (1 if rev else 0)
    return jnp.where(j == ref_row, 1.0, 0.0), same & i_late & j_early


def _rows_bcast(x, rows, span):
    zero = jnp.zeros((1, x.shape[1]), x.dtype)
    picked = [x[r:r + 1] if 0 <= r < x.shape[0] else zero for r in rows]
    return jnp.concatenate([jnp.broadcast_to(r, (span, x.shape[1])) for r in picked], axis=0)


def _gla_block16(i, j, rev):
    blk = jnp.right_shift(i, 4)
    ref_row = jnp.left_shift(blk + 1, 4) if rev else jnp.left_shift(blk, 4) - 1
    tri = (j >= i) if rev else (j <= i)
    return jnp.where(j == ref_row, 1.0, 0.0), tri & (blk == jnp.right_shift(j, 4))


def _gla_pair(q, k, bc, ref, key_side_nonpositive):
    ek = ref - bc
    if key_side_nonpositive:
        ek = jnp.minimum(ek, 0.0)
    return (q * jnp.exp(jnp.minimum(bc - ref, 0.0))).astype(BF16), (k * jnp.exp(ek)).astype(BF16)


def _gla_kernel(qf_ref, kf_ref, vf_ref, lf_ref, qb_ref, kb_ref, vb_ref, lb_ref, w2_ref, gb_ref,
                s0f_ref, s0b_ref, bf_buf, bb_buf, of_ref, ob_ref, sff_ref, sfb_ref, s_scr, a_scr):
    del bf_buf, bb_buf
    c = pl.program_id(1)

    @pl.when(c == 0)
    def _():
        s_scr[0] = s0f_ref[0]
        s_scr[1] = s0b_ref[0]

    streams = ((qf_ref, kf_ref, vf_ref, lf_ref, False), (qb_ref, kb_ref, vb_ref, lb_ref, True))
    heads = [slice(hd * D_DK, (hd + 1) * D_DK) for hd in range(D_HEADS)]
    units = [(d, hd) for d in range(2) for hd in range(D_HEADS)]
    ops = []
    for d, (q_ref, k_ref, v_ref, l_ref, rev) in enumerate(streams):
        tri, _, i, j = _tri_masks(rev)
        x = _dot(l_ref[...], w2_ref[d]) + gb_ref[d]
        lg = (jnp.minimum(x, 0.0) - jnp.log(1.0 + jnp.exp(-jnp.abs(x)))) * (1.0 / GATE_TAU)
        bc = _dot_exact_lhs(jnp.where(tri, 1.0, 0.0), lg)
        btot = bc[0:1] if rev else bc[CHUNK - 1:CHUNK]
        q, k = q_ref[...] * D_DK ** -0.5, k_ref[...]
        def level(lv, q=q, k=k, bc=bc, i=i, j=j, rev=rev):
            pick, mask = _gla_level(i, j, lv, rev)
            if (1 << lv) >= 8:
                span = 2 << lv
                ref = _rows_bcast(bc, [b + (1 << lv) - (0 if rev else 1) for b in range(0, CHUNK, span)], span)
            else:
                ref = _dot_exact_lhs(pick, bc)
            return (mask,) + _gla_pair(q, k, bc, ref, True)

        _, mask16 = _gla_block16(i, j, rev)
        ref16 = _rows_bcast(bc, [b + 16 if rev else b - 1 for b in range(0, CHUNK, 16)], 16)
        ops.append(dict(level=level, coarse=[level(lv) for lv in GLA_LEVELS[:2]],
                        block16=(mask16,) + _gla_pair(q, k, bc, ref16, False), risk=jnp.max(ref16 - bc),
                        diag=(i == j, q.astype(BF16), k.astype(BF16)),
                        qg=(q * jnp.exp(bc)).astype(BF16), kd=(k * jnp.exp(btot - bc)).astype(BF16),
                        etot=jnp.exp(btot), v=v_ref[...].astype(BF16)))

    def term(pair, hd):
        mask, qe, ke = pair
        return jnp.where(mask, _dot_nt(qe[:, heads[hd]], ke[:, heads[hd]]), 0.0)

    a_coarse = [term(ops[d]['coarse'][0], hd) + term(ops[d]['coarse'][1], hd) for d, hd in units]
    risk = jnp.maximum(ops[0]['risk'], ops[1]['risk'])

    @pl.when(risk <= GLA_SAFE_EXP)
    def _():
        for u, (d, hd) in enumerate(units):
            a_scr[u] = a_coarse[u] + term(ops[d]['block16'], hd)

    @pl.when(risk > GLA_SAFE_EXP)
    def _():
        fine = [[ops[d]['diag']] + [ops[d]['level'](lv) for lv in GLA_LEVELS[2:]] for d in range(2)]
        for u, (d, hd) in enumerate(units):
            a_scr[u] = a_coarse[u] + functools.reduce(lambda x, y: x + y, [term(pr, hd) for pr in fine[d]])

    a = [a_scr[u] for u in range(len(units))]
    st = [s_scr[d, hd] for d, hd in units]
    vs = [ops[d]['v'][:, hd * D_DV:(hd + 1) * D_DV] for d, hd in units]
    o = [_dot_nt(ops[d]['qg'][:, heads[hd]], st[u]) + _dot(a[u], vs[u]) for u, (d, hd) in enumerate(units)]
    st_new = [st[u] * ops[d]['etot'][:, heads[hd]] + _dot_tn(vs[u], ops[d]['kd'][:, heads[hd]])
              for u, (d, hd) in enumerate(units)]
    for u, (d, hd) in enumerate(units):
        (ob_ref if d else of_ref)[:, hd * D_DV:(hd + 1) * D_DV] = o[u]
        s_scr[d, hd] = st_new[u]

    @pl.when(c == pl.num_programs(1) - 1)
    def _():
        sff_ref[0] = s_scr[0]
        sfb_ref[0] = s_scr[1]


def _gla_scan(p, lr, w2, gb, s0f, s0b, buf_f, buf_b, n_seq, n_tok, row0):
    nc = n_tok // CHUNK
    cb0 = row0 // CHUNK
    fwd = lambda s, c: cb0 + s * nc + c
    bwd = lambda s, c: cb0 + s * nc + (nc - 1 - c)

    def stream_specs(ix):
        return [pl.BlockSpec((CHUNK, D_QKW), lambda s, c: (ix(s, c), 0)),
                pl.BlockSpec((CHUNK, D_QKW), lambda s, c: (ix(s, c), 1)),
                pl.BlockSpec((CHUNK, D_VW), lambda s, c: (ix(s, c), 2 * D_QKW // D_VW)),
                pl.BlockSpec((CHUNK, 2 * GATE_RANK), lambda s, c: (ix(s, c), 0))]

    st_spec = pl.BlockSpec((1, D_HEADS, D_DV, D_DK), lambda s, c: (s, 0, 0, 0))
    st_shape = jax.ShapeDtypeStruct((n_seq, D_HEADS, D_DV, D_DK), F32)
    any_spec = pl.BlockSpec(memory_space=pl.ANY)
    return pl.pallas_call(
        _gla_kernel,
        grid=(n_seq, nc),
        in_specs=stream_specs(fwd) + stream_specs(bwd) + [
            pl.BlockSpec((2, 2 * GATE_RANK, D_QKW), lambda s, c: (0, 0, 0)),
            pl.BlockSpec((2, 1, D_QKW), lambda s, c: (0, 0, 0)),
            st_spec, st_spec, any_spec, any_spec],
        out_specs=[pl.BlockSpec((CHUNK, D_VW), lambda s, c: (fwd(s, c), 0)),
                   pl.BlockSpec((CHUNK, D_VW), lambda s, c: (bwd(s, c), 0)),
                   st_spec, st_spec],
        out_shape=[jax.ShapeDtypeStruct(buf_f.shape, F32), jax.ShapeDtypeStruct(buf_b.shape, F32),
                   st_shape, st_shape],
        scratch_shapes=[pltpu.VMEM((2, D_HEADS, D_DV, D_DK), F32), pltpu.VMEM((2 * D_HEADS, CHUNK, CHUNK), F32)],
        input_output_aliases={12: 0, 13: 1},
        compiler_params=_cparams(("parallel", "arbitrary")),
        name="gla_scan",
    )(p, p, p, lr, p, p, p, lr, w2, gb, s0f, s0b, buf_f, buf_b)


def _gdn_prep_kernel(x_ref, prev_ref, next_ref, edge_ref, w_ref, o_ref, *, starts, ends):
    i, j = pl.program_id(0), pl.program_id(1)
    tm, cw = x_ref.shape
    r0 = i * tm
    is_start = functools.reduce(jnp.logical_or, [r0 == s for s in starts])
    is_end = functools.reduce(jnp.logical_or, [r0 + tm == e for e in ends])
    x = x_ref[...]
    prev_row = jnp.where(is_start, edge_ref[0, 7:8, :], prev_ref[7:8, :])
    next_row = jnp.where(is_end, 0.0, next_ref[0:1, :])
    rows = lax.broadcasted_iota(jnp.int32, (tm, 1), 0)
    xp = jnp.where(rows == 0, prev_row, pltpu.roll(x, 1, axis=0))
    xn = jnp.where(rows == tm - 1, next_row, pltpu.roll(x, tm - 1, axis=0))
    y = xp * w_ref[0:1, :] + x * w_ref[1:2, :] + xn * w_ref[2:3, :]
    y = y * _sigmoid(y)
    is_qk = j < 2 * A_QK // cw
    q_scale = jnp.where(j < A_QK // cw, A_DK ** -0.5, 1.0)
    for c0 in range(0, cw, A_DK):
        yh = y[:, c0:c0 + A_DK]
        nrm = yh * (lax.rsqrt(jnp.sum(yh * yh, axis=-1, keepdims=True) + EPS) * q_scale)
        o_ref[:, c0:c0 + A_DK] = jnp.where(is_qk, nrm, yh)


def _gdn_prep(lay, x, edge, conv_w):
    r, c = x.shape
    tm, cw = 512, 512
    starts = [row + s * n for b, n, row, _ in lay.groups for s in range(b)]
    ends = [row + (s + 1) * n for b, n, row, _ in lay.groups for s in range(b)]
    seq_of = lambda i: sum((i * tm >= s).astype(jnp.int32) for s in starts) - 1
    w8 = jnp.pad(conv_w.astype(F32), ((0, 8 - conv_w.shape[0]), (0, 0)))
    return pl.pallas_call(
        functools.partial(_gdn_prep_kernel, starts=starts, ends=ends),
        grid=(r // tm, c // cw),
        in_specs=[pl.BlockSpec((tm, cw), lambda i, j: (i, j)),
                  pl.BlockSpec((8, cw), lambda i, j: (jnp.maximum(i * (tm // 8) - 1, 0), j)),
                  pl.BlockSpec((8, cw), lambda i, j: (jnp.minimum((i + 1) * (tm // 8), r // 8 - 1), j)),
                  pl.BlockSpec((1, 8, cw), lambda i, j: (seq_of(i), 0, j)),
                  pl.BlockSpec((8, cw), lambda i, j: (0, j))],
        out_specs=pl.BlockSpec((tm, cw), lambda i, j: (i, j)),
        out_shape=jax.ShapeDtypeStruct((r, c), F32),
        compiler_params=_cparams(("parallel", "parallel")),
        name="gdn_prep",
    )(x, x, x, edge, w8)


def _gqa_prep_kernel(q_ref, kv_ref, cos_ref, sin_ref, qg_ref, kg_ref, qt_ref, k_ref, vt_ref):
    cos, sin = cos_ref[...], sin_ref[...]

    def norm_rope(x, g):
        xn = x * lax.rsqrt(jnp.mean(x * x, axis=-1, keepdims=True) + EPS) * g
        return xn * cos + pltpu.roll(xn, B_DH // 2, axis=1) * sin

    for hd in range(B_HEADS):
        sl = slice(hd * B_DH, (hd + 1) * B_DH)
        qt_ref[sl, :] = norm_rope(q_ref[:, sl], qg_ref[...]).T.astype(BF16)
    for hd in range(B_KV_HEADS):
        sl = slice(hd * B_DH, (hd + 1) * B_DH)
        k_ref[:, sl] = norm_rope(kv_ref[:, sl], kg_ref[...]).astype(BF16)
        vt_ref[sl, :] = kv_ref[:, B_KVW + hd * B_DH:B_KVW + (hd + 1) * B_DH].T.astype(BF16)


def _pos_block(lay, tm):
    def pos(i):
        r = i * tm
        out = 0
        for b, n, row, _ in lay.groups:
            inside = (r >= row) & (r < row + b * n)
            out = out + jnp.where(inside, ((r - row) % n) // tm, 0)
        return out
    return pos


def _gqa_prep(lay, pq, pkv, cos, sin, qg, kg):
    r = pq.shape[0]
    tm = min(512, min(n for _, n, _, _ in lay.groups))
    pos = _pos_block(lay, tm)
    row = lambda i: (i, 0)
    col = lambda i: (0, i)
    fixed = lambda i: (0, 0)
    return pl.pallas_call(
        _gqa_prep_kernel,
        grid=(r // tm,),
        in_specs=[pl.BlockSpec((tm, B_QW), row), pl.BlockSpec((tm, 2 * B_KVW), row),
                  pl.BlockSpec((tm, B_DH), lambda i: (pos(i), 0)), pl.BlockSpec((tm, B_DH), lambda i: (pos(i), 0)),
                  pl.BlockSpec((1, B_DH), fixed), pl.BlockSpec((1, B_DH), fixed)],
        out_specs=[pl.BlockSpec((B_QW, tm), col), pl.BlockSpec((tm, B_KVW), row), pl.BlockSpec((B_KVW, tm), col)],
        out_shape=[jax.ShapeDtypeStruct((B_QW, r), BF16), jax.ShapeDtypeStruct((r, B_KVW), BF16),
                   jax.ShapeDtypeStruct((B_KVW, r), BF16)],
        compiler_params=_cparams(("parallel",)),
        name="gqa_prep",
    )(pq, pkv, cos, sin, qg.reshape(1, B_DH), kg.reshape(1, B_DH))


def _diff_prep_kernel(q_ref, k_ref, v_ref, qg_ref, kg_ref, qt_ref, ko_ref, vt_ref):
    lo = lax.broadcasted_iota(jnp.int32, (1, LANES), 1) < C_DQK

    def norm_maps(x, g):
        sq = x * x
        s_lo = jnp.sum(jnp.where(lo, sq, 0.0), axis=-1, keepdims=True)
        s_hi = jnp.sum(jnp.where(lo, 0.0, sq), axis=-1, keepdims=True)
        inv = jnp.where(lo, lax.rsqrt(s_lo * (1.0 / C_DQK) + EPS), lax.rsqrt(s_hi * (1.0 / C_DQK) + EPS))
        return x * inv * g

    qt_ref[...] = norm_maps(q_ref[...], qg_ref[...]).T.astype(BF16)
    ko_ref[...] = norm_maps(k_ref[...], kg_ref[...]).astype(BF16)
    vt_ref[...] = v_ref[...].T.astype(BF16)


def _diff_prep(p, qg, kg):
    r = p.shape[0]
    tm = 512 if r % 512 == 0 else LANES
    fixed = lambda i, h: (0, 0)
    return pl.pallas_call(
        _diff_prep_kernel,
        grid=(r // tm, C_HEADS),
        in_specs=[pl.BlockSpec((tm, LANES), lambda i, h: (i, h)),
                  pl.BlockSpec((tm, LANES), lambda i, h: (i, C_HEADS + h)),
                  pl.BlockSpec((tm, LANES), lambda i, h: (i, 2 * C_HEADS + h)),
                  pl.BlockSpec((1, LANES), fixed), pl.BlockSpec((1, LANES), fixed)],
        out_specs=[pl.BlockSpec((LANES, tm), lambda i, h: (h, i)), pl.BlockSpec((tm, LANES), lambda i, h: (i, h)),
                   pl.BlockSpec((LANES, tm), lambda i, h: (h, i))],
        out_shape=[jax.ShapeDtypeStruct((C_QKW, r), BF16), jax.ShapeDtypeStruct((r, C_QKW), BF16),
                   jax.ShapeDtypeStruct((C_VW, r), BF16)],
        compiler_params=_cparams(("parallel", "parallel")),
        name="diff_prep",
    )(p, p, p, qg.reshape(1, LANES), kg.reshape(1, LANES))


class _Layout:
    def __init__(self, shapes):
        self.groups = []
        row, seq = 0, 0
        for b, n in shapes:
            self.groups.append((b, n, row, seq))
            row += b * n
            seq += b
        self.rows, self.n_seq = row, seq

    def to_seqs(self, main, meta):
        out = []
        c = main.shape[-1]
        for b, n, row, seq in self.groups:
            m = meta[seq * N_META:(seq + b) * N_META].reshape(b, N_META, c)
            x = main[row:row + b * n].reshape(b, n, c)
            out.append(jnp.concatenate([m, x], axis=1))
        return out

    def from_seqs(self, seqs):
        main = jnp.concatenate([s[:, N_META:].reshape(-1, s.shape[-1]) for s in seqs], axis=0)
        meta = jnp.concatenate([s[:, :N_META].reshape(-1, s.shape[-1]) for s in seqs], axis=0)
        return main, meta


def _meta_chunk(meta):
    c = meta.shape[-1]
    m = meta.reshape(-1, N_META, c)
    return jnp.pad(m, ((0, 0), (CHUNK - N_META, 0), (0, 0))).reshape(-1, c)


def _meta_unchunk(x):
    c = x.shape[-1]
    return x.reshape(-1, CHUNK, c)[:, CHUNK - N_META:].reshape(-1, c)


def _map_seqs(lay, fn, main, meta):
    return lay.from_seqs([fn(s) for s in lay.to_seqs(main, meta)])


def _run_scan(scan, lay, main_args, meta_args, state_shape, width):
    s_all = lay.n_seq
    zeros = jnp.zeros((s_all,) + state_shape, F32)
    mrows = s_all * CHUNK
    mbuf = jnp.zeros((mrows, width), F32)
    _, _, s_meta, _ = scan(*meta_args, zeros, zeros, mbuf, mbuf, s_all, CHUNK, 0)
    of = jnp.zeros((lay.rows, width), F32)
    ob = jnp.zeros((lay.rows, width), F32)
    finals = []
    for b, n, row, seq in lay.groups:
        of, ob, _, sfb = scan(*main_args, s_meta[seq:seq + b], zeros[seq:seq + b], of, ob, b, n, row)
        finals.append(sfb)
    mof, mob, _, _ = scan(*meta_args, zeros, jnp.concatenate(finals, axis=0), mbuf, mbuf, s_all, CHUNK, 0)
    return (of, ob), (_meta_unchunk(mof), _meta_unchunk(mob))


def _gdn_layer(lay, h, hm, norm_g, w_in, conv_w, a_log, dt_bias, out_g, w_out):
    w_qkv, w_z, w_ab = (w_in[:, :A_CONV_CH], w_in[:, A_CONV_CH:A_CONV_CH + A_VW], w_in[:, A_CONV_CH + A_VW:])
    x, xm = _norm_proj(h, norm_g, w_qkv), _norm_proj(hm, norm_g, w_qkv)
    z, zm = _norm_proj(h, norm_g, w_z), _norm_proj(hm, norm_g, w_z)
    ab, abm = _norm_proj(h, norm_g, w_ab), _norm_proj(hm, norm_g, w_ab)
    s_all = lay.n_seq
    xm3 = xm.reshape(s_all, N_META, A_CONV_CH)
    qkv = _gdn_prep(lay, x, xm3[:, N_META - 8:], conv_w)

    def prep(x):
        prev = jnp.pad(x[:, :-1], ((0, 0), (1, 0), (0, 0)))
        nxt = jnp.pad(x[:, 1:], ((0, 0), (0, 1), (0, 0)))
        y = jax.nn.silu(prev * conv_w[0] + x * conv_w[1] + nxt * conv_w[2])
        b, l, _ = y.shape
        qk = y[..., :2 * A_QK].reshape(b, l, 2 * A_QK_HEADS, A_DK)
        qk = qk * lax.rsqrt(jnp.sum(qk * qk, axis=-1, keepdims=True) + EPS)
        qk = qk.reshape(b, l, 2, A_QK) * jnp.array([A_DK ** -0.5, 1.0], F32)[:, None]
        return jnp.concatenate([qk.reshape(b, l, 2 * A_QK), y[..., 2 * A_QK:]], axis=-1)

    first = x[np.array([row + s * n for b, n, row, _ in lay.groups for s in range(b)])]
    qkvm = prep(jnp.concatenate([xm3, first[:, None]], axis=1))[:, :N_META].reshape(-1, A_CONV_CH)

    def gates(x):
        x = x.reshape(-1, 2, 2, A_V_HEADS)
        g = -jnp.exp(a_log.astype(F32)) * jax.nn.softplus(x[:, 0] + dt_bias.astype(F32))
        return g, jax.nn.sigmoid(x[:, 1])

    ng, nv = A_QK_HEADS // A_HG, 2 * A_HG

    def pack(g, beta, d):
        r = g.shape[0]
        gd, bd = g[:, d].reshape(r, ng, nv), beta[:, d].reshape(r, ng, nv)
        col = jnp.concatenate([gd, bd, jnp.zeros((r, ng, LANES - 2 * nv), F32)], axis=-1).reshape(r, ng * LANES)
        row = gd.reshape(r // CHUNK, CHUNK, ng, nv).transpose(0, 2, 3, 1)
        return col, row

    def scan_args(qkv_rows, g, beta):
        colf, rowf = pack(g, beta, 0)
        colb, rowb = pack(g, beta, 1)
        return (qkv_rows, colf, colb, rowf, rowb)

    g, beta = gates(ab)
    gm, betam = gates(abm)
    pad_gate = lambda t: _meta_chunk(t.reshape(-1, 2 * A_V_HEADS)).reshape(-1, 2, A_V_HEADS)
    (of, ob), (mof, mob) = _run_scan(
        _gdn_scan, lay, scan_args(qkv, g, beta), scan_args(_meta_chunk(qkvm), pad_gate(gm), pad_gate(betam)),
        (A_V_HEADS, A_DK, A_DV), A_VW)
    gain = jnp.tile(out_g.astype(F32), A_V_HEADS)
    return (_out_proj([of, ob], z, gain, w_out, h, A_DV), _out_proj([mof, mob], zm, gain, w_out, hm, A_DV))


def _rope_tables(n_tok):
    rows = n_tok // GRID_W
    row = jnp.repeat(jnp.arange(rows, dtype=F32), GRID_W)
    col = (jnp.arange(rows * GRID_W) % GRID_W).astype(F32)
    inv = ROPE_THETA ** (-jnp.arange(0, ROPE_AXIS, 2, dtype=F32) / ROPE_AXIS)
    ang = jnp.concatenate([row[:, None] * inv, col[:, None] * inv], axis=-1)
    return jnp.cos(ang), jnp.sin(ang)


def _pad_meta(x, n_seq):
    return jnp.pad(x.reshape(n_seq, N_META, -1), ((0, 0), (0, LANES - N_META), (0, 0)))


def _gqa_layer(lay, h, hm, norm_g, w_in, q_g, k_g, w_out):
    kv_end = B_QW + 2 * B_KVW
    pq, pkv, z = (_norm_proj(h, norm_g, w_in[:, :B_QW]), _norm_proj(h, norm_g, w_in[:, B_QW:kv_end]),
                  _norm_proj(h, norm_g, w_in[:, kv_end:]))
    pm = _norm_proj(hm, norm_g, w_in)

    def norm_heads(x, g):
        r = x.shape[0]
        x = x.reshape(r, -1, B_DH)
        return x * lax.rsqrt(jnp.mean(x * x, axis=-1, keepdims=True) + EPS) * g.astype(F32)

    scale = B_DH ** -0.5 * LOG2E
    cos, sin = _rope_tables(max(n for _, n, _, _ in lay.groups))
    qt, k, vt = _gqa_prep(lay, pq, pkv, jnp.concatenate([cos, cos], axis=-1), jnp.concatenate([-sin, sin], axis=-1),
                          q_g.astype(F32) * scale, k_g.astype(F32))
    s_all = lay.n_seq
    qm = (norm_heads(pm[:, :B_QW], q_g) * scale).reshape(-1, B_QW).astype(BF16)
    km = norm_heads(pm[:, B_QW:B_QW + B_KVW], k_g).reshape(-1, B_KVW).astype(BF16)
    vm = pm[:, B_QW + B_KVW:B_QW + 2 * B_KVW].astype(BF16)
    zm = pm[:, B_QW + 2 * B_KVW:]
    qmt = _pad_meta(qm, s_all).reshape(s_all * LANES, B_QW).T
    kmp, vmt = _pad_meta(km, s_all), _pad_meta(vm, s_all).transpose(0, 2, 1)

    o = jnp.zeros((lay.rows, B_QW), F32)
    om = jnp.zeros((s_all * LANES, B_QW), F32)
    for b, n, row, seq in lay.groups:
        tq, tk = min(GQA_TQ, n), min(GQA_TK, n)
        o = _gqa_attention(qt, k, vt, kmp[seq:seq + b], vmt[seq:seq + b], o, b, n, n, row, row, tq, tk)
        om = _gqa_attention(qmt, k, vt, kmp[seq:seq + b], vmt[seq:seq + b], om, b, LANES, n,
                            seq * LANES, row, LANES, tk)
    om = om.reshape(s_all, LANES, B_QW)[:, :N_META].reshape(-1, B_QW)
    ones = jnp.ones((B_QW,), F32)
    return _out_proj([o], z, ones, w_out, h, 0), _out_proj([om], zm, ones, w_out, hm, 0)


def _t5_bucket_np(rel):
    nb = REL_BUCKETS // 2
    max_exact = nb // 2
    n = np.abs(rel)
    steps = nb - max_exact
    large = np.full(n.shape, max_exact, np.int64)
    for kstep in range(1, steps + 1):
        large += (n.astype(np.float64) ** steps >= (max_exact ** steps) * float(REL_MAX_DIST // max_exact) ** kstep)
    large = np.minimum(large, nb - 1)
    return np.where(rel > 0, nb, 0) + np.where(n < max_exact, n, large)


def _diff_layer(lay, h, hm, norm_g, w_in, q_g, k_g, lam, sub_g, w_out, rel_bias, layer_idx):
    n_qkv = 2 * C_QKW + C_VW
    p, z = _norm_proj(h, norm_g, w_in[:, :n_qkv]), _norm_proj(h, norm_g, w_in[:, n_qkv:])
    pm = _norm_proj(hm, norm_g, w_in)

    def norm_maps(x, g):
        r = x.shape[0]
        x = x.reshape(r, -1, C_DQK)
        return (x * lax.rsqrt(jnp.mean(x * x, axis=-1, keepdims=True) + EPS) * g.astype(F32)).reshape(r, -1)

    s_all = lay.n_seq
    scale = C_DQK ** -0.5 * LOG2E
    qt, k, vt = _diff_prep(p, jnp.tile(q_g.astype(F32), 2) * scale, jnp.tile(k_g.astype(F32), 2))
    qm = (norm_maps(pm[:, :C_QKW], q_g) * scale).astype(BF16)
    km = norm_maps(pm[:, C_QKW:2 * C_QKW], k_g).astype(BF16)
    vm = pm[:, 2 * C_QKW:n_qkv].astype(BF16)
    zm = pm[:, n_qkv:]
    qmt = _pad_meta(qm, s_all).reshape(s_all * LANES, C_QKW).T
    kmp, vmt = _pad_meta(km, s_all), _pad_meta(vm, s_all).transpose(0, 2, 1)

    lam = lam.astype(F32)
    lam_init = 0.8 - 0.6 * math.exp(-0.3 * layer_idx)
    lam_full = jnp.exp(jnp.sum(lam[0] * lam[1])) - jnp.exp(jnp.sum(lam[2] * lam[3])) + lam_init
    table = rel_bias.astype(F32).T * LOG2E
    look = lambda rel: table[:, _t5_bucket_np(rel)]
    nb = REL_BUCKETS // 2
    cst = jnp.zeros((C_HEADS, 8, LANES), F32)
    cst = cst.at[:, 0].set(table[:, nb - 1:nb]).at[:, 1].set(table[:, 2 * nb - 1:2 * nb]).at[:, 2].set(lam_full)
    ar = np.arange(LANES)
    key_pad = jnp.asarray(np.where(ar < N_META, 0.0, NEG), F32)[:, None]
    tt = jnp.stack([look(o * LANES + ar[:, None] - ar[None, :]) for o in (-1, 0, 1)], axis=1)
    mq = np.minimum(ar, N_META - 1)
    tt_m = jnp.stack([look(N_META + ar[:, None] - mq[None, :])] * 3, axis=1)
    meta_rows = (ar < N_META)[:, None]
    bm_m = (jnp.where(meta_rows, look(ar[:, None] - mq[None, :]), 0.0) + key_pad)[:, None]

    o = jnp.zeros((lay.rows, C_VW), F32)
    om = jnp.zeros((s_all * LANES, C_VW), F32)
    for b, n, row, seq in lay.groups:
        t, tk = min(DIFF_TQ, n), min(DIFF_TK, n)
        pq = np.arange(t)
        near = look(ar[:N_META, None] - N_META - pq[None, :])
        far = jnp.broadcast_to(table[:, nb - 1][:, None, None], near.shape)
        pad_rows = jnp.full((C_HEADS, 2, LANES - N_META, t), NEG, F32)
        bm = jnp.concatenate([jnp.stack([near, far], axis=1), pad_rows], axis=2)
        o = _diff_attention(qt, k, vt, kmp[seq:seq + b], vmt[seq:seq + b], tt, bm, cst, o,
                            b, n, n, row, row, t, tk, False)
        om = _diff_attention(qmt, k, vt, kmp[seq:seq + b], vmt[seq:seq + b], tt_m, bm_m, cst, om,
                             b, LANES, n, seq * LANES, row, LANES, tk, True)
    om = om.reshape(s_all, LANES, C_VW)[:, :N_META].reshape(-1, C_VW)
    gain = jnp.tile(sub_g.astype(F32), C_HEADS) * (1.0 - lam_init)
    return _out_proj([o], z, gain, w_out, h, C_DV), _out_proj([om], zm, gain, w_out, hm, C_DV)


def _gla_layer(lay, h, hm, norm_g, w_in, gate_w2, gate_b, out_g, w_out):
    n_qkv = 2 * D_QKW + D_VW
    w_qkv, w_z, w_lr = w_in[:, :n_qkv], w_in[:, n_qkv:n_qkv + D_VW], w_in[:, n_qkv + D_VW:]
    p, pm = _norm_proj(h, norm_g, w_qkv), _norm_proj(hm, norm_g, w_qkv)
    z, zm = _norm_proj(h, norm_g, w_z), _norm_proj(hm, norm_g, w_z)
    lr, lrm = _norm_proj(h, norm_g, w_lr), _norm_proj(hm, norm_g, w_lr)
    zero = jnp.zeros((GATE_RANK, D_QKW), F32)
    w2 = jnp.stack([jnp.concatenate([gate_w2[0].astype(F32), zero], axis=0),
                    jnp.concatenate([zero, gate_w2[1].astype(F32)], axis=0)])
    gb = gate_b.astype(F32).reshape(2, 1, D_QKW)
    (of, ob), (mof, mob) = _run_scan(
        _gla_scan, lay, (p, lr, w2, gb), (_meta_chunk(pm), _meta_chunk(lrm), w2, gb), (D_HEADS, D_DV, D_DK), D_VW)
    gain = jnp.tile(out_g.astype(F32), D_HEADS)
    return _out_proj([of, ob], z, gain, w_out, h, D_DV), _out_proj([mof, mob], zm, gain, w_out, hm, D_DV)


def _trunk(xs, meta_tokens, rel_bias, a, b, c, d, depth):
    lay = _Layout([(x.shape[0], x.shape[1]) for x in xs])
    h = jnp.concatenate([x.reshape(-1, D_MODEL) for x in xs], axis=0)
    hm = jnp.tile(meta_tokens.astype(F32), (lay.n_seq, 1))
    for i in range(depth):
        m, j = i % 4, i // 4
        if m == 0:
            h, hm = _gdn_layer(lay, h, hm, *(t[j] for t in a))
        elif m == 1:
            h, hm = _gqa_layer(lay, h, hm, *(t[j] for t in b))
        elif m == 2:
            h, hm = _diff_layer(lay, h, hm, *(t[j] for t in c), rel_bias, i)
        else:
            h, hm = _gla_layer(lay, h, hm, *(t[j] for t in d))
    return [h[row:row + bsz * n].reshape(bsz, n, D_MODEL) for bsz, n, row, _ in lay.groups]


def kernel(x_prompt, x_sample, meta_tokens, rel_bias, a_norm, a_w_in, a_conv, a_a_log, a_dt_bias, a_out_norm, a_w_out, b_norm, b_w_in, b_q_norm, b_k_norm, b_w_out, c_norm, c_w_in, c_q_norm, c_k_norm, c_lambda, c_sub_norm, c_w_out, d_norm, d_w_in, d_gate_w2, d_gate_b, d_out_norm, d_w_out):
    y_prompt, y_sample = _trunk(
        [x_prompt, x_sample], meta_tokens, rel_bias,
        (a_norm, a_w_in, a_conv, a_a_log, a_dt_bias, a_out_norm, a_w_out),
        (b_norm, b_w_in, b_q_norm, b_k_norm, b_w_out),
        (c_norm, c_w_in, c_q_norm, c_k_norm, c_lambda, c_sub_norm, c_w_out),
        (d_norm, d_w_in, d_gate_w2, d_gate_b, d_out_norm, d_w_out), 4)
    return (y_prompt, y_sample)
```

```python
import functools
import math

import numpy as np
import jax
import jax.numpy as jnp
from jax import lax
from jax.experimental import pallas as pl
from jax.experimental.pallas import tpu as pltpu

F32 = jnp.float32
BF16 = jnp.bfloat16

D_MODEL = 1024
N_META = 16
CHUNK = 64
GRID_W = 64
EPS = 1e-6
LANES = 128
VMEM_LIMIT = 56 * 1024 * 1024
NEG = -1e30
LOG2E = math.log2(math.e)
STRIP = 1024
ONES_ROWS = 16
PROJ_TN_MAX = 2048
GQA_TQ, GQA_TK = 1024, 2048
DIFF_TQ, DIFF_TK = 2048, 2048
BIAS_TILES = 5

A_QK_HEADS, A_V_HEADS, A_DK, A_DV = 8, 16, 128, 128
A_QK, A_VW = A_QK_HEADS * A_DK, A_V_HEADS * A_DV
A_CONV_CH = 2 * A_QK + A_VW
A_HG = 8
B_HEADS, B_KV_HEADS, B_DH = 8, 2, 128
B_GROUP = B_HEADS // B_KV_HEADS
B_QW, B_KVW = B_HEADS * B_DH, B_KV_HEADS * B_DH
ROPE_AXIS = B_DH // 2
ROPE_THETA = 10000.0
C_HEADS, C_DQK, C_DV = 8, 64, 128
C_QKW, C_VW = C_HEADS * 2 * C_DQK, C_HEADS * C_DV
REL_BUCKETS, REL_MAX_DIST = 32, 128
D_HEADS, D_DK, D_DV = 4, 128, 256
D_QKW, D_VW = D_HEADS * D_DK, D_HEADS * D_DV
GATE_RANK, GATE_TAU = 16, 16.0


def _cparams(sem):
    return pltpu.CompilerParams(dimension_semantics=sem, vmem_limit_bytes=VMEM_LIMIT)


def _sigmoid(x):
    return 1.0 / (1.0 + jnp.exp(-x))


def _dot(a, b):
    return jnp.dot(a.astype(BF16), b.astype(BF16), preferred_element_type=F32)


def _dot_nt(a, b):
    return lax.dot_general(a.astype(BF16), b.astype(BF16), (((1,), (1,)), ((), ())),
                           preferred_element_type=F32)


def _dot_tn(a, b):
    return lax.dot_general(a.astype(BF16), b.astype(BF16), (((0,), (0,)), ((), ())),
                           preferred_element_type=F32)


def _split3(x):
    x1 = x.astype(BF16)
    r1 = x - x1.astype(F32)
    x2 = r1.astype(BF16)
    x3 = (r1 - x2.astype(F32)).astype(BF16)
    return x1, x2, x3


def _dot_exact_lhs(m01, x):
    m = m01.astype(BF16)
    x1, x2, x3 = _split3(x)
    return (jnp.dot(m, x1, preferred_element_type=F32) + jnp.dot(m, x2, preferred_element_type=F32)
            + jnp.dot(m, x3, preferred_element_type=F32))


def _dot_exact_rhs(x, m01):
    m = m01.astype(BF16)
    x1, x2, x3 = _split3(x)
    return (jnp.dot(x1, m, preferred_element_type=F32) + jnp.dot(x2, m, preferred_element_type=F32)
            + jnp.dot(x3, m, preferred_element_type=F32))


def _norm_proj_kernel(x_ref, g_ref, w_ref, o_ref):
    x = x_ref[...]
    ms = jnp.mean(x * x, axis=-1, keepdims=True)
    xn = (x * lax.rsqrt(ms + EPS) * g_ref[...]).astype(BF16)
    o_ref[...] = jnp.dot(xn, w_ref[...], preferred_element_type=F32)


def _norm_proj(x, g, w):
    m, k = x.shape
    n = w.shape[1]
    tm = 512 if m % 512 == 0 else m
    tn = max([t for t in range(256, PROJ_TN_MAX + 1, 256) if n % t == 0], default=n)
    return pl.pallas_call(
        _norm_proj_kernel,
        grid=(n // tn, m // tm),
        in_specs=[pl.BlockSpec((tm, k), lambda j, i: (i, 0)),
                  pl.BlockSpec((1, k), lambda j, i: (0, 0)),
                  pl.BlockSpec((k, tn), lambda j, i: (0, j))],
        out_specs=pl.BlockSpec((tm, tn), lambda j, i: (i, j)),
        out_shape=jax.ShapeDtypeStruct((m, n), F32),
        compiler_params=_cparams(("parallel", "parallel")),
        name="norm_proj",
    )(x, g.reshape(1, k).astype(F32), w.astype(BF16))


def _out_proj_kernel(*refs, n_o, hw):
    o_refs = refs[:n_o]
    z_ref, g_ref, w_ref, h_ref, out_ref = refs[n_o:]
    k = z_ref.shape[1]
    cw = hw if hw else 256
    acc = h_ref[...]
    for c0 in range(0, k, cw):
        o = o_refs[0][:, c0:c0 + cw]
        for r in o_refs[1:]:
            o = o + r[:, c0:c0 + cw]
        if hw:
            ms = jnp.mean(o * o, axis=-1, keepdims=True)
            o = o * lax.rsqrt(ms + EPS) * g_ref[:, c0:c0 + cw]
        z = z_ref[:, c0:c0 + cw]
        gated = o * (z * _sigmoid(z))
        acc = acc + jnp.dot(gated.astype(BF16), w_ref[c0:c0 + cw, :], preferred_element_type=F32)
    out_ref[...] = acc


def _out_proj(os, z, g, w, h, hw):
    m, k = z.shape
    n = w.shape[1]
    tm = (512 if k <= 1024 else 256) if m % 512 == 0 else m
    row = lambda i: (i, 0)
    fixed = lambda i: (0, 0)
    return pl.pallas_call(
        functools.partial(_out_proj_kernel, n_o=len(os), hw=hw),
        grid=(m // tm,),
        in_specs=[pl.BlockSpec((tm, k), row) for _ in os] + [
            pl.BlockSpec((tm, k), row), pl.BlockSpec((1, k), fixed),
            pl.BlockSpec((k, n), fixed), pl.BlockSpec((tm, n), row)],
        out_specs=pl.BlockSpec((tm, n), row),
        out_shape=jax.ShapeDtypeStruct((m, n), F32),
        compiler_params=_cparams(("parallel",)),
        name="out_proj",
    )(*os, z, g.reshape(1, k).astype(F32), w.astype(BF16), h)


def _strip_update(s_t, c, v_ext, m_scr, l_scr, acc_scr, cols):
    p, alpha = _strip_softmax(s_t, c, m_scr, cols)
    _strip_pv(p, alpha, v_ext, l_scr, acc_scr, cols)


def _strip_softmax(s_t, c, m_scr, cols):
    m_prev = m_scr[:, cols]
    m_new = jnp.maximum(m_prev, jnp.max(s_t, axis=0, keepdims=True) + c)
    m_scr[:, cols] = m_new
    return jnp.exp2(s_t - (m_new - c)).astype(BF16), jnp.exp2(m_prev - m_new)


def _strip_pv(p, alpha, v_ext, l_scr, acc_scr, cols):
    dv = acc_scr.shape[0]
    pv = jnp.dot(v_ext, p, preferred_element_type=F32)
    acc_scr[:, cols] = alpha * acc_scr[:, cols] + pv[:dv]
    l_scr[:, cols] = alpha * l_scr[:, cols] + pv[dv:dv + 1]


def _run_strips(score_fns, cols_list, c, v_ext, m_scr, l_scr, acc_scr):
    s_next = score_fns[0]()
    pending = None
    for n, cols in enumerate(cols_list):
        s_t, s_next = s_next, (score_fns[n + 1]() if n + 1 < len(score_fns) else None)
        if pending is not None:
            _strip_pv(pending[0], pending[1], v_ext, l_scr, acc_scr, pending[2])
        p, alpha = _strip_softmax(s_t, c, m_scr, cols)
        pending = (p, alpha, cols)
    _strip_pv(pending[0], pending[1], v_ext, l_scr, acc_scr, pending[2])


def _attn_init(m_scr, l_scr, acc_scr):
    m_scr[...] = jnp.full(m_scr.shape, NEG, F32)
    l_scr[...] = jnp.zeros(l_scr.shape, F32)
    acc_scr[...] = jnp.zeros(acc_scr.shape, F32)


def _with_ones(v_t):
    return jnp.concatenate([v_t, jnp.ones((ONES_ROWS, v_t.shape[1]), BF16)], axis=0)


def _strips(width):
    w = min(STRIP, width)
    return [slice(c0, c0 + w) for c0 in range(0, width, w)]


def _gqa_kernel(qt_ref, k_ref, vt_ref, km_ref, vmt_ref, buf_ref, o_ref, m_scr, l_scr, acc_scr):
    del buf_ref
    ki = pl.program_id(3)
    tq = qt_ref.shape[1]
    heads = [slice(g * B_DH, (g + 1) * B_DH) for g in range(B_GROUP)]

    @pl.when(ki == 0)
    def _():
        _attn_init(m_scr, l_scr, acc_scr)
        key = lax.broadcasted_iota(jnp.int32, (LANES, 1), 0)
        pad = jnp.where(key < N_META, 0.0, NEG)
        vm_ext = _with_ones(vmt_ref[0])
        for g, rows in enumerate(heads):
            for cols in _strips(tq):
                s_t = jnp.dot(km_ref[0], qt_ref[rows, cols], preferred_element_type=F32) + pad
                _strip_update(s_t, 0.0, vm_ext, m_scr, l_scr, acc_scr,
                              slice(g * tq + cols.start, g * tq + cols.stop))

    k = k_ref[...]
    v_ext = _with_ones(vt_ref[...])
    work = [(g, rows, cols) for g, rows in enumerate(heads) for cols in _strips(tq)]
    scr_cols = [slice(g * tq + cols.start, g * tq + cols.stop) for g, _, cols in work]
    _run_strips([functools.partial(lambda rows, cols: jnp.dot(k, qt_ref[rows, cols], preferred_element_type=F32),
                                   rows, cols) for _, rows, cols in work],
                scr_cols, 0.0, v_ext, m_scr, l_scr, acc_scr)

    @pl.when(ki == pl.num_programs(3) - 1)
    def _():
        for g, rows in enumerate(heads):
            cols = slice(g * tq, (g + 1) * tq)
            o_ref[:, rows] = (acc_scr[:, cols] / l_scr[:, cols]).T


def _gqa_attention(qt, k, vt, km, vmt, buf, n_seq, n_q, n_k, q_col0, k_row0, tq, tk):
    nq, nk = n_q // tq, n_k // tk
    qb0, kb0 = q_col0 // tq, k_row0 // tk
    r = B_GROUP * tq
    gw = B_GROUP * B_DH
    return pl.pallas_call(
        _gqa_kernel,
        grid=(n_seq, B_KV_HEADS, nq, nk),
        in_specs=[pl.BlockSpec((gw, tq), lambda s, h, i, j: (h, qb0 + s * nq + i)),
                  pl.BlockSpec((tk, B_DH), lambda s, h, i, j: (kb0 + s * nk + j, h)),
                  pl.BlockSpec((B_DH, tk), lambda s, h, i, j: (h, kb0 + s * nk + j)),
                  pl.BlockSpec((1, LANES, B_DH), lambda s, h, i, j: (s, 0, h)),
                  pl.BlockSpec((1, B_DH, LANES), lambda s, h, i, j: (s, h, 0)),
                  pl.BlockSpec(memory_space=pl.ANY)],
        out_specs=pl.BlockSpec((tq, gw), lambda s, h, i, j: (qb0 + s * nq + i, h)),
        out_shape=jax.ShapeDtypeStruct(buf.shape, F32),
        scratch_shapes=[pltpu.VMEM((1, r), F32), pltpu.VMEM((1, r), F32), pltpu.VMEM((B_DH, r), F32)],
        input_output_aliases={5: 0},
        compiler_params=_cparams(("parallel", "parallel", "parallel", "arbitrary")),
        name="gqa_attention",
    )(qt, k, vt, km, vmt, buf)


def _diff_kernel(qt_ref, k_ref, vt_ref, km_ref, vmt_ref, tt_ref, bm_ref, cst_ref, buf_ref, o_ref,
                 m_scr, l_scr, acc_scr, *, meta_q):
    del buf_ref
    qi, ki = pl.program_id(2), pl.program_id(3)
    tq, tk = qt_ref.shape[1], k_ref.shape[0]
    nt = tk // LANES
    qt = qt_ref[...]
    dim = lax.broadcasted_iota(jnp.int32, qt.shape, 0)
    zero = jnp.zeros_like(qt)
    maps = (jnp.where(dim < C_DQK, qt, zero), jnp.where(dim >= C_DQK, qt, zero))
    c_left = cst_ref[0, 0:1, 0:1]
    c_right = cst_ref[0, 1:2, 0:1]
    lam = cst_ref[0, 2:3, 0:1]
    strips = [(mp, cols) for mp in range(2) for cols in _strips(tq)]
    scr = lambda mp, cols: slice(mp * tq + cols.start, mp * tq + cols.stop)

    @pl.when(ki == 0)
    def _():
        _attn_init(m_scr, l_scr, acc_scr)
        vm_ext = _with_ones(vmt_ref[0])
        for mp, cols in strips:
            s_t = jnp.dot(km_ref[0], maps[mp][:, cols], preferred_element_type=F32) + bm_ref[0, 0, :, cols]
            _strip_update(s_t, 0.0, vm_ext, m_scr, l_scr, acc_scr, scr(mp, cols))

    k = k_ref[...]
    v_ext = _with_ones(vt_ref[...])

    ratio = max(tq // tk, 1)
    d = ki - qi * ratio
    near = (0, 0) if meta_q else (-1, ratio)
    mid = BIAS_TILES // 2

    def strip_bias(cols):
        q_tiles = range(cols.start // LANES, cols.stop // LANES)
        if meta_q:
            rows = [[tt_ref[0, mid]]] + [[tt_ref[0, BIAS_TILES - 1]] for _ in range(nt - 1)]
        else:
            rows = [[tt_ref[0, jnp.clip(d * nt + (b - a), -mid, mid) + mid] for a in q_tiles] for b in range(nt)]
        return jnp.concatenate([jnp.concatenate(r, axis=1) for r in rows], axis=0)

    def run(c, bias):
        def scores(mp, cols):
            s_t = jnp.dot(k, maps[mp][:, cols], preferred_element_type=F32)
            return s_t if bias is None else s_t + bias(cols)

        _run_strips([functools.partial(scores, mp, cols) for mp, cols in strips],
                    [scr(mp, cols) for mp, cols in strips], c, v_ext, m_scr, l_scr, acc_scr)

    is_near = (d >= near[0]) & (d <= near[1])

    @pl.when(is_near)
    def _():
        run(0.0, strip_bias)

    @pl.when(jnp.logical_not(is_near))
    def _():
        run(jnp.where(d < 0, c_left, c_right), None)

    @pl.when(ki == pl.num_programs(3) - 1)
    def _():
        o1 = acc_scr[:, :tq] / l_scr[:, :tq]
        o2 = acc_scr[:, tq:] / l_scr[:, tq:]
        o_ref[...] = (o1 - lam * o2).T


def _diff_attention(qt, k, vt, km, vmt, tt, bm, cst, buf, n_seq, n_q, n_k, q_col0, k_row0, tq, tk, meta_q):
    nq, nk = n_q // tq, n_k // tk
    qb0, kb0 = q_col0 // tq, k_row0 // tk
    r = 2 * tq
    return pl.pallas_call(
        functools.partial(_diff_kernel, meta_q=meta_q),
        grid=(n_seq, C_HEADS, nq, nk),
        in_specs=[pl.BlockSpec((LANES, tq), lambda s, h, i, j: (h, qb0 + s * nq + i)),
                  pl.BlockSpec((tk, LANES), lambda s, h, i, j: (kb0 + s * nk + j, h)),
                  pl.BlockSpec((C_DV, tk), lambda s, h, i, j: (h, kb0 + s * nk + j)),
                  pl.BlockSpec((1, LANES, LANES), lambda s, h, i, j: (s, 0, h)),
                  pl.BlockSpec((1, C_DV, LANES), lambda s, h, i, j: (s, h, 0)),
                  pl.BlockSpec((1, BIAS_TILES, LANES, LANES), lambda s, h, i, j: (h, 0, 0, 0)),
                  pl.BlockSpec((1, 1, LANES, tq), lambda s, h, i, j: (h, jnp.minimum(i, 1), 0, 0)),
                  pl.BlockSpec((1, 8, LANES), lambda s, h, i, j: (h, 0, 0)),
                  pl.BlockSpec(memory_space=pl.ANY)],
        out_specs=pl.BlockSpec((tq, C_DV), lambda s, h, i, j: (qb0 + s * nq + i, h)),
        out_shape=jax.ShapeDtypeStruct(buf.shape, F32),
        scratch_shapes=[pltpu.VMEM((1, r), F32), pltpu.VMEM((1, r), F32), pltpu.VMEM((C_DV, r), F32)],
        input_output_aliases={8: 0},
        compiler_params=_cparams(("parallel", "parallel", "parallel", "arbitrary")),
        name="diff_attention",
    )(qt, k, vt, km, vmt, tt, bm, cst, buf)


def _tri_masks(rev):
    i = lax.broadcasted_iota(jnp.int32, (CHUNK, CHUNK), 0)
    j = lax.broadcasted_iota(jnp.int32, (CHUNK, CHUNK), 1)
    if rev:
        return j >= i, j > i, i, j
    return j <= i, j < i, i, j


def _cum_mat_t(rev):
    t = lax.broadcasted_iota(jnp.int32, (CHUNK, CHUNK), 0)
    j = lax.broadcasted_iota(jnp.int32, (CHUNK, CHUNK), 1)
    return jnp.where((t >= j) if rev else (t <= j), 1.0, 0.0)


def _gdn_kernel(qf_ref, kf_ref, vf_ref, cf_ref, rf_ref, qb_ref, kb_ref, vb_ref, cb_ref, rb_ref,
                s0f_ref, s0b_ref, bf_buf, bb_buf, of_ref, ob_ref, sff_ref, sfb_ref, s_scr):
    del bf_buf, bb_buf
    c = pl.program_id(2)
    nv = 2 * A_HG

    @pl.when(c == 0)
    def _():
        s_scr[0] = s0f_ref[0]
        s_scr[1] = s0b_ref[0]

    streams = ((qf_ref, kf_ref, vf_ref, cf_ref, rf_ref, False), (qb_ref, kb_ref, vb_ref, cb_ref, rb_ref, True))
    units = [(d, e) for d in range(2) for e in range(nv)]
    eye = jnp.where(_tri_masks(False)[0] & _tri_masks(True)[0], 1.0, 0.0)
    q, k, kk, qk, gc_all, gr_all, col, masks = [], [], [], [], [], [], [], []
    for q_ref, k_ref, _, c_ref, r_ref, rev in streams:
        tri, strict, _, _ = _tri_masks(rev)
        masks.append((tri, strict))
        qd = [q_ref[:, j * A_DK:(j + 1) * A_DK] for j in range(A_HG)]
        kd = [k_ref[:, j * A_DK:(j + 1) * A_DK] for j in range(A_HG)]
        q.append(qd)
        k.append(kd)
        kk.append([_dot_nt(x, x) for x in kd])
        qk.append([_dot_nt(x, y) for x, y in zip(qd, kd)])
        cv = c_ref[...]
        col.append(cv)
        gc_all.append(_dot_exact_lhs(jnp.where(tri, 1.0, 0.0), cv))
        gr_all.append(_dot_exact_rhs(r_ref[0, 0], _cum_mat_t(rev)))
    gcb = [jnp.broadcast_to(gc_all[d][:, e:e + 1], (CHUNK, LANES)) for d, e in units]
    beta = [jnp.broadcast_to(col[d][:, nv + e:nv + e + 1], (CHUNK, LANES)) for d, e in units]
    decay = [jnp.exp(jnp.where(masks[d][0], gcb[u][:, :CHUNK] - gr_all[d][e:e + 1, :], -jnp.inf))
             for u, (d, e) in enumerate(units)]
    a = [jnp.where(masks[d][1], beta[u][:, :CHUNK] * kk[d][e // 2] * decay[u], 0.0)
         for u, (d, e) in enumerate(units)]
    x = [eye - au for au in a]
    p = a
    for _ in range(5):
        p = [_dot(pu, pu) for pu in p]
        x = [xu + _dot(xu, pu) for xu, pu in zip(x, p)]
    egc = [jnp.exp(g) for g in gcb]
    v = [streams[d][2][:, e * A_DV:(e + 1) * A_DV] for d, e in units]
    uw = [_dot(x[u], jnp.concatenate([v[u] * beta[u], k[d][e // 2] * (beta[u] * egc[u])], axis=1))
          for u, (d, e) in enumerate(units)]
    s = [s_scr[d, e] for d, e in units]
    r2 = [_dot(jnp.concatenate([uw[u][:, A_DV:], q[d][e // 2] * egc[u]], axis=0), s[u])
          for u, (d, e) in enumerate(units)]
    v_new = [uw[u][:, :A_DV] - r2[u][:CHUNK] for u in range(len(units))]
    o = [r2[u][CHUNK:] + _dot(qk[d][e // 2] * decay[u], v_new[u]) for u, (d, e) in enumerate(units)]
    gtot = [gcb[u][0:1, :] if d else gcb[u][CHUNK - 1:CHUNK, :] for u, (d, e) in enumerate(units)]
    s_new = [s[u] * jnp.exp(gtot[u]) + _dot_tn(k[d][e // 2] * jnp.exp(gtot[u] - gcb[u]), v_new[u])
             for u, (d, e) in enumerate(units)]
    for u, (d, e) in enumerate(units):
        (ob_ref if d else of_ref)[:, e * A_DV:(e + 1) * A_DV] = o[u]
        s_scr[d, e] = s_new[u]

    @pl.when(c == pl.num_programs(2) - 1)
    def _():
        sff_ref[0] = s_scr[0]
        sfb_ref[0] = s_scr[1]


def _gdn_scan(qkv, colf, colb, rowf, rowb, s0f, s0b, buf_f, buf_b, n_seq, n_tok, row0):
    nc = n_tok // CHUNK
    cb0 = row0 // CHUNK
    ng = A_QK_HEADS // A_HG
    nv = 2 * A_HG
    fwd = lambda s, j, c: cb0 + s * nc + c
    bwd = lambda s, j, c: cb0 + s * nc + (nc - 1 - c)
    k_blk, v_blk = A_QK // (A_HG * A_DK), 2 * A_QK // (nv * A_DV)

    def stream_specs(ix):
        return [pl.BlockSpec((CHUNK, A_HG * A_DK), lambda s, j, c: (ix(s, j, c), j)),
                pl.BlockSpec((CHUNK, A_HG * A_DK), lambda s, j, c: (ix(s, j, c), k_blk + j)),
                pl.BlockSpec((CHUNK, nv * A_DV), lambda s, j, c: (ix(s, j, c), v_blk + j)),
                pl.BlockSpec((CHUNK, LANES), lambda s, j, c: (ix(s, j, c), j)),
                pl.BlockSpec((1, 1, nv, CHUNK), lambda s, j, c: (ix(s, j, c), j, 0, 0))]

    st_spec = pl.BlockSpec((1, nv, A_DK, A_DV), lambda s, j, c: (s, j, 0, 0))
    st_shape = jax.ShapeDtypeStruct((n_seq, A_V_HEADS, A_DK, A_DV), F32)
    any_spec = pl.BlockSpec(memory_space=pl.ANY)
    return pl.pallas_call(
        _gdn_kernel,
        grid=(n_seq, ng, nc),
        in_specs=stream_specs(fwd) + stream_specs(bwd) + [st_spec, st_spec, any_spec, any_spec],
        out_specs=[pl.BlockSpec((CHUNK, nv * A_DV), lambda s, j, c: (fwd(s, j, c), j)),
                   pl.BlockSpec((CHUNK, nv * A_DV), lambda s, j, c: (bwd(s, j, c), j)),
                   st_spec, st_spec],
        out_shape=[jax.ShapeDtypeStruct(buf_f.shape, F32), jax.ShapeDtypeStruct(buf_b.shape, F32),
                   st_shape, st_shape],
        scratch_shapes=[pltpu.VMEM((2, nv, A_DK, A_DV), F32)],
        input_output_aliases={12: 0, 13: 1},
        compiler_params=_cparams(("parallel", "parallel", "arbitrary")),
        name="gdn_scan",
    )(qkv, qkv, qkv, colf, rowf, qkv, qkv, qkv, colb, rowb, s0f, s0b, buf_f, buf_b)


GLA_LEVELS = (5, 4, 3, 2, 1, 0)
GLA_SAFE_EXP = 60.0


def _gla_level(i, j, lv, rev):
    sup = lv + 1
    base = jnp.left_shift(jnp.right_shift(i, sup), sup)
    ref_row = base + (1 << lv) - (0 if rev else 1)
    same = jnp.right_shift(i, sup) == jnp.right_shift(j, sup)
    i_late = jnp.bitwise_and(jnp.right_shift(i, lv), 1) == (0 if rev else 1)
    j_early = jnp.bitwise_and(jnp.right_shift(j, lv), 1) == (1 if rev else 0)
    return jnp.where(j == ref_row, 1.0, 0.0), same & i_late & j_early


def _rows_bcast(x, rows, span):
    zero = jnp.zeros((1, x.shape[1]), x.dtype)
    picked = [x[r:r + 1] if 0 <= r < x.shape[0] else zero for r in rows]
    return jnp.concatenate([jnp.broadcast_to(r, (span, x.shape[1])) for r in picked], axis=0)


def _gla_block16(i, j, rev):
    blk = jnp.right_shift(i, 4)
    ref_row = jnp.left_shift(blk + 1, 4) if rev else jnp.left_shift(blk, 4) - 1
    tri = (j >= i) if rev else (j <= i)
    return jnp.where(j == ref_row, 1.0, 0.0), tri & (blk == jnp.right_shift(j, 4))


def _gla_pair(q, k, bc, ref, key_side_nonpositive):
    ek = ref - bc
    if key_side_nonpositive:
        ek = jnp.minimum(ek, 0.0)
    return (q * jnp.exp(jnp.minimum(bc - ref, 0.0))).astype(BF16), (k * jnp.exp(ek)).astype(BF16)


def _gla_kernel(qf_ref, kf_ref, vf_ref, lf_ref, qb_ref, kb_ref, vb_ref, lb_ref, w2_ref, gb_ref,
                s0f_ref, s0b_ref, bf_buf, bb_buf, of_ref, ob_ref, sff_ref, sfb_ref, s_scr, a_scr):
    del bf_buf, bb_buf
    c = pl.program_id(1)

    @pl.when(c == 0)
    def _():
        s_scr[0] = s0f_ref[0]
        s_scr[1] = s0b_ref[0]

    streams = ((qf_ref, kf_ref, vf_ref, lf_ref, False), (qb_ref, kb_ref, vb_ref, lb_ref, True))
    heads = [slice(hd * D_DK, (hd + 1) * D_DK) for hd in range(D_HEADS)]
    units = [(d, hd) for d in range(2) for hd in range(D_HEADS)]
    ops = []
    for d, (q_ref, k_ref, v_ref, l_ref, rev) in enumerate(streams):
        tri, _, i, j = _tri_masks(rev)
        x = _dot(l_ref[...], w2_ref[d]) + gb_ref[d]
        lg = (jnp.minimum(x, 0.0) - jnp.log(1.0 + jnp.exp(-jnp.abs(x)))) * (1.0 / GATE_TAU)
        bc = _dot_exact_lhs(jnp.where(tri, 1.0, 0.0), lg)
        btot = bc[0:1] if rev else bc[CHUNK - 1:CHUNK]
        q, k = q_ref[...] * D_DK ** -0.5, k_ref[...]
        def level(lv, q=q, k=k, bc=bc, i=i, j=j, rev=rev):
            pick, mask = _gla_level(i, j, lv, rev)
            if (1 << lv) >= 8:
                span = 2 << lv
                ref = _rows_bcast(bc, [b + (1 << lv) - (0 if rev else 1) for b in range(0, CHUNK, span)], span)
            else:
                ref = _dot_exact_lhs(pick, bc)
            return (mask,) + _gla_pair(q, k, bc, ref, True)

        _, mask16 = _gla_block16(i, j, rev)
        ref16 = _rows_bcast(bc, [b + 16 if rev else b - 1 for b in range(0, CHUNK, 16)], 16)
        ops.append(dict(level=level, coarse=[level(lv) for lv in GLA_LEVELS[:2]],
                        block16=(mask16,) + _gla_pair(q, k, bc, ref16, False), risk=jnp.max(ref16 - bc),
                        diag=(i == j, q.astype(BF16), k.astype(BF16)),
                        qg=(q * jnp.exp(bc)).astype(BF16), kd=(k * jnp.exp(btot - bc)).astype(BF16),
                        etot=jnp.exp(btot), v=v_ref[...].astype(BF16)))

    def term(pair, hd):
        mask, qe, ke = pair
        return jnp.where(mask, _dot_nt(qe[:, heads[hd]], ke[:, heads[hd]]), 0.0)

    a_coarse = [term(ops[d]['coarse'][0], hd) + term(ops[d]['coarse'][1], hd) for d, hd in units]
    risk = jnp.maximum(ops[0]['risk'], ops[1]['risk'])

    @pl.when(risk <= GLA_SAFE_EXP)
    def _():
        for u, (d, hd) in enumerate(units):
            a_scr[u] = a_coarse[u] + term(ops[d]['block16'], hd)

    @pl.when(risk > GLA_SAFE_EXP)
    def _():
        fine = [[ops[d]['diag']] + [ops[d]['level'](lv) for lv in GLA_LEVELS[2:]] for d in range(2)]
        for u, (d, hd) in enumerate(units):
            a_scr[u] = a_coarse[u] + functools.reduce(lambda x, y: x + y, [term(pr, hd) for pr in fine[d]])

    a = [a_scr[u] for u in range(len(units))]
    st = [s_scr[d, hd] for d, hd in units]
    vs = [ops[d]['v'][:, hd * D_DV:(hd + 1) * D_DV] for d, hd in units]
    o = [_dot_nt(ops[d]['qg'][:, heads[hd]], st[u]) + _dot(a[u], vs[u]) for u, (d, hd) in enumerate(units)]
    st_new = [st[u] * ops[d]['etot'][:, heads[hd]] + _dot_tn(vs[u], ops[d]['kd'][:, heads[hd]])
              for u, (d, hd) in enumerate(units)]
    for u, (d, hd) in enumerate(units):
        (ob_ref if d else of_ref)[:, hd * D_DV:(hd + 1) * D_DV] = o[u]
        s_scr[d, hd] = st_new[u]

    @pl.when(c == pl.num_programs(1) - 1)
    def _():
        sff_ref[0] = s_scr[0]
        sfb_ref[0] = s_scr[1]


def _gla_scan(p, lr, w2, gb, s0f, s0b, buf_f, buf_b, n_seq, n_tok, row0):
    nc = n_tok // CHUNK
    cb0 = row0 // CHUNK
    fwd = lambda s, c: cb0 + s * nc + c
    bwd = lambda s, c: cb0 + s * nc + (nc - 1 - c)

    def stream_specs(ix):
        return [pl.BlockSpec((CHUNK, D_QKW), lambda s, c: (ix(s, c), 0)),
                pl.BlockSpec((CHUNK, D_QKW), lambda s, c: (ix(s, c), 1)),
                pl.BlockSpec((CHUNK, D_VW), lambda s, c: (ix(s, c), 2 * D_QKW // D_VW)),
                pl.BlockSpec((CHUNK, 2 * GATE_RANK), lambda s, c: (ix(s, c), 0))]

    st_spec = pl.BlockSpec((1, D_HEADS, D_DV, D_DK), lambda s, c: (s, 0, 0, 0))
    st_shape = jax.ShapeDtypeStruct((n_seq, D_HEADS, D_DV, D_DK), F32)
    any_spec = pl.BlockSpec(memory_space=pl.ANY)
    return pl.pallas_call(
        _gla_kernel,
        grid=(n_seq, nc),
        in_specs=stream_specs(fwd) + stream_specs(bwd) + [
            pl.BlockSpec((2, 2 * GATE_RANK, D_QKW), lambda s, c: (0, 0, 0)),
            pl.BlockSpec((2, 1, D_QKW), lambda s, c: (0, 0, 0)),
            st_spec, st_spec, any_spec, any_spec],
        out_specs=[pl.BlockSpec((CHUNK, D_VW), lambda s, c: (fwd(s, c), 0)),
                   pl.BlockSpec((CHUNK, D_VW), lambda s, c: (bwd(s, c), 0)),
                   st_spec, st_spec],
        out_shape=[jax.ShapeDtypeStruct(buf_f.shape, F32), jax.ShapeDtypeStruct(buf_b.shape, F32),
                   st_shape, st_shape],
        scratch_shapes=[pltpu.VMEM((2, D_HEADS, D_DV, D_DK), F32), pltpu.VMEM((2 * D_HEADS, CHUNK, CHUNK), F32)],
        input_output_aliases={12: 0, 13: 1},
        compiler_params=_cparams(("parallel", "arbitrary")),
        name="gla_scan",
    )(p, p, p, lr, p, p, p, lr, w2, gb, s0f, s0b, buf_f, buf_b)


def _gdn_prep_kernel(x_ref, prev_ref, next_ref, edge_ref, w_ref, o_ref, *, starts, ends):
    i, j = pl.program_id(0), pl.program_id(1)
    tm, cw = x_ref.shape
    r0 = i * tm
    is_start = functools.reduce(jnp.logical_or, [r0 == s for s in starts])
    is_end = functools.reduce(jnp.logical_or, [r0 + tm == e for e in ends])
    x = x_ref[...]
    prev_row = jnp.where(is_start, edge_ref[0, 7:8, :], prev_ref[7:8, :])
    next_row = jnp.where(is_end, 0.0, next_ref[0:1, :])
    rows = lax.broadcasted_iota(jnp.int32, (tm, 1), 0)
    xp = jnp.where(rows == 0, prev_row, pltpu.roll(x, 1, axis=0))
    xn = jnp.where(rows == tm - 1, next_row, pltpu.roll(x, tm - 1, axis=0))
    y = xp * w_ref[0:1, :] + x * w_ref[1:2, :] + xn * w_ref[2:3, :]
    y = y * _sigmoid(y)
    is_qk = j < 2 * A_QK // cw
    q_scale = jnp.where(j < A_QK // cw, A_DK ** -0.5, 1.0)
    for c0 in range(0, cw, A_DK):
        yh = y[:, c0:c0 + A_DK]
        nrm = yh * (lax.rsqrt(jnp.sum(yh * yh, axis=-1, keepdims=True) + EPS) * q_scale)
        o_ref[:, c0:c0 + A_DK] = jnp.where(is_qk, nrm, yh)


def _gdn_prep(lay, x, edge, conv_w):
    r, c = x.shape
    tm, cw = 512, 512
    starts = [row + s * n for b, n, row, _ in lay.groups for s in range(b)]
    ends = [row + (s + 1) * n for b, n, row, _ in lay.groups for s in range(b)]
    seq_of = lambda i: sum((i * tm >= s).astype(jnp.int32) for s in starts) - 1
    w8 = jnp.pad(conv_w.astype(F32), ((0, 8 - conv_w.shape[0]), (0, 0)))
    return pl.pallas_call(
        functools.partial(_gdn_prep_kernel, starts=starts, ends=ends),
        grid=(r // tm, c // cw),
        in_specs=[pl.BlockSpec((tm, cw), lambda i, j: (i, j)),
                  pl.BlockSpec((8, cw), lambda i, j: (jnp.maximum(i * (tm // 8) - 1, 0), j)),
                  pl.BlockSpec((8, cw), lambda i, j: (jnp.minimum((i + 1) * (tm // 8), r // 8 - 1), j)),
                  pl.BlockSpec((1, 8, cw), lambda i, j: (seq_of(i), 0, j)),
                  pl.BlockSpec((8, cw), lambda i, j: (0, j))],
        out_specs=pl.BlockSpec((tm, cw), lambda i, j: (i, j)),
        out_shape=jax.ShapeDtypeStruct((r, c), F32),
        compiler_params=_cparams(("parallel", "parallel")),
        name="gdn_prep",
    )(x, x, x, edge, w8)


def _gqa_prep_kernel(q_ref, kv_ref, cos_ref, sin_ref, qg_ref, kg_ref, qt_ref, k_ref, vt_ref):
    cos, sin = cos_ref[...], sin_ref[...]

    def norm_rope(x, g):
        xn = x * lax.rsqrt(jnp.mean(x * x, axis=-1, keepdims=True) + EPS) * g
        return xn * cos + pltpu.roll(xn, B_DH // 2, axis=1) * sin

    for hd in range(B_HEADS):
        sl = slice(hd * B_DH, (hd + 1) * B_DH)
        qt_ref[sl, :] = norm_rope(q_ref[:, sl], qg_ref[...]).T.astype(BF16)
    for hd in range(B_KV_HEADS):
        sl = slice(hd * B_DH, (hd + 1) * B_DH)
        k_ref[:, sl] = norm_rope(kv_ref[:, sl], kg_ref[...]).astype(BF16)
        vt_ref[sl, :] = kv_ref[:, B_KVW + hd * B_DH:B_KVW + (hd + 1) * B_DH].T.astype(BF16)


def _pos_block(lay, tm):
    def pos(i):
        r = i * tm
        out = 0
        for b, n, row, _ in lay.groups:
            inside = (r >= row) & (r < row + b * n)
            out = out + jnp.where(inside, ((r - row) % n) // tm, 0)
        return out
    return pos


def _gqa_prep(lay, pq, pkv, cos, sin, qg, kg):
    r = pq.shape[0]
    tm = min(512, min(n for _, n, _, _ in lay.groups))
    pos = _pos_block(lay, tm)
    row = lambda i: (i, 0)
    col = lambda i: (0, i)
    fixed = lambda i: (0, 0)
    return pl.pallas_call(
        _gqa_prep_kernel,
        grid=(r // tm,),
        in_specs=[pl.BlockSpec((tm, B_QW), row), pl.BlockSpec((tm, 2 * B_KVW), row),
                  pl.BlockSpec((tm, B_DH), lambda i: (pos(i), 0)), pl.BlockSpec((tm, B_DH), lambda i: (pos(i), 0)),
                  pl.BlockSpec((1, B_DH), fixed), pl.BlockSpec((1, B_DH), fixed)],
        out_specs=[pl.BlockSpec((B_QW, tm), col), pl.BlockSpec((tm, B_KVW), row), pl.BlockSpec((B_KVW, tm), col)],
        out_shape=[jax.ShapeDtypeStruct((B_QW, r), BF16), jax.ShapeDtypeStruct((r, B_KVW), BF16),
                   jax.ShapeDtypeStruct((B_KVW, r), BF16)],
        compiler_params=_cparams(("parallel",)),
        name="gqa_prep",
    )(pq, pkv, cos, sin, qg.reshape(1, B_DH), kg.reshape(1, B_DH))


def _diff_prep_kernel(q_ref, k_ref, v_ref, qg_ref, kg_ref, qt_ref, ko_ref, vt_ref):
    lo = lax.broadcasted_iota(jnp.int32, (1, LANES), 1) < C_DQK

    def norm_maps(x, g):
        sq = x * x
        s_lo = jnp.sum(jnp.where(lo, sq, 0.0), axis=-1, keepdims=True)
        s_hi = jnp.sum(jnp.where(lo, 0.0, sq), axis=-1, keepdims=True)
        inv = jnp.where(lo, lax.rsqrt(s_lo * (1.0 / C_DQK) + EPS), lax.rsqrt(s_hi * (1.0 / C_DQK) + EPS))
        return x * inv * g

    qt_ref[...] = norm_maps(q_ref[...], qg_ref[...]).T.astype(BF16)
    ko_ref[...] = norm_maps(k_ref[...], kg_ref[...]).astype(BF16)
    vt_ref[...] = v_ref[...].T.astype(BF16)


def _diff_prep(p, qg, kg):
    r = p.shape[0]
    tm = 512 if r % 512 == 0 else LANES
    fixed = lambda i, h: (0, 0)
    return pl.pallas_call(
        _diff_prep_kernel,
        grid=(r // tm, C_HEADS),
        in_specs=[pl.BlockSpec((tm, LANES), lambda i, h: (i, h)),
                  pl.BlockSpec((tm, LANES), lambda i, h: (i, C_HEADS + h)),
                  pl.BlockSpec((tm, LANES), lambda i, h: (i, 2 * C_HEADS + h)),
                  pl.BlockSpec((1, LANES), fixed), pl.BlockSpec((1, LANES), fixed)],
        out_specs=[pl.BlockSpec((LANES, tm), lambda i, h: (h, i)), pl.BlockSpec((tm, LANES), lambda i, h: (i, h)),
                   pl.BlockSpec((LANES, tm), lambda i, h: (h, i))],
        out_shape=[jax.ShapeDtypeStruct((C_QKW, r), BF16), jax.ShapeDtypeStruct((r, C_QKW), BF16),
                   jax.ShapeDtypeStruct((C_VW, r), BF16)],
        compiler_params=_cparams(("parallel", "parallel")),
        name="diff_prep",
    )(p, p, p, qg.reshape(1, LANES), kg.reshape(1, LANES))


class _Layout:
    def __init__(self, shapes):
        self.groups = []
        row, seq = 0, 0
        for b, n in shapes:
            self.groups.append((b, n, row, seq))
            row += b * n
            seq += b
        self.rows, self.n_seq = row, seq

    def to_seqs(self, main, meta):
        out = []
        c = main.shape[-1]
        for b, n, row, seq in self.groups:
            m = meta[seq * N_META:(seq + b) * N_META].reshape(b, N_META, c)
            x = main[row:row + b * n].reshape(b, n, c)
            out.append(jnp.concatenate([m, x], axis=1))
        return out

    def from_seqs(self, seqs):
        main = jnp.concatenate([s[:, N_META:].reshape(-1, s.shape[-1]) for s in seqs], axis=0)
        meta = jnp.concatenate([s[:, :N_META].reshape(-1, s.shape[-1]) for s in seqs], axis=0)
        return main, meta


def _meta_chunk(meta):
    c = meta.shape[-1]
    m = meta.reshape(-1, N_META, c)
    return jnp.pad(m, ((0, 0), (CHUNK - N_META, 0), (0, 0))).reshape(-1, c)


def _meta_unchunk(x):
    c = x.shape[-1]
    return x.reshape(-1, CHUNK, c)[:, CHUNK - N_META:].reshape(-1, c)


def _map_seqs(lay, fn, main, meta):
    return lay.from_seqs([fn(s) for s in lay.to_seqs(main, meta)])


def _run_scan(scan, lay, main_args, meta_args, state_shape, width):
    s_all = lay.n_seq
    zeros = jnp.zeros((s_all,) + state_shape, F32)
    mrows = s_all * CHUNK
    mbuf = jnp.zeros((mrows, width), F32)
    _, _, s_meta, _ = scan(*meta_args, zeros, zeros, mbuf, mbuf, s_all, CHUNK, 0)
    of = jnp.zeros((lay.rows, width), F32)
    ob = jnp.zeros((lay.rows, width), F32)
    finals = []
    for b, n, row, seq in lay.groups:
        of, ob, _, sfb = scan(*main_args, s_meta[seq:seq + b], zeros[seq:seq + b], of, ob, b, n, row)
        finals.append(sfb)
    mof, mob, _, _ = scan(*meta_args, zeros, jnp.concatenate(finals, axis=0), mbuf, mbuf, s_all, CHUNK, 0)
    return (of, ob), (_meta_unchunk(mof), _meta_unchunk(mob))


def _gdn_layer(lay, h, hm, norm_g, w_in, conv_w, a_log, dt_bias, out_g, w_out):
    w_qkv, w_z, w_ab = (w_in[:, :A_CONV_CH], w_in[:, A_CONV_CH:A_CONV_CH + A_VW], w_in[:, A_CONV_CH + A_VW:])
    x, xm = _norm_proj(h, norm_g, w_qkv), _norm_proj(hm, norm_g, w_qkv)
    z, zm = _norm_proj(h, norm_g, w_z), _norm_proj(hm, norm_g, w_z)
    ab, abm = _norm_proj(h, norm_g, w_ab), _norm_proj(hm, norm_g, w_ab)
    s_all = lay.n_seq
    xm3 = xm.reshape(s_all, N_META, A_CONV_CH)
    qkv = _gdn_prep(lay, x, xm3[:, N_META - 8:], conv_w)

    def prep(x):
        prev = jnp.pad(x[:, :-1], ((0, 0), (1, 0), (0, 0)))
        nxt = jnp.pad(x[:, 1:], ((0, 0), (0, 1), (0, 0)))
        y = jax.nn.silu(prev * conv_w[0] + x * conv_w[1] + nxt * conv_w[2])
        b, l, _ = y.shape
        qk = y[..., :2 * A_QK].reshape(b, l, 2 * A_QK_HEADS, A_DK)
        qk = qk * lax.rsqrt(jnp.sum(qk * qk, axis=-1, keepdims=True) + EPS)
        qk = qk.reshape(b, l, 2, A_QK) * jnp.array([A_DK ** -0.5, 1.0], F32)[:, None]
        return jnp.concatenate([qk.reshape(b, l, 2 * A_QK), y[..., 2 * A_QK:]], axis=-1)

    first = x[np.array([row + s * n for b, n, row, _ in lay.groups for s in range(b)])]
    qkvm = prep(jnp.concatenate([xm3, first[:, None]], axis=1))[:, :N_META].reshape(-1, A_CONV_CH)

    def gates(x):
        x = x.reshape(-1, 2, 2, A_V_HEADS)
        g = -jnp.exp(a_log.astype(F32)) * jax.nn.softplus(x[:, 0] + dt_bias.astype(F32))
        return g, jax.nn.sigmoid(x[:, 1])

    ng, nv = A_QK_HEADS // A_HG, 2 * A_HG

    def pack(g, beta, d):
        r = g.shape[0]
        gd, bd = g[:, d].reshape(r, ng, nv), beta[:, d].reshape(r, ng, nv)
        col = jnp.concatenate([gd, bd, jnp.zeros((r, ng, LANES - 2 * nv), F32)], axis=-1).reshape(r, ng * LANES)
        row = gd.reshape(r // CHUNK, CHUNK, ng, nv).transpose(0, 2, 3, 1)
        return col, row

    def scan_args(qkv_rows, g, beta):
        colf, rowf = pack(g, beta, 0)
        colb, rowb = pack(g, beta, 1)
        return (qkv_rows, colf, colb, rowf, rowb)

    g, beta = gates(ab)
    gm, betam = gates(abm)
    pad_gate = lambda t: _meta_chunk(t.reshape(-1, 2 * A_V_HEADS)).reshape(-1, 2, A_V_HEADS)
    (of, ob), (mof, mob) = _run_scan(
        _gdn_scan, lay, scan_args(qkv, g, beta), scan_args(_meta_chunk(qkvm), pad_gate(gm), pad_gate(betam)),
        (A_V_HEADS, A_DK, A_DV), A_VW)
    gain = jnp.tile(out_g.astype(F32), A_V_HEADS)
    return (_out_proj([of, ob], z, gain, w_out, h, A_DV), _out_proj([mof, mob], zm, gain, w_out, hm, A_DV))


def _rope_tables(n_tok):
    rows = n_tok // GRID_W
    row = jnp.repeat(jnp.arange(rows, dtype=F32), GRID_W)
    col = (jnp.arange(rows * GRID_W) % GRID_W).astype(F32)
    inv = ROPE_THETA ** (-jnp.arange(0, ROPE_AXIS, 2, dtype=F32) / ROPE_AXIS)
    ang = jnp.concatenate([row[:, None] * inv, col[:, None] * inv], axis=-1)
    return jnp.cos(ang), jnp.sin(ang)


def _pad_meta(x, n_seq):
    return jnp.pad(x.reshape(n_seq, N_META, -1), ((0, 0), (0, LANES - N_META), (0, 0)))


def _gqa_layer(lay, h, hm, norm_g, w_in, q_g, k_g, w_out):
    kv_end = B_QW + 2 * B_KVW
    pq, pkv, z = (_norm_proj(h, norm_g, w_in[:, :B_QW]), _norm_proj(h, norm_g, w_in[:, B_QW:kv_end]),
                  _norm_proj(h, norm_g, w_in[:, kv_end:]))
    pm = _norm_proj(hm, norm_g, w_in)

    def norm_heads(x, g):
        r = x.shape[0]
        x = x.reshape(r, -1, B_DH)
        return x * lax.rsqrt(jnp.mean(x * x, axis=-1, keepdims=True) + EPS) * g.astype(F32)

    scale = B_DH ** -0.5 * LOG2E
    cos, sin = _rope_tables(max(n for _, n, _, _ in lay.groups))
    qt, k, vt = _gqa_prep(lay, pq, pkv, jnp.concatenate([cos, cos], axis=-1), jnp.concatenate([-sin, sin], axis=-1),
                          q_g.astype(F32) * scale, k_g.astype(F32))
    s_all = lay.n_seq
    qm = (norm_heads(pm[:, :B_QW], q_g) * scale).reshape(-1, B_QW).astype(BF16)
    km = norm_heads(pm[:, B_QW:B_QW + B_KVW], k_g).reshape(-1, B_KVW).astype(BF16)
    vm = pm[:, B_QW + B_KVW:B_QW + 2 * B_KVW].astype(BF16)
    zm = pm[:, B_QW + 2 * B_KVW:]
    qmt = _pad_meta(qm, s_all).reshape(s_all * LANES, B_QW).T
    kmp, vmt = _pad_meta(km, s_all), _pad_meta(vm, s_all).transpose(0, 2, 1)

    o = jnp.zeros((lay.rows, B_QW), F32)
    om = jnp.zeros((s_all * LANES, B_QW), F32)
    for b, n, row, seq in lay.groups:
        tq, tk = min(GQA_TQ, n), min(GQA_TK, n)
        o = _gqa_attention(qt, k, vt, kmp[seq:seq + b], vmt[seq:seq + b], o, b, n, n, row, row, tq, tk)
        om = _gqa_attention(qmt, k, vt, kmp[seq:seq + b], vmt[seq:seq + b], om, b, LANES, n,
                            seq * LANES, row, LANES, tk)
    om = om.reshape(s_all, LANES, B_QW)[:, :N_META].reshape(-1, B_QW)
    ones = jnp.ones((B_QW,), F32)
    return _out_proj([o], z, ones, w_out, h, 0), _out_proj([om], zm, ones, w_out, hm, 0)


def _t5_bucket_np(rel):
    nb = REL_BUCKETS // 2
    max_exact = nb // 2
    n = np.abs(rel)
    steps = nb - max_exact
    large = np.full(n.shape, max_exact, np.int64)
    for kstep in range(1, steps + 1):
        large += (n.astype(np.float64) ** steps >= (max_exact ** steps) * float(REL_MAX_DIST // max_exact) ** kstep)
    large = np.minimum(large, nb - 1)
    return np.where(rel > 0, nb, 0) + np.where(n < max_exact, n, large)


def _diff_layer(lay, h, hm, norm_g, w_in, q_g, k_g, lam, sub_g, w_out, rel_bias, layer_idx):
    n_qkv = 2 * C_QKW + C_VW
    p, z = _norm_proj(h, norm_g, w_in[:, :n_qkv]), _norm_proj(h, norm_g, w_in[:, n_qkv:])
    pm = _norm_proj(hm, norm_g, w_in)

    def norm_maps(x, g):
        r = x.shape[0]
        x = x.reshape(r, -1, C_DQK)
        return (x * lax.rsqrt(jnp.mean(x * x, axis=-1, keepdims=True) + EPS) * g.astype(F32)).reshape(r, -1)

    s_all = lay.n_seq
    scale = C_DQK ** -0.5 * LOG2E
    qt, k, vt = _diff_prep(p, jnp.tile(q_g.astype(F32), 2) * scale, jnp.tile(k_g.astype(F32), 2))
    qm = (norm_maps(pm[:, :C_QKW], q_g) * scale).astype(BF16)
    km = norm_maps(pm[:, C_QKW:2 * C_QKW], k_g).astype(BF16)
    vm = pm[:, 2 * C_QKW:n_qkv].astype(BF16)
    zm = pm[:, n_qkv:]
    qmt = _pad_meta(qm, s_all).reshape(s_all * LANES, C_QKW).T
    kmp, vmt = _pad_meta(km, s_all), _pad_meta(vm, s_all).transpose(0, 2, 1)

    lam = lam.astype(F32)
    lam_init = 0.8 - 0.6 * math.exp(-0.3 * layer_idx)
    lam_full = jnp.exp(jnp.sum(lam[0] * lam[1])) - jnp.exp(jnp.sum(lam[2] * lam[3])) + lam_init
    table = rel_bias.astype(F32).T * LOG2E
    look = lambda rel: table[:, _t5_bucket_np(rel)]
    nb = REL_BUCKETS // 2
    cst = jnp.zeros((C_HEADS, 8, LANES), F32)
    cst = cst.at[:, 0].set(table[:, nb - 1:nb]).at[:, 1].set(table[:, 2 * nb - 1:2 * nb]).at[:, 2].set(lam_full)
    ar = np.arange(LANES)
    key_pad = jnp.asarray(np.where(ar < N_META, 0.0, NEG), F32)[:, None]
    offsets = range(-(BIAS_TILES // 2), BIAS_TILES // 2 + 1)
    tt = jnp.stack([look(o * LANES + ar[:, None] - ar[None, :]) for o in offsets], axis=1)
    mq = np.minimum(ar, N_META - 1)
    tt_m = jnp.stack([look(o * LANES + N_META + ar[:, None] - mq[None, :]) for o in offsets], axis=1)
    meta_rows = (ar < N_META)[:, None]
    bm_m = (jnp.where(meta_rows, look(ar[:, None] - mq[None, :]), 0.0) + key_pad)[:, None]

    o = jnp.zeros((lay.rows, C_VW), F32)
    om = jnp.zeros((s_all * LANES, C_VW), F32)
    for b, n, row, seq in lay.groups:
        t, tk = min(DIFF_TQ, n), min(DIFF_TK, n)
        pq = np.arange(t)
        near = look(ar[:N_META, None] - N_META - pq[None, :])
        far = jnp.broadcast_to(table[:, nb - 1][:, None, None], near.shape)
        pad_rows = jnp.full((C_HEADS, 2, LANES - N_META, t), NEG, F32)
        bm = jnp.concatenate([jnp.stack([near, far], axis=1), pad_rows], axis=2)
        o = _diff_attention(qt, k, vt, kmp[seq:seq + b], vmt[seq:seq + b], tt, bm, cst, o,
                            b, n, n, row, row, t, tk, False)
        om = _diff_attention(qmt, k, vt, kmp[seq:seq + b], vmt[seq:seq + b], tt_m, bm_m, cst, om,
                             b, LANES, n, seq * LANES, row, LANES, tk, True)
    om = om.reshape(s_all, LANES, C_VW)[:, :N_META].reshape(-1, C_VW)
    gain = jnp.tile(sub_g.astype(F32), C_HEADS) * (1.0 - lam_init)
    return _out_proj([o], z, gain, w_out, h, C_DV), _out_proj([om], zm, gain, w_out, hm, C_DV)


def _gla_layer(lay, h, hm, norm_g, w_in, gate_w2, gate_b, out_g, w_out):
    n_qkv = 2 * D_QKW + D_VW
    w_qkv, w_z, w_lr = w_in[:, :n_qkv], w_in[:, n_qkv:n_qkv + D_VW], w_in[:, n_qkv + D_VW:]
    p, pm = _norm_proj(h, norm_g, w_qkv), _norm_proj(hm, norm_g, w_qkv)
    z, zm = _norm_proj(h, norm_g, w_z), _norm_proj(hm, norm_g, w_z)
    lr, lrm = _norm_proj(h, norm_g, w_lr), _norm_proj(hm, norm_g, w_lr)
    zero = jnp.zeros((GATE_RANK, D_QKW), F32)
    w2 = jnp.stack([jnp.concatenate([gate_w2[0].astype(F32), zero], axis=0),
                    jnp.concatenate([zero, gate_w2[1].astype(F32)], axis=0)])
    gb = gate_b.astype(F32).reshape(2, 1, D_QKW)
    (of, ob), (mof, mob) = _run_scan(
        _gla_scan, lay, (p, lr, w2, gb), (_meta_chunk(pm), _meta_chunk(lrm), w2, gb), (D_HEADS, D_DV, D_DK), D_VW)
    gain = jnp.tile(out_g.astype(F32), D_HEADS)
    return _out_proj([of, ob], z, gain, w_out, h, D_DV), _out_proj([mof, mob], zm, gain, w_out, hm, D_DV)


def _trunk(xs, meta_tokens, rel_bias, a, b, c, d, depth):
    lay = _Layout([(x.shape[0], x.shape[1]) for x in xs])
    h = jnp.concatenate([x.reshape(-1, D_MODEL) for x in xs], axis=0)
    hm = jnp.tile(meta_tokens.astype(F32), (lay.n_seq, 1))
    for i in range(depth):
        m, j = i % 4, i // 4
        if m == 0:
            h, hm = _gdn_layer(lay, h, hm, *(t[j] for t in a))
        elif m == 1:
            h, hm = _gqa_layer(lay, h, hm, *(t[j] for t in b))
        elif m == 2:
            h, hm = _diff_layer(lay, h, hm, *(t[j] for t in c), rel_bias, i)
        else:
            h, hm = _gla_layer(lay, h, hm, *(t[j] for t in d))
    return [h[row:row + bsz * n].reshape(bsz, n, D_MODEL) for bsz, n, row, _ in lay.groups]


def kernel(x_prompt, x_sample, meta_tokens, rel_bias, a_norm, a_w_in, a_conv, a_a_log, a_dt_bias, a_out_norm, a_w_out, b_norm, b_w_in, b_q_norm, b_k_norm, b_w_out, c_norm, c_w_in, c_q_norm, c_k_norm, c_lambda, c_sub_norm, c_w_out, d_norm, d_w_in, d_gate_w2, d_gate_b, d_out_norm, d_w_out):
    y_prompt, y_sample = _trunk(
        [x_prompt, x_sample], meta_tokens, rel_bias,
        (a_norm, a_w_in, a_conv, a_a_log, a_dt_bias, a_out_norm, a_w_out),
        (b_norm, b_w_in, b_q_norm, b_k_norm, b_w_out),
        (c_norm, c_w_in, c_q_norm, c_k_norm, c_lambda, c_sub_norm, c_w_out),
        (d_norm, d_w_in, d_gate_w2, d_gate_b, d_out_norm, d_w_out), 4)
    return (y_prompt, y_sample)
```

```python
import functools
import math

import numpy as np
import jax
import jax.numpy as jnp
from jax import lax
from jax.experimental import pallas as pl
from jax.experimental.pallas import tpu as pltpu

F32 = jnp.float32
BF16 = jnp.bfloat16

D_MODEL = 1024
N_META = 16
CHUNK = 64
GRID_W = 64
EPS = 1e-6
LANES = 128
VMEM_LIMIT = 56 * 1024 * 1024
NEG = -1e30
LOG2E = math.log2(math.e)
STRIP = 1024
ONES_ROWS = 16
PROJ_TN_MAX = 2048
GQA_TQ, GQA_TK = 1024, 2048
DIFF_TQ, DIFF_TK = 2048, 2048
BIAS_TILES = 5

A_QK_HEADS, A_V_HEADS, A_DK, A_DV = 8, 16, 128, 128
A_QK, A_VW = A_QK_HEADS * A_DK, A_V_HEADS * A_DV
A_CONV_CH = 2 * A_QK + A_VW
A_HG = 8
B_HEADS, B_KV_HEADS, B_DH = 8, 2, 128
B_GROUP = B_HEADS // B_KV_HEADS
B_QW, B_KVW = B_HEADS * B_DH, B_KV_HEADS * B_DH
ROPE_AXIS = B_DH // 2
ROPE_THETA = 10000.0
C_HEADS, C_DQK, C_DV = 8, 64, 128
C_QKW, C_VW = C_HEADS * 2 * C_DQK, C_HEADS * C_DV
REL_BUCKETS, REL_MAX_DIST = 32, 128
D_HEADS, D_DK, D_DV = 4, 128, 256
D_QKW, D_VW = D_HEADS * D_DK, D_HEADS * D_DV
GATE_RANK, GATE_TAU = 16, 16.0


def _cparams(sem):
    return pltpu.CompilerParams(dimension_semantics=sem, vmem_limit_bytes=VMEM_LIMIT)


def _sigmoid(x):
    return 1.0 / (1.0 + jnp.exp(-x))


def _dot(a, b):
    return jnp.dot(a.astype(BF16), b.astype(BF16), preferred_element_type=F32)


def _dot_nt(a, b):
    return lax.dot_general(a.astype(BF16), b.astype(BF16), (((1,), (1,)), ((), ())),
                           preferred_element_type=F32)


def _dot_tn(a, b):
    return lax.dot_general(a.astype(BF16), b.astype(BF16), (((0,), (0,)), ((), ())),
                           preferred_element_type=F32)


def _split3(x):
    x1 = x.astype(BF16)
    r1 = x - x1.astype(F32)
    x2 = r1.astype(BF16)
    x3 = (r1 - x2.astype(F32)).astype(BF16)
    return x1, x2, x3


def _dot_exact_lhs(m01, x):
    m = m01.astype(BF16)
    x1, x2, x3 = _split3(x)
    return (jnp.dot(m, x1, preferred_element_type=F32) + jnp.dot(m, x2, preferred_element_type=F32)
            + jnp.dot(m, x3, preferred_element_type=F32))


def _dot_exact_rhs(x, m01):
    m = m01.astype(BF16)
    x1, x2, x3 = _split3(x)
    return (jnp.dot(x1, m, preferred_element_type=F32) + jnp.dot(x2, m, preferred_element_type=F32)
            + jnp.dot(x3, m, preferred_element_type=F32))


def _norm_proj_kernel(x_ref, g_ref, w_ref, o_ref):
    x = x_ref[...]
    ms = jnp.mean(x * x, axis=-1, keepdims=True)
    xn = (x * lax.rsqrt(ms + EPS) * g_ref[...]).astype(BF16)
    o_ref[...] = jnp.dot(xn, w_ref[...], preferred_element_type=F32)


def _norm_proj(x, g, w):
    m, k = x.shape
    n = w.shape[1]
    tm = 512 if m % 512 == 0 else m
    tn = max([t for t in range(256, PROJ_TN_MAX + 1, 256) if n % t == 0], default=n)
    return pl.pallas_call(
        _norm_proj_kernel,
        grid=(n // tn, m // tm),
        in_specs=[pl.BlockSpec((tm, k), lambda j, i: (i, 0)),
                  pl.BlockSpec((1, k), lambda j, i: (0, 0)),
                  pl.BlockSpec((k, tn), lambda j, i: (0, j))],
        out_specs=pl.BlockSpec((tm, tn), lambda j, i: (i, j)),
        out_shape=jax.ShapeDtypeStruct((m, n), F32),
        compiler_params=_cparams(("parallel", "parallel")),
        name="norm_proj",
    )(x, g.reshape(1, k).astype(F32), w.astype(BF16))


def _out_proj_kernel(*refs, n_o, hw):
    o_refs = refs[:n_o]
    z_ref, g_ref, w_ref, h_ref, out_ref = refs[n_o:]
    k = z_ref.shape[1]
    cw = hw if hw else 256
    acc = h_ref[...]
    for c0 in range(0, k, cw):
        o = o_refs[0][:, c0:c0 + cw]
        for r in o_refs[1:]:
            o = o + r[:, c0:c0 + cw]
        if hw:
            ms = jnp.mean(o * o, axis=-1, keepdims=True)
            o = o * lax.rsqrt(ms + EPS) * g_ref[:, c0:c0 + cw]
        z = z_ref[:, c0:c0 + cw]
        gated = o * (z * _sigmoid(z))
        acc = acc + jnp.dot(gated.astype(BF16), w_ref[c0:c0 + cw, :], preferred_element_type=F32)
    out_ref[...] = acc


def _out_proj(os, z, g, w, h, hw):
    m, k = z.shape
    n = w.shape[1]
    tm = (512 if k <= 1024 else 256) if m % 512 == 0 else m
    row = lambda i: (i, 0)
    fixed = lambda i: (0, 0)
    return pl.pallas_call(
        functools.partial(_out_proj_kernel, n_o=len(os), hw=hw),
        grid=(m // tm,),
        in_specs=[pl.BlockSpec((tm, k), row) for _ in os] + [
            pl.BlockSpec((tm, k), row), pl.BlockSpec((1, k), fixed),
            pl.BlockSpec((k, n), fixed), pl.BlockSpec((tm, n), row)],
        out_specs=pl.BlockSpec((tm, n), row),
        out_shape=jax.ShapeDtypeStruct((m, n), F32),
        compiler_params=_cparams(("parallel",)),
        name="out_proj",
    )(*os, z, g.reshape(1, k).astype(F32), w.astype(BF16), h)


def _strip_update(s_t, c, v_ext, m_scr, l_scr, acc_scr, cols):
    p, alpha = _strip_softmax(s_t, c, m_scr, cols)
    _strip_pv(p, alpha, v_ext, l_scr, acc_scr, cols)


def _strip_softmax(s_t, c, m_scr, cols):
    m_prev = m_scr[:, cols]
    m_new = jnp.maximum(m_prev, jnp.max(s_t, axis=0, keepdims=True) + c)
    m_scr[:, cols] = m_new
    return jnp.exp2(s_t - (m_new - c)).astype(BF16), jnp.exp2(m_prev - m_new)


def _strip_pv(p, alpha, v_ext, l_scr, acc_scr, cols):
    dv = acc_scr.shape[0]
    pv = jnp.dot(v_ext, p, preferred_element_type=F32)
    acc_scr[:, cols] = alpha * acc_scr[:, cols] + pv[:dv]
    l_scr[:, cols] = alpha * l_scr[:, cols] + pv[dv:dv + 1]


def _run_strips(score_fns, cols_list, c, v_ext, m_scr, l_scr, acc_scr):
    s_next = score_fns[0]()
    pending = None
    for n, cols in enumerate(cols_list):
        s_t, s_next = s_next, (score_fns[n + 1]() if n + 1 < len(score_fns) else None)
        if pending is not None:
            _strip_pv(pending[0], pending[1], v_ext, l_scr, acc_scr, pending[2])
        p, alpha = _strip_softmax(s_t, c, m_scr, cols)
        pending = (p, alpha, cols)
    _strip_pv(pending[0], pending[1], v_ext, l_scr, acc_scr, pending[2])


def _attn_init(m_scr, l_scr, acc_scr):
    m_scr[...] = jnp.full(m_scr.shape, NEG, F32)
    l_scr[...] = jnp.zeros(l_scr.shape, F32)
    acc_scr[...] = jnp.zeros(acc_scr.shape, F32)


def _with_ones(v_t):
    return jnp.concatenate([v_t, jnp.ones((ONES_ROWS, v_t.shape[1]), BF16)], axis=0)


def _strips(width):
    w = min(STRIP, width)
    return [slice(c0, c0 + w) for c0 in range(0, width, w)]


def _gqa_kernel(qt_ref, k_ref, vt_ref, km_ref, vmt_ref, buf_ref, o_ref, m_scr, l_scr, acc_scr):
    del buf_ref
    ki = pl.program_id(3)
    tq = qt_ref.shape[1]
    heads = [slice(g * B_DH, (g + 1) * B_DH) for g in range(B_GROUP)]

    @pl.when(ki == 0)
    def _():
        _attn_init(m_scr, l_scr, acc_scr)
        key = lax.broadcasted_iota(jnp.int32, (LANES, 1), 0)
        pad = jnp.where(key < N_META, 0.0, NEG)
        vm_ext = _with_ones(vmt_ref[0])
        for g, rows in enumerate(heads):
            for cols in _strips(tq):
                s_t = jnp.dot(km_ref[0], qt_ref[rows, cols], preferred_element_type=F32) + pad
                _strip_update(s_t, 0.0, vm_ext, m_scr, l_scr, acc_scr,
                              slice(g * tq + cols.start, g * tq + cols.stop))

    k = k_ref[...]
    v_ext = _with_ones(vt_ref[...])
    work = [(g, rows, cols) for g, rows in enumerate(heads) for cols in _strips(tq)]
    scr_cols = [slice(g * tq + cols.start, g * tq + cols.stop) for g, _, cols in work]
    _run_strips([functools.partial(lambda rows, cols: jnp.dot(k, qt_ref[rows, cols], preferred_element_type=F32),
                                   rows, cols) for _, rows, cols in work],
                scr_cols, 0.0, v_ext, m_scr, l_scr, acc_scr)

    @pl.when(ki == pl.num_programs(3) - 1)
    def _():
        for g, rows in enumerate(heads):
            cols = slice(g * tq, (g + 1) * tq)
            o_ref[:, rows] = (acc_scr[:, cols] / l_scr[:, cols]).T


def _gqa_attention(qt, k, vt, km, vmt, buf, n_seq, n_q, n_k, q_col0, k_row0, tq, tk):
    nq, nk = n_q // tq, n_k // tk
    qb0, kb0 = q_col0 // tq, k_row0 // tk
    r = B_GROUP * tq
    gw = B_GROUP * B_DH
    return pl.pallas_call(
        _gqa_kernel,
        grid=(n_seq, B_KV_HEADS, nq, nk),
        in_specs=[pl.BlockSpec((gw, tq), lambda s, h, i, j: (h, qb0 + s * nq + i)),
                  pl.BlockSpec((tk, B_DH), lambda s, h, i, j: (kb0 + s * nk + j, h)),
                  pl.BlockSpec((B_DH, tk), lambda s, h, i, j: (h, kb0 + s * nk + j)),
                  pl.BlockSpec((1, LANES, B_DH), lambda s, h, i, j: (s, 0, h)),
                  pl.BlockSpec((1, B_DH, LANES), lambda s, h, i, j: (s, h, 0)),
                  pl.BlockSpec(memory_space=pl.ANY)],
        out_specs=pl.BlockSpec((tq, gw), lambda s, h, i, j: (qb0 + s * nq + i, h)),
        out_shape=jax.ShapeDtypeStruct(buf.shape, F32),
        scratch_shapes=[pltpu.VMEM((1, r), F32), pltpu.VMEM((1, r), F32), pltpu.VMEM((B_DH, r), F32)],
        input_output_aliases={5: 0},
        compiler_params=_cparams(("parallel", "parallel", "parallel", "arbitrary")),
        name="gqa_attention",
    )(qt, k, vt, km, vmt, buf)


def _diff_kernel(qt_ref, k_ref, vt_ref, km_ref, vmt_ref, tt_ref, bm_ref, cst_ref, buf_ref, o_ref,
                 m_scr, l_scr, acc_scr, *, meta_q):
    del buf_ref
    qi, ki = pl.program_id(2), pl.program_id(3)
    tq, tk = qt_ref.shape[1], k_ref.shape[0]
    nt = tk // LANES
    qt = qt_ref[...]
    dim = lax.broadcasted_iota(jnp.int32, qt.shape, 0)
    zero = jnp.zeros_like(qt)
    maps = (jnp.where(dim < C_DQK, qt, zero), jnp.where(dim >= C_DQK, qt, zero))
    c_left = cst_ref[0, 0:1, 0:1]
    c_right = cst_ref[0, 1:2, 0:1]
    lam = cst_ref[0, 2:3, 0:1]
    strips = [(mp, cols) for mp in range(2) for cols in _strips(tq)]
    scr = lambda mp, cols: slice(mp * tq + cols.start, mp * tq + cols.stop)

    @pl.when(ki == 0)
    def _():
        _attn_init(m_scr, l_scr, acc_scr)
        vm_ext = _with_ones(vmt_ref[0])
        for mp, cols in strips:
            s_t = jnp.dot(km_ref[0], maps[mp][:, cols], preferred_element_type=F32) + bm_ref[0, 0, :, cols]
            _strip_update(s_t, 0.0, vm_ext, m_scr, l_scr, acc_scr, scr(mp, cols))

    k = k_ref[...]
    v_ext = _with_ones(vt_ref[...])

    ratio = max(tq // tk, 1)
    d = ki - qi * ratio
    near = (0, 0) if meta_q else (-1, ratio)
    mid = BIAS_TILES // 2

    def strip_bias(cols):
        q_tiles = range(cols.start // LANES, cols.stop // LANES)
        if meta_q:
            rows = [[tt_ref[0, mid]]] + [[tt_ref[0, BIAS_TILES - 1]] for _ in range(nt - 1)]
        else:
            rows = [[tt_ref[0, jnp.clip(d * nt + (b - a), -mid, mid) + mid] for a in q_tiles] for b in range(nt)]
        return jnp.concatenate([jnp.concatenate(r, axis=1) for r in rows], axis=0)

    def run(c, bias):
        def scores(mp, cols):
            s_t = jnp.dot(k, maps[mp][:, cols], preferred_element_type=F32)
            return s_t if bias is None else s_t + bias(cols)

        _run_strips([functools.partial(scores, mp, cols) for mp, cols in strips],
                    [scr(mp, cols) for mp, cols in strips], c, v_ext, m_scr, l_scr, acc_scr)

    is_near = (d >= near[0]) & (d <= near[1])

    @pl.when(is_near)
    def _():
        run(0.0, strip_bias)

    @pl.when(jnp.logical_not(is_near))
    def _():
        run(jnp.where(d < 0, c_left, c_right), None)

    @pl.when(ki == pl.num_programs(3) - 1)
    def _():
        o1 = acc_scr[:, :tq] / l_scr[:, :tq]
        o2 = acc_scr[:, tq:] / l_scr[:, tq:]
        o_ref[...] = (o1 - lam * o2).T


def _diff_attention(qt, k, vt, km, vmt, tt, bm, cst, buf, n_seq, n_q, n_k, q_col0, k_row0, tq, tk, meta_q):
    nq, nk = n_q // tq, n_k // tk
    qb0, kb0 = q_col0 // tq, k_row0 // tk
    r = 2 * tq
    return pl.pallas_call(
        functools.partial(_diff_kernel, meta_q=meta_q),
        grid=(n_seq, C_HEADS, nq, nk),
        in_specs=[pl.BlockSpec((LANES, tq), lambda s, h, i, j: (h, qb0 + s * nq + i)),
                  pl.BlockSpec((tk, LANES), lambda s, h, i, j: (kb0 + s * nk + j, h)),
                  pl.BlockSpec((C_DV, tk), lambda s, h, i, j: (h, kb0 + s * nk + j)),
                  pl.BlockSpec((1, LANES, LANES), lambda s, h, i, j: (s, 0, h)),
                  pl.BlockSpec((1, C_DV, LANES), lambda s, h, i, j: (s, h, 0)),
                  pl.BlockSpec((1, BIAS_TILES, LANES, LANES), lambda s, h, i, j: (h, 0, 0, 0)),
                  pl.BlockSpec((1, 1, LANES, tq), lambda s, h, i, j: (h, jnp.minimum(i, 1), 0, 0)),
                  pl.BlockSpec((1, 8, LANES), lambda s, h, i, j: (h, 0, 0)),
                  pl.BlockSpec(memory_space=pl.ANY)],
        out_specs=pl.BlockSpec((tq, C_DV), lambda s, h, i, j: (qb0 + s * nq + i, h)),
        out_shape=jax.ShapeDtypeStruct(buf.shape, F32),
        scratch_shapes=[pltpu.VMEM((1, r), F32), pltpu.VMEM((1, r), F32), pltpu.VMEM((C_DV, r), F32)],
        input_output_aliases={8: 0},
        compiler_params=_cparams(("parallel", "parallel", "parallel", "arbitrary")),
        name="diff_attention",
    )(qt, k, vt, km, vmt, tt, bm, cst, buf)


def _tri_masks(rev):
    i = lax.broadcasted_iota(jnp.int32, (CHUNK, CHUNK), 0)
    j = lax.broadcasted_iota(jnp.int32, (CHUNK, CHUNK), 1)
    if rev:
        return j >= i, j > i, i, j
    return j <= i, j < i, i, j


def _cum_mat_t(rev):
    t = lax.broadcasted_iota(jnp.int32, (CHUNK, CHUNK), 0)
    j = lax.broadcasted_iota(jnp.int32, (CHUNK, CHUNK), 1)
    return jnp.where((t >= j) if rev else (t <= j), 1.0, 0.0)


def _gdn_kernel(qf_ref, kf_ref, vf_ref, cf_ref, rf_ref, qb_ref, kb_ref, vb_ref, cb_ref, rb_ref,
                s0f_ref, s0b_ref, bf_buf, bb_buf, of_ref, ob_ref, sff_ref, sfb_ref, s_scr):
    del bf_buf, bb_buf
    c = pl.program_id(2)
    nv = 2 * A_HG

    @pl.when(c == 0)
    def _():
        s_scr[0] = s0f_ref[0]
        s_scr[1] = s0b_ref[0]

    streams = ((qf_ref, kf_ref, vf_ref, cf_ref, rf_ref, False), (qb_ref, kb_ref, vb_ref, cb_ref, rb_ref, True))
    units = [(d, e) for d in range(2) for e in range(nv)]
    eye = jnp.where(_tri_masks(False)[0] & _tri_masks(True)[0], 1.0, 0.0)
    q, k, kk, qk, gc_all, gr_all, col, masks = [], [], [], [], [], [], [], []
    for q_ref, k_ref, _, c_ref, r_ref, rev in streams:
        tri, strict, _, _ = _tri_masks(rev)
        masks.append((tri, strict))
        qd = [q_ref[:, j * A_DK:(j + 1) * A_DK] for j in range(A_HG)]
        kd = [k_ref[:, j * A_DK:(j + 1) * A_DK] for j in range(A_HG)]
        q.append(qd)
        k.append(kd)
        kk.append([_dot_nt(x, x) for x in kd])
        qk.append([_dot_nt(x, y) for x, y in zip(qd, kd)])
        cv = c_ref[...]
        col.append(cv)
        gc_all.append(_dot_exact_lhs(jnp.where(tri, 1.0, 0.0), cv))
        gr_all.append(_dot_exact_rhs(r_ref[0, 0], _cum_mat_t(rev)))
    gcb = [jnp.broadcast_to(gc_all[d][:, e:e + 1], (CHUNK, LANES)) for d, e in units]
    beta = [jnp.broadcast_to(col[d][:, nv + e:nv + e + 1], (CHUNK, LANES)) for d, e in units]
    decay = [jnp.exp(jnp.where(masks[d][0], gcb[u][:, :CHUNK] - gr_all[d][e:e + 1, :], -jnp.inf))
             for u, (d, e) in enumerate(units)]
    a = [jnp.where(masks[d][1], beta[u][:, :CHUNK] * kk[d][e // 2] * decay[u], 0.0)
         for u, (d, e) in enumerate(units)]
    x = [eye - au for au in a]
    p = a
    for _ in range(5):
        p = [_dot(pu, pu) for pu in p]
        x = [xu + _dot(xu, pu) for xu, pu in zip(x, p)]
    egc = [jnp.exp(g) for g in gcb]
    v = [streams[d][2][:, e * A_DV:(e + 1) * A_DV] for d, e in units]
    uw = [_dot(x[u], jnp.concatenate([v[u] * beta[u], k[d][e // 2] * (beta[u] * egc[u])], axis=1))
          for u, (d, e) in enumerate(units)]
    s = [s_scr[d, e] for d, e in units]
    r2 = [_dot(jnp.concatenate([uw[u][:, A_DV:], q[d][e // 2] * egc[u]], axis=0), s[u])
          for u, (d, e) in enumerate(units)]
    v_new = [uw[u][:, :A_DV] - r2[u][:CHUNK] for u in range(len(units))]
    o = [r2[u][CHUNK:] + _dot(qk[d][e // 2] * decay[u], v_new[u]) for u, (d, e) in enumerate(units)]
    gtot = [gcb[u][0:1, :] if d else gcb[u][CHUNK - 1:CHUNK, :] for u, (d, e) in enumerate(units)]
    s_new = [s[u] * jnp.exp(gtot[u]) + _dot_tn(k[d][e // 2] * jnp.exp(gtot[u] - gcb[u]), v_new[u])
             for u, (d, e) in enumerate(units)]
    for u, (d, e) in enumerate(units):
        (ob_ref if d else of_ref)[:, e * A_DV:(e + 1) * A_DV] = o[u]
        s_scr[d, e] = s_new[u]

    @pl.when(c == pl.num_programs(2) - 1)
    def _():
        sff_ref[0] = s_scr[0]
        sfb_ref[0] = s_scr[1]


def _gdn_scan(qkv, colf, colb, rowf, rowb, s0f, s0b, buf_f, buf_b, n_seq, n_tok, row0):
    nc = n_tok // CHUNK
    cb0 = row0 // CHUNK
    ng = A_QK_HEADS // A_HG
    nv = 2 * A_HG
    fwd = lambda s, j, c: cb0 + s * nc + c
    bwd = lambda s, j, c: cb0 + s * nc + (nc - 1 - c)
    k_blk, v_blk = A_QK // (A_HG * A_DK), 2 * A_QK // (nv * A_DV)

    def stream_specs(ix):
        return [pl.BlockSpec((CHUNK, A_HG * A_DK), lambda s, j, c: (ix(s, j, c), j)),
                pl.BlockSpec((CHUNK, A_HG * A_DK), lambda s, j, c: (ix(s, j, c), k_blk + j)),
                pl.BlockSpec((CHUNK, nv * A_DV), lambda s, j, c: (ix(s, j, c), v_blk + j)),
                pl.BlockSpec((CHUNK, LANES), lambda s, j, c: (ix(s, j, c), j)),
                pl.BlockSpec((1, 1, nv, CHUNK), lambda s, j, c: (ix(s, j, c), j, 0, 0))]

    st_spec = pl.BlockSpec((1, nv, A_DK, A_DV), lambda s, j, c: (s, j, 0, 0))
    st_shape = jax.ShapeDtypeStruct((n_seq, A_V_HEADS, A_DK, A_DV), F32)
    any_spec = pl.BlockSpec(memory_space=pl.ANY)
    return pl.pallas_call(
        _gdn_kernel,
        grid=(n_seq, ng, nc),
        in_specs=stream_specs(fwd) + stream_specs(bwd) + [st_spec, st_spec, any_spec, any_spec],
        out_specs=[pl.BlockSpec((CHUNK, nv * A_DV), lambda s, j, c: (fwd(s, j, c), j)),
                   pl.BlockSpec((CHUNK, nv * A_DV), lambda s, j, c: (bwd(s, j, c), j)),
                   st_spec, st_spec],
        out_shape=[jax.ShapeDtypeStruct(buf_f.shape, F32), jax.ShapeDtypeStruct(buf_b.shape, F32),
                   st_shape, st_shape],
        scratch_shapes=[pltpu.VMEM((2, nv, A_DK, A_DV), F32)],
        input_output_aliases={12: 0, 13: 1},
        compiler_params=_cparams(("parallel", "parallel", "arbitrary")),
        name="gdn_scan",
    )(qkv, qkv, qkv, colf, rowf, qkv, qkv, qkv, colb, rowb, s0f, s0b, buf_f, buf_b)


GLA_LEVELS = (5, 4, 3, 2, 1, 0)
GLA_SAFE_EXP = 60.0


def _gla_level(i, j, lv, rev):
    sup = lv + 1
    base = jnp.left_shift(jnp.right_shift(i, sup), sup)
    ref_row = base + (1 << lv) - (0 if rev else 1)
    same = jnp.right_shift(i, sup) == jnp.right_shift(j, sup)
    i_late = jnp.bitwise_and(jnp.right_shift(i, lv), 1) == (0 if rev else 1)
    j_early = jnp.bitwise_and(jnp.right_shift(j, lv), 1) == (1 if rev else 0)
    return jnp.where(j == ref_row, 1.0, 0.0), same & i_late & j_early


def _rows_bcast(x, rows, span):
    zero = jnp.zeros((1, x.shape[1]), x.dtype)
    picked = [x[r:r + 1] if 0 <= r < x.shape[0] else zero for r in rows]
    return jnp.concatenate([jnp.broadcast_to(r, (span, x.shape[1])) for r in picked], axis=0)


def _gla_block16(i, j, rev):
    blk = jnp.right_shift(i, 4)
    ref_row = jnp.left_shift(blk + 1, 4) if rev else jnp.left_shift(blk, 4) - 1
    tri = (j >= i) if rev else (j <= i)
    return jnp.where(j == ref_row, 1.0, 0.0), tri & (blk == jnp.right_shift(j, 4))


def _gla_pair(q, k, bc, ref, key_side_nonpositive):
    ek = ref - bc
    if key_side_nonpositive:
        ek = jnp.minimum(ek, 0.0)
    return (q * jnp.exp(jnp.minimum(bc - ref, 0.0))).astype(BF16), (k * jnp.exp(ek)).astype(BF16)


def _gla_kernel(qf_ref, kf_ref, vf_ref, lf_ref, qb_ref, kb_ref, vb_ref, lb_ref, w2_ref, gb_ref,
                s0f_ref, s0b_ref, bf_buf, bb_buf, of_ref, ob_ref, sff_ref, sfb_ref, s_scr, a_scr):
    del bf_buf, bb_buf
    c = pl.program_id(1)

    @pl.when(c == 0)
    def _():
        s_scr[0] = s0f_ref[0]
        s_scr[1] = s0b_ref[0]

    streams = ((qf_ref, kf_ref, vf_ref, lf_ref, False), (qb_ref, kb_ref, vb_ref, lb_ref, True))
    heads = [slice(hd * D_DK, (hd + 1) * D_DK) for hd in range(D_HEADS)]
    units = [(d, hd) for d in range(2) for hd in range(D_HEADS)]
    ops = []
    for d, (q_ref, k_ref, v_ref, l_ref, rev) in enumerate(streams):
        tri, _, i, j = _tri_masks(rev)
        x = _dot(l_ref[...], w2_ref[d]) + gb_ref[d]
        lg = (jnp.minimum(x, 0.0) - jnp.log(1.0 + jnp.exp(-jnp.abs(x)))) * (1.0 / GATE_TAU)
        bc = _dot_exact_lhs(jnp.where(tri, 1.0, 0.0), lg)
        btot = bc[0:1] if rev else bc[CHUNK - 1:CHUNK]
        q, k = q_ref[...] * D_DK ** -0.5, k_ref[...]
        def level(lv, q=q, k=k, bc=bc, i=i, j=j, rev=rev):
            pick, mask = _gla_level(i, j, lv, rev)
            if (1 << lv) >= 8:
                span = 2 << lv
                ref = _rows_bcast(bc, [b + (1 << lv) - (0 if rev else 1) for b in range(0, CHUNK, span)], span)
            else:
                ref = _dot_exact_lhs(pick, bc)
            return (mask,) + _gla_pair(q, k, bc, ref, True)

        _, mask16 = _gla_block16(i, j, rev)
        ref16 = _rows_bcast(bc, [b + 16 if rev else b - 1 for b in range(0, CHUNK, 16)], 16)
        ops.append(dict(level=level, coarse=[level(lv) for lv in GLA_LEVELS[:2]],
                        block16=(mask16,) + _gla_pair(q, k, bc, ref16, False), risk=jnp.max(ref16 - bc),
                        diag=(i == j, q.astype(BF16), k.astype(BF16)),
                        qg=(q * jnp.exp(bc)).astype(BF16), kd=(k * jnp.exp(btot - bc)).astype(BF16),
                        etot=jnp.exp(btot), v=v_ref[...].astype(BF16)))

    def term(pair, hd):
        mask, qe, ke = pair
        return jnp.where(mask, _dot_nt(qe[:, heads[hd]], ke[:, heads[hd]]), 0.0)

    a_coarse = [term(ops[d]['coarse'][0], hd) + term(ops[d]['coarse'][1], hd) for d, hd in units]
    risk = jnp.maximum(ops[0]['risk'], ops[1]['risk'])

    @pl.when(risk <= GLA_SAFE_EXP)
    def _():
        for u, (d, hd) in enumerate(units):
            a_scr[u] = a_coarse[u] + term(ops[d]['block16'], hd)

    @pl.when(risk > GLA_SAFE_EXP)
    def _():
        fine = [[ops[d]['diag']] + [ops[d]['level'](lv) for lv in GLA_LEVELS[2:]] for d in range(2)]
        for u, (d, hd) in enumerate(units):
            a_scr[u] = a_coarse[u] + functools.reduce(lambda x, y: x + y, [term(pr, hd) for pr in fine[d]])

    a = [a_scr[u] for u in range(len(units))]
    st = [s_scr[d, hd] for d, hd in units]
    vs = [ops[d]['v'][:, hd * D_DV:(hd + 1) * D_DV] for d, hd in units]
    o = [_dot_nt(ops[d]['qg'][:, heads[hd]], st[u]) + _dot(a[u], vs[u]) for u, (d, hd) in enumerate(units)]
    st_new = [st[u] * ops[d]['etot'][:, heads[hd]] + _dot_tn(vs[u], ops[d]['kd'][:, heads[hd]])
              for u, (d, hd) in enumerate(units)]
    for u, (d, hd) in enumerate(units):
        (ob_ref if d else of_ref)[:, hd * D_DV:(hd + 1) * D_DV] = o[u]
        s_scr[d, hd] = st_new[u]

    @pl.when(c == pl.num_programs(1) - 1)
    def _():
        sff_ref[0] = s_scr[0]
        sfb_ref[0] = s_scr[1]


def _gla_scan(p, lr, w2, gb, s0f, s0b, buf_f, buf_b, n_seq, n_tok, row0):
    nc = n_tok // CHUNK
    cb0 = row0 // CHUNK
    fwd = lambda s, c: cb0 + s * nc + c
    bwd = lambda s, c: cb0 + s * nc + (nc - 1 - c)

    def stream_specs(ix):
        return [pl.BlockSpec((CHUNK, D_QKW), lambda s, c: (ix(s, c), 0)),
                pl.BlockSpec((CHUNK, D_QKW), lambda s, c: (ix(s, c), 1)),
                pl.BlockSpec((CHUNK, D_VW), lambda s, c: (ix(s, c), 2 * D_QKW // D_VW)),
                pl.BlockSpec((CHUNK, 2 * GATE_RANK), lambda s, c: (ix(s, c), 0))]

    st_spec = pl.BlockSpec((1, D_HEADS, D_DV, D_DK), lambda s, c: (s, 0, 0, 0))
    st_shape = jax.ShapeDtypeStruct((n_seq, D_HEADS, D_DV, D_DK), F32)
    any_spec = pl.BlockSpec(memory_space=pl.ANY)
    return pl.pallas_call(
        _gla_kernel,
        grid=(n_seq, nc),
        in_specs=stream_specs(fwd) + stream_specs(bwd) + [
            pl.BlockSpec((2, 2 * GATE_RANK, D_QKW), lambda s, c: (0, 0, 0)),
            pl.BlockSpec((2, 1, D_QKW), lambda s, c: (0, 0, 0)),
            st_spec, st_spec, any_spec, any_spec],
        out_specs=[pl.BlockSpec((CHUNK, D_VW), lambda s, c: (fwd(s, c), 0)),
                   pl.BlockSpec((CHUNK, D_VW), lambda s, c: (bwd(s, c), 0)),
                   st_spec, st_spec],
        out_shape=[jax.ShapeDtypeStruct(buf_f.shape, F32), jax.ShapeDtypeStruct(buf_b.shape, F32),
                   st_shape, st_shape],
        scratch_shapes=[pltpu.VMEM((2, D_HEADS, D_DV, D_DK), F32), pltpu.VMEM((2 * D_HEADS, CHUNK, CHUNK), F32)],
        input_output_aliases={12: 0, 13: 1},
        compiler_params=_cparams(("parallel", "arbitrary")),
        name="gla_scan",
    )(p, p, p, lr, p, p, p, lr, w2, gb, s0f, s0b, buf_f, buf_b)


def _gdn_prep_kernel(x_ref, prev_ref, next_ref, edge_ref, w_ref, o_ref, *, starts, ends):
    i, j = pl.program_id(0), pl.program_id(1)
    tm, cw = x_ref.shape
    r0 = i * tm
    is_start = functools.reduce(jnp.logical_or, [r0 == s for s in starts])
    is_end = functools.reduce(jnp.logical_or, [r0 + tm == e for e in ends])
    x = x_ref[...]
    prev_row = jnp.where(is_start, edge_ref[0, 7:8, :], prev_ref[7:8, :])
    next_row = jnp.where(is_end, 0.0, next_ref[0:1, :])
    rows = lax.broadcasted_iota(jnp.int32, (tm, 1), 0)
    xp = jnp.where(rows == 0, prev_row, pltpu.roll(x, 1, axis=0))
    xn = jnp.where(rows == tm - 1, next_row, pltpu.roll(x, tm - 1, axis=0))
    y = xp * w_ref[0:1, :] + x * w_ref[1:2, :] + xn * w_ref[2:3, :]
    y = y * _sigmoid(y)
    is_qk = j < 2 * A_QK // cw
    q_scale = jnp.where(j < A_QK // cw, A_DK ** -0.5, 1.0)
    for c0 in range(0, cw, A_DK):
        yh = y[:, c0:c0 + A_DK]
        nrm = yh * (lax.rsqrt(jnp.sum(yh * yh, axis=-1, keepdims=True) + EPS) * q_scale)
        o_ref[:, c0:c0 + A_DK] = jnp.where(is_qk, nrm, yh)


def _gdn_prep(lay, x, edge, conv_w):
    r, c = x.shape
    tm, cw = 512, 512
    starts = [row + s * n for b, n, row, _ in lay.groups for s in range(b)]
    ends = [row + (s + 1) * n for b, n, row, _ in lay.groups for s in range(b)]
    seq_of = lambda i: sum((i * tm >= s).astype(jnp.int32) for s in starts) - 1
    w8 = jnp.pad(conv_w.astype(F32), ((0, 8 - conv_w.shape[0]), (0, 0)))
    return pl.pallas_call(
        functools.partial(_gdn_prep_kernel, starts=starts, ends=ends),
        grid=(r // tm, c // cw),
        in_specs=[pl.BlockSpec((tm, cw), lambda i, j: (i, j)),
                  pl.BlockSpec((8, cw), lambda i, j: (jnp.maximum(i * (tm // 8) - 1, 0), j)),
                  pl.BlockSpec((8, cw), lambda i, j: (jnp.minimum((i + 1) * (tm // 8), r // 8 - 1), j)),
                  pl.BlockSpec((1, 8, cw), lambda i, j: (seq_of(i), 0, j)),
                  pl.BlockSpec((8, cw), lambda i, j: (0, j))],
        out_specs=pl.BlockSpec((tm, cw), lambda i, j: (i, j)),
        out_shape=jax.ShapeDtypeStruct((r, c), F32),
        compiler_params=_cparams(("parallel", "parallel")),
        name="gdn_prep",
    )(x, x, x, edge, w8)


def _proj_prep_kernel(*refs, head_norm, rope, transpose):
    x_ref, g_ref, w_ref = refs[:3]
    rest = list(refs[3:])
    hg_ref = rest.pop(0) if head_norm else None
    cos_ref, sin_ref = (rest.pop(0), rest.pop(0)) if rope else (None, None)
    o_ref, = rest
    x = x_ref[...]
    xn = (x * lax.rsqrt(jnp.mean(x * x, axis=-1, keepdims=True) + EPS) * g_ref[...]).astype(BF16)
    acc = jnp.dot(xn, w_ref[...], preferred_element_type=F32)
    lo = lax.broadcasted_iota(jnp.int32, (1, LANES), 1) < C_DQK
    for c0 in range(0, acc.shape[1], LANES):
        y = acc[:, c0:c0 + LANES]
        if head_norm == "head":
            y = y * lax.rsqrt(jnp.mean(y * y, axis=-1, keepdims=True) + EPS) * hg_ref[...]
        elif head_norm == "maps":
            sq = y * y
            s_lo = jnp.sum(jnp.where(lo, sq, 0.0), axis=-1, keepdims=True)
            s_hi = jnp.sum(jnp.where(lo, 0.0, sq), axis=-1, keepdims=True)
            inv = jnp.where(lo, lax.rsqrt(s_lo * (1.0 / C_DQK) + EPS), lax.rsqrt(s_hi * (1.0 / C_DQK) + EPS))
            y = y * inv * hg_ref[...]
        if rope:
            y = y * cos_ref[...] + pltpu.roll(y, LANES // 2, axis=1) * sin_ref[...]
        if transpose:
            o_ref[c0:c0 + LANES, :] = y.T.astype(BF16)
        else:
            o_ref[:, c0:c0 + LANES] = y.astype(BF16)


def _pos_block(lay, tm):
    def pos(i):
        r = i * tm
        out = 0
        for b, n, row, _ in lay.groups:
            inside = (r >= row) & (r < row + b * n)
            out = out + jnp.where(inside, ((r - row) % n) // tm, 0)
        return out
    return pos


def _proj_prep(lay, x, g, w, head_gain=None, head_norm=None, rope=None, transpose=False):
    m, k = x.shape
    n = w.shape[1]
    tm = min(512, min(nt for _, nt, _, _ in lay.groups))
    tn = min(n, 1024)
    pos = _pos_block(lay, tm)
    fixed = lambda j, i: (0, 0)
    ins = [x, g.reshape(1, k).astype(F32), w.astype(BF16)]
    specs = [pl.BlockSpec((tm, k), lambda j, i: (i, 0)), pl.BlockSpec((1, k), fixed),
             pl.BlockSpec((k, tn), lambda j, i: (0, j))]
    if head_norm:
        ins.append(head_gain.reshape(1, LANES).astype(F32))
        specs.append(pl.BlockSpec((1, LANES), fixed))
    if rope:
        ins += list(rope)
        specs += [pl.BlockSpec((tm, LANES), lambda j, i: (pos(i), 0))] * 2
    if transpose:
        out_spec, out_shape = pl.BlockSpec((tn, tm), lambda j, i: (j, i)), (n, m)
    else:
        out_spec, out_shape = pl.BlockSpec((tm, tn), lambda j, i: (i, j)), (m, n)
    return pl.pallas_call(
        functools.partial(_proj_prep_kernel, head_norm=head_norm, rope=bool(rope), transpose=transpose),
        grid=(n // tn, m // tm),
        in_specs=specs, out_specs=out_spec,
        out_shape=jax.ShapeDtypeStruct(out_shape, BF16),
        compiler_params=_cparams(("parallel", "parallel")),
        name="proj_prep",
    )(*ins)


class _Layout:
    def __init__(self, shapes):
        self.groups = []
        row, seq = 0, 0
        for b, n in shapes:
            self.groups.append((b, n, row, seq))
            row += b * n
            seq += b
        self.rows, self.n_seq = row, seq

    def to_seqs(self, main, meta):
        out = []
        c = main.shape[-1]
        for b, n, row, seq in self.groups:
            m = meta[seq * N_META:(seq + b) * N_META].reshape(b, N_META, c)
            x = main[row:row + b * n].reshape(b, n, c)
            out.append(jnp.concatenate([m, x], axis=1))
        return out

    def from_seqs(self, seqs):
        main = jnp.concatenate([s[:, N_META:].reshape(-1, s.shape[-1]) for s in seqs], axis=0)
        meta = jnp.concatenate([s[:, :N_META].reshape(-1, s.shape[-1]) for s in seqs], axis=0)
        return main, meta


def _meta_chunk(meta):
    c = meta.shape[-1]
    m = meta.reshape(-1, N_META, c)
    return jnp.pad(m, ((0, 0), (CHUNK - N_META, 0), (0, 0))).reshape(-1, c)


def _meta_unchunk(x):
    c = x.shape[-1]
    return x.reshape(-1, CHUNK, c)[:, CHUNK - N_META:].reshape(-1, c)


def _map_seqs(lay, fn, main, meta):
    return lay.from_seqs([fn(s) for s in lay.to_seqs(main, meta)])


def _run_scan(scan, lay, main_args, meta_args, state_shape, width):
    s_all = lay.n_seq
    zeros = jnp.zeros((s_all,) + state_shape, F32)
    mrows = s_all * CHUNK
    mbuf = jnp.zeros((mrows, width), F32)
    _, _, s_meta, _ = scan(*meta_args, zeros, zeros, mbuf, mbuf, s_all, CHUNK, 0)
    of = jnp.zeros((lay.rows, width), F32)
    ob = jnp.zeros((lay.rows, width), F32)
    finals = []
    for b, n, row, seq in lay.groups:
        of, ob, _, sfb = scan(*main_args, s_meta[seq:seq + b], zeros[seq:seq + b], of, ob, b, n, row)
        finals.append(sfb)
    mof, mob, _, _ = scan(*meta_args, zeros, jnp.concatenate(finals, axis=0), mbuf, mbuf, s_all, CHUNK, 0)
    return (of, ob), (_meta_unchunk(mof), _meta_unchunk(mob))


def _gdn_layer(lay, h, hm, norm_g, w_in, conv_w, a_log, dt_bias, out_g, w_out):
    w_qkv, w_z, w_ab = (w_in[:, :A_CONV_CH], w_in[:, A_CONV_CH:A_CONV_CH + A_VW], w_in[:, A_CONV_CH + A_VW:])
    x, xm = _norm_proj(h, norm_g, w_qkv), _norm_proj(hm, norm_g, w_qkv)
    z, zm = _norm_proj(h, norm_g, w_z), _norm_proj(hm, norm_g, w_z)
    ab, abm = _norm_proj(h, norm_g, w_ab), _norm_proj(hm, norm_g, w_ab)
    s_all = lay.n_seq
    xm3 = xm.reshape(s_all, N_META, A_CONV_CH)
    qkv = _gdn_prep(lay, x, xm3[:, N_META - 8:], conv_w)

    def prep(x):
        prev = jnp.pad(x[:, :-1], ((0, 0), (1, 0), (0, 0)))
        nxt = jnp.pad(x[:, 1:], ((0, 0), (0, 1), (0, 0)))
        y = jax.nn.silu(prev * conv_w[0] + x * conv_w[1] + nxt * conv_w[2])
        b, l, _ = y.shape
        qk = y[..., :2 * A_QK].reshape(b, l, 2 * A_QK_HEADS, A_DK)
        qk = qk * lax.rsqrt(jnp.sum(qk * qk, axis=-1, keepdims=True) + EPS)
        qk = qk.reshape(b, l, 2, A_QK) * jnp.array([A_DK ** -0.5, 1.0], F32)[:, None]
        return jnp.concatenate([qk.reshape(b, l, 2 * A_QK), y[..., 2 * A_QK:]], axis=-1)

    first = x[np.array([row + s * n for b, n, row, _ in lay.groups for s in range(b)])]
    qkvm = prep(jnp.concatenate([xm3, first[:, None]], axis=1))[:, :N_META].reshape(-1, A_CONV_CH)

    def gates(x):
        x = x.reshape(-1, 2, 2, A_V_HEADS)
        g = -jnp.exp(a_log.astype(F32)) * jax.nn.softplus(x[:, 0] + dt_bias.astype(F32))
        return g, jax.nn.sigmoid(x[:, 1])

    ng, nv = A_QK_HEADS // A_HG, 2 * A_HG

    def pack(g, beta, d):
        r = g.shape[0]
        gd, bd = g[:, d].reshape(r, ng, nv), beta[:, d].reshape(r, ng, nv)
        col = jnp.concatenate([gd, bd, jnp.zeros((r, ng, LANES - 2 * nv), F32)], axis=-1).reshape(r, ng * LANES)
        row = gd.reshape(r // CHUNK, CHUNK, ng, nv).transpose(0, 2, 3, 1)
        return col, row

    def scan_args(qkv_rows, g, beta):
        colf, rowf = pack(g, beta, 0)
        colb, rowb = pack(g, beta, 1)
        return (qkv_rows, colf, colb, rowf, rowb)

    g, beta = gates(ab)
    gm, betam = gates(abm)
    pad_gate = lambda t: _meta_chunk(t.reshape(-1, 2 * A_V_HEADS)).reshape(-1, 2, A_V_HEADS)
    (of, ob), (mof, mob) = _run_scan(
        _gdn_scan, lay, scan_args(qkv, g, beta), scan_args(_meta_chunk(qkvm), pad_gate(gm), pad_gate(betam)),
        (A_V_HEADS, A_DK, A_DV), A_VW)
    gain = jnp.tile(out_g.astype(F32), A_V_HEADS)
    return (_out_proj([of, ob], z, gain, w_out, h, A_DV), _out_proj([mof, mob], zm, gain, w_out, hm, A_DV))


def _rope_tables(n_tok):
    rows = n_tok // GRID_W
    row = jnp.repeat(jnp.arange(rows, dtype=F32), GRID_W)
    col = (jnp.arange(rows * GRID_W) % GRID_W).astype(F32)
    inv = ROPE_THETA ** (-jnp.arange(0, ROPE_AXIS, 2, dtype=F32) / ROPE_AXIS)
    ang = jnp.concatenate([row[:, None] * inv, col[:, None] * inv], axis=-1)
    return jnp.cos(ang), jnp.sin(ang)


def _pad_meta(x, n_seq):
    return jnp.pad(x.reshape(n_seq, N_META, -1), ((0, 0), (0, LANES - N_META), (0, 0)))


def _gqa_layer(lay, h, hm, norm_g, w_in, q_g, k_g, w_out):
    k_end, kv_end = B_QW + B_KVW, B_QW + 2 * B_KVW
    z = _norm_proj(h, norm_g, w_in[:, kv_end:])
    pm = _norm_proj(hm, norm_g, w_in)

    def norm_heads(x, g):
        r = x.shape[0]
        x = x.reshape(r, -1, B_DH)
        return x * lax.rsqrt(jnp.mean(x * x, axis=-1, keepdims=True) + EPS) * g.astype(F32)

    scale = B_DH ** -0.5 * LOG2E
    cos, sin = _rope_tables(max(n for _, n, _, _ in lay.groups))
    rope = (jnp.concatenate([cos, cos], axis=-1), jnp.concatenate([-sin, sin], axis=-1))
    qt = _proj_prep(lay, h, norm_g, w_in[:, :B_QW], q_g.astype(F32) * scale, "head", rope, transpose=True)
    k = _proj_prep(lay, h, norm_g, w_in[:, B_QW:k_end], k_g, "head", rope)
    vt = _proj_prep(lay, h, norm_g, w_in[:, k_end:kv_end], transpose=True)
    s_all = lay.n_seq
    qm = (norm_heads(pm[:, :B_QW], q_g) * scale).reshape(-1, B_QW).astype(BF16)
    km = norm_heads(pm[:, B_QW:B_QW + B_KVW], k_g).reshape(-1, B_KVW).astype(BF16)
    vm = pm[:, B_QW + B_KVW:B_QW + 2 * B_KVW].astype(BF16)
    zm = pm[:, B_QW + 2 * B_KVW:]
    qmt = _pad_meta(qm, s_all).reshape(s_all * LANES, B_QW).T
    kmp, vmt = _pad_meta(km, s_all), _pad_meta(vm, s_all).transpose(0, 2, 1)

    o = jnp.zeros((lay.rows, B_QW), F32)
    om = jnp.zeros((s_all * LANES, B_QW), F32)
    for b, n, row, seq in lay.groups:
        tq, tk = min(GQA_TQ, n), min(GQA_TK, n)
        o = _gqa_attention(qt, k, vt, kmp[seq:seq + b], vmt[seq:seq + b], o, b, n, n, row, row, tq, tk)
        om = _gqa_attention(qmt, k, vt, kmp[seq:seq + b], vmt[seq:seq + b], om, b, LANES, n,
                            seq * LANES, row, LANES, tk)
    om = om.reshape(s_all, LANES, B_QW)[:, :N_META].reshape(-1, B_QW)
    ones = jnp.ones((B_QW,), F32)
    return _out_proj([o], z, ones, w_out, h, 0), _out_proj([om], zm, ones, w_out, hm, 0)


def _t5_bucket_np(rel):
    nb = REL_BUCKETS // 2
    max_exact = nb // 2
    n = np.abs(rel)
    steps = nb - max_exact
    large = np.full(n.shape, max_exact, np.int64)
    for kstep in range(1, steps + 1):
        large += (n.astype(np.float64) ** steps >= (max_exact ** steps) * float(REL_MAX_DIST // max_exact) ** kstep)
    large = np.minimum(large, nb - 1)
    return np.where(rel > 0, nb, 0) + np.where(n < max_exact, n, large)


def _diff_layer(lay, h, hm, norm_g, w_in, q_g, k_g, lam, sub_g, w_out, rel_bias, layer_idx):
    n_qkv = 2 * C_QKW + C_VW
    z = _norm_proj(h, norm_g, w_in[:, n_qkv:])
    pm = _norm_proj(hm, norm_g, w_in)

    def norm_maps(x, g):
        r = x.shape[0]
        x = x.reshape(r, -1, C_DQK)
        return (x * lax.rsqrt(jnp.mean(x * x, axis=-1, keepdims=True) + EPS) * g.astype(F32)).reshape(r, -1)

    s_all = lay.n_seq
    scale = C_DQK ** -0.5 * LOG2E
    qt = _proj_prep(lay, h, norm_g, w_in[:, :C_QKW], jnp.tile(q_g.astype(F32), 2) * scale, "maps", transpose=True)
    k = _proj_prep(lay, h, norm_g, w_in[:, C_QKW:2 * C_QKW], jnp.tile(k_g.astype(F32), 2), "maps")
    vt = _proj_prep(lay, h, norm_g, w_in[:, 2 * C_QKW:n_qkv], transpose=True)
    qm = (norm_maps(pm[:, :C_QKW], q_g) * scale).astype(BF16)
    km = norm_maps(pm[:, C_QKW:2 * C_QKW], k_g).astype(BF16)
    vm = pm[:, 2 * C_QKW:n_qkv].astype(BF16)
    zm = pm[:, n_qkv:]
    qmt = _pad_meta(qm, s_all).reshape(s_all * LANES, C_QKW).T
    kmp, vmt = _pad_meta(km, s_all), _pad_meta(vm, s_all).transpose(0, 2, 1)

    lam = lam.astype(F32)
    lam_init = 0.8 - 0.6 * math.exp(-0.3 * layer_idx)
    lam_full = jnp.exp(jnp.sum(lam[0] * lam[1])) - jnp.exp(jnp.sum(lam[2] * lam[3])) + lam_init
    table = rel_bias.astype(F32).T * LOG2E
    nb = REL_BUCKETS // 2

    def look_all(rels):
        flat = table[:, np.concatenate([_t5_bucket_np(r).reshape(-1) for r in rels])]
        ends = np.cumsum([r.size for r in rels])
        return [flat[:, e - r.size:e].reshape((C_HEADS,) + r.shape) for r, e in zip(rels, ends)]

    cst = jnp.zeros((C_HEADS, 8, LANES), F32)
    cst = cst.at[:, 0].set(table[:, nb - 1:nb]).at[:, 1].set(table[:, 2 * nb - 1:2 * nb]).at[:, 2].set(lam_full)
    ar = np.arange(LANES)
    key_pad = jnp.asarray(np.where(ar < N_META, 0.0, NEG), F32)[:, None]
    offsets = np.arange(-(BIAS_TILES // 2), BIAS_TILES // 2 + 1)
    mq = np.minimum(ar, N_META - 1)
    tiles = [min(DIFF_TQ, n) for _, n, _, _ in lay.groups]
    tt, tt_m, mm, *nears = look_all(
        [offsets[:, None, None] * LANES + ar[:, None] - ar[None, :],
         offsets[:, None, None] * LANES + N_META + ar[:, None] - mq[None, :],
         ar[:, None] - mq[None, :]]
        + [ar[:N_META, None] - N_META - np.arange(t)[None, :] for t in tiles])
    meta_rows = (ar < N_META)[:, None]
    bm_m = (jnp.where(meta_rows, mm, 0.0) + key_pad)[:, None]

    o = jnp.zeros((lay.rows, C_VW), F32)
    om = jnp.zeros((s_all * LANES, C_VW), F32)
    for (b, n, row, seq), near in zip(lay.groups, nears):
        t, tk = min(DIFF_TQ, n), min(DIFF_TK, n)
        far = jnp.broadcast_to(table[:, nb - 1][:, None, None], near.shape)
        pad_rows = jnp.full((C_HEADS, 2, LANES - N_META, t), NEG, F32)
        bm = jnp.concatenate([jnp.stack([near, far], axis=1), pad_rows], axis=2)
        o = _diff_attention(qt, k, vt, kmp[seq:seq + b], vmt[seq:seq + b], tt, bm, cst, o,
                            b, n, n, row, row, t, tk, False)
        om = _diff_attention(qmt, k, vt, kmp[seq:seq + b], vmt[seq:seq + b], tt_m, bm_m, cst, om,
                             b, LANES, n, seq * LANES, row, LANES, tk, True)
    om = om.reshape(s_all, LANES, C_VW)[:, :N_META].reshape(-1, C_VW)
    gain = jnp.tile(sub_g.astype(F32), C_HEADS) * (1.0 - lam_init)
    return _out_proj([o], z, gain, w_out, h, C_DV), _out_proj([om], zm, gain, w_out, hm, C_DV)


def _gla_layer(lay, h, hm, norm_g, w_in, gate_w2, gate_b, out_g, w_out):
    n_qkv = 2 * D_QKW + D_VW
    w_qkv, w_z, w_lr = w_in[:, :n_qkv], w_in[:, n_qkv:n_qkv + D_VW], w_in[:, n_qkv + D_VW:]
    p, pm = _norm_proj(h, norm_g, w_qkv), _norm_proj(hm, norm_g, w_qkv)
    z, zm = _norm_proj(h, norm_g, w_z), _norm_proj(hm, norm_g, w_z)
    lr, lrm = _norm_proj(h, norm_g, w_lr), _norm_proj(hm, norm_g, w_lr)
    zero = jnp.zeros((GATE_RANK, D_QKW), F32)
    w2 = jnp.stack([jnp.concatenate([gate_w2[0].astype(F32), zero], axis=0),
                    jnp.concatenate([zero, gate_w2[1].astype(F32)], axis=0)])
    gb = gate_b.astype(F32).reshape(2, 1, D_QKW)
    (of, ob), (mof, mob) = _run_scan(
        _gla_scan, lay, (p, lr, w2, gb), (_meta_chunk(pm), _meta_chunk(lrm), w2, gb), (D_HEADS, D_DV, D_DK), D_VW)
    gain = jnp.tile(out_g.astype(F32), D_HEADS)
    return _out_proj([of, ob], z, gain, w_out, h, D_DV), _out_proj([mof, mob], zm, gain, w_out, hm, D_DV)


def _trunk(xs, meta_tokens, rel_bias, a, b, c, d, depth):
    lay = _Layout([(x.shape[0], x.shape[1]) for x in xs])
    h = jnp.concatenate([x.reshape(-1, D_MODEL) for x in xs], axis=0)
    hm = jnp.tile(meta_tokens.astype(F32), (lay.n_seq, 1))
    for i in range(depth):
        m, j = i % 4, i // 4
        if m == 0:
            h, hm = _gdn_layer(lay, h, hm, *(t[j] for t in a))
        elif m == 1:
            h, hm = _gqa_layer(lay, h, hm, *(t[j] for t in b))
        elif m == 2:
            h, hm = _diff_layer(lay, h, hm, *(t[j] for t in c), rel_bias, i)
        else:
            h, hm = _gla_layer(lay, h, hm, *(t[j] for t in d))
    return [h[row:row + bsz * n].reshape(bsz, n, D_MODEL) for bsz, n, row, _ in lay.groups]


def kernel(x_prompt, x_sample, meta_tokens, rel_bias, a_norm, a_w_in, a_conv, a_a_log, a_dt_bias, a_out_norm, a_w_out, b_norm, b_w_in, b_q_norm, b_k_norm, b_w_out, c_norm, c_w_in, c_q_norm, c_k_norm, c_lambda, c_sub_norm, c_w_out, d_norm, d_w_in, d_gate_w2, d_gate_b, d_out_norm, d_w_out):
    y_prompt, y_sample = _trunk(
        [x_prompt, x_sample], meta_tokens, rel_bias,
        (a_norm, a_w_in, a_conv, a_a_log, a_dt_bias, a_out_norm, a_w_out),
        (b_norm, b_w_in, b_q_norm, b_k_norm, b_w_out),
        (c_norm, c_w_in, c_q_norm, c_k_norm, c_lambda, c_sub_norm, c_w_out),
        (d_norm, d_w_in, d_gate_w2, d_gate_b, d_out_norm, d_w_out), 4)
    return (y_prompt, y_sample)
```

```python
import functools
import math

import numpy as np
import jax
import jax.numpy as jnp
from jax import lax
from jax.experimental import pallas as pl
from jax.experimental.pallas import tpu as pltpu

F32 = jnp.float32
BF16 = jnp.bfloat16

D_MODEL = 1024
N_META = 16
CHUNK = 64
GRID_W = 64
EPS = 1e-6
LANES = 128
VMEM_LIMIT = 56 * 1024 * 1024
NEG = -1e30
LOG2E = math.log2(math.e)
STRIP = 1024
ONES_ROWS = 16
PROJ_TN_MAX = 2048
GQA_TQ, GQA_TK = 1024, 2048
DIFF_TQ, DIFF_TK = 2048, 2048
BIAS_TILES = 5

A_QK_HEADS, A_V_HEADS, A_DK, A_DV = 8, 16, 128, 128
A_QK, A_VW = A_QK_HEADS * A_DK, A_V_HEADS * A_DV
A_CONV_CH = 2 * A_QK + A_VW
A_HG = 8
B_HEADS, B_KV_HEADS, B_DH = 8, 2, 128
B_GROUP = B_HEADS // B_KV_HEADS
B_QW, B_KVW = B_HEADS * B_DH, B_KV_HEADS * B_DH
ROPE_AXIS = B_DH // 2
ROPE_THETA = 10000.0
C_HEADS, C_DQK, C_DV = 8, 64, 128
C_QKW, C_VW = C_HEADS * 2 * C_DQK, C_HEADS * C_DV
REL_BUCKETS, REL_MAX_DIST = 32, 128
D_HEADS, D_DK, D_DV = 4, 128, 256
D_QKW, D_VW = D_HEADS * D_DK, D_HEADS * D_DV
GATE_RANK, GATE_TAU = 16, 16.0


def _cparams(sem):
    return pltpu.CompilerParams(dimension_semantics=sem, vmem_limit_bytes=VMEM_LIMIT)


def _sigmoid(x):
    return 1.0 / (1.0 + jnp.exp(-x))


def _dot(a, b):
    return jnp.dot(a.astype(BF16), b.astype(BF16), preferred_element_type=F32)


def _dot_nt(a, b):
    return lax.dot_general(a.astype(BF16), b.astype(BF16), (((1,), (1,)), ((), ())),
                           preferred_element_type=F32)


def _dot_tn(a, b):
    return lax.dot_general(a.astype(BF16), b.astype(BF16), (((0,), (0,)), ((), ())),
                           preferred_element_type=F32)


def _split3(x):
    x1 = x.astype(BF16)
    r1 = x - x1.astype(F32)
    x2 = r1.astype(BF16)
    x3 = (r1 - x2.astype(F32)).astype(BF16)
    return x1, x2, x3


def _dot_exact_lhs(m01, x):
    m = m01.astype(BF16)
    x1, x2, x3 = _split3(x)
    return (jnp.dot(m, x1, preferred_element_type=F32) + jnp.dot(m, x2, preferred_element_type=F32)
            + jnp.dot(m, x3, preferred_element_type=F32))


def _dot_exact_rhs(x, m01):
    m = m01.astype(BF16)
    x1, x2, x3 = _split3(x)
    return (jnp.dot(x1, m, preferred_element_type=F32) + jnp.dot(x2, m, preferred_element_type=F32)
            + jnp.dot(x3, m, preferred_element_type=F32))


def _norm_proj_kernel(x_ref, g_ref, w_ref, o_ref):
    x = x_ref[...]
    ms = jnp.mean(x * x, axis=-1, keepdims=True)
    xn = (x * lax.rsqrt(ms + EPS) * g_ref[...]).astype(BF16)
    o_ref[...] = jnp.dot(xn, w_ref[...], preferred_element_type=F32)


def _norm_proj(x, g, w):
    m, k = x.shape
    n = w.shape[1]
    tm = 512 if m % 512 == 0 else m
    tn = max([t for t in range(256, PROJ_TN_MAX + 1, 256) if n % t == 0], default=n)
    return pl.pallas_call(
        _norm_proj_kernel,
        grid=(n // tn, m // tm),
        in_specs=[pl.BlockSpec((tm, k), lambda j, i: (i, 0)),
                  pl.BlockSpec((1, k), lambda j, i: (0, 0)),
                  pl.BlockSpec((k, tn), lambda j, i: (0, j))],
        out_specs=pl.BlockSpec((tm, tn), lambda j, i: (i, j)),
        out_shape=jax.ShapeDtypeStruct((m, n), F32),
        compiler_params=_cparams(("parallel", "parallel")),
        name="norm_proj",
    )(x, g.reshape(1, k).astype(F32), w.astype(BF16))


def _out_proj_kernel(*refs, n_o, hw):
    o_refs = refs[:n_o]
    z_ref, g_ref, w_ref, h_ref, out_ref = refs[n_o:]
    k = z_ref.shape[1]
    cw = hw if hw else 256
    acc = h_ref[...]
    for c0 in range(0, k, cw):
        o = o_refs[0][:, c0:c0 + cw]
        for r in o_refs[1:]:
            o = o + r[:, c0:c0 + cw]
        if hw:
            ms = jnp.mean(o * o, axis=-1, keepdims=True)
            o = o * lax.rsqrt(ms + EPS) * g_ref[:, c0:c0 + cw]
        z = z_ref[:, c0:c0 + cw]
        gated = o * (z * _sigmoid(z))
        acc = acc + jnp.dot(gated.astype(BF16), w_ref[c0:c0 + cw, :], preferred_element_type=F32)
    out_ref[...] = acc


def _out_proj(os, z, g, w, h, hw):
    m, k = z.shape
    n = w.shape[1]
    tm = (512 if k <= 1024 else 256) if m % 512 == 0 else m
    row = lambda i: (i, 0)
    fixed = lambda i: (0, 0)
    return pl.pallas_call(
        functools.partial(_out_proj_kernel, n_o=len(os), hw=hw),
        grid=(m // tm,),
        in_specs=[pl.BlockSpec((tm, k), row) for _ in os] + [
            pl.BlockSpec((tm, k), row), pl.BlockSpec((1, k), fixed),
            pl.BlockSpec((k, n), fixed), pl.BlockSpec((tm, n), row)],
        out_specs=pl.BlockSpec((tm, n), row),
        out_shape=jax.ShapeDtypeStruct((m, n), F32),
        compiler_params=_cparams(("parallel",)),
        name="out_proj",
    )(*os, z, g.reshape(1, k).astype(F32), w.astype(BF16), h)


def _strip_update(s_t, c, v_ext, m_scr, l_scr, acc_scr, cols):
    p, alpha = _strip_softmax(s_t, c, m_scr, cols)
    _strip_pv(p, alpha, v_ext, l_scr, acc_scr, cols)


def _strip_softmax(s_t, c, m_scr, cols):
    m_prev = m_scr[:, cols]
    m_new = jnp.maximum(m_prev, jnp.max(s_t, axis=0, keepdims=True) + c)
    m_scr[:, cols] = m_new
    return jnp.exp2(s_t - (m_new - c)).astype(BF16), jnp.exp2(m_prev - m_new)


def _strip_pv(p, alpha, v_ext, l_scr, acc_scr, cols):
    dv = acc_scr.shape[0]
    pv = jnp.dot(v_ext, p, preferred_element_type=F32)
    acc_scr[:, cols] = alpha * acc_scr[:, cols] + pv[:dv]
    l_scr[:, cols] = alpha * l_scr[:, cols] + pv[dv:dv + 1]


def _run_strips(score_fns, cols_list, c, v_ext, m_scr, l_scr, acc_scr):
    s_next = score_fns[0]()
    pending = None
    for n, cols in enumerate(cols_list):
        s_t, s_next = s_next, (score_fns[n + 1]() if n + 1 < len(score_fns) else None)
        if pending is not None:
            _strip_pv(pending[0], pending[1], v_ext, l_scr, acc_scr, pending[2])
        p, alpha = _strip_softmax(s_t, c, m_scr, cols)
        pending = (p, alpha, cols)
    _strip_pv(pending[0], pending[1], v_ext, l_scr, acc_scr, pending[2])


def _attn_init(m_scr, l_scr, acc_scr):
    m_scr[...] = jnp.full(m_scr.shape, NEG, F32)
    l_scr[...] = jnp.zeros(l_scr.shape, F32)
    acc_scr[...] = jnp.zeros(acc_scr.shape, F32)


def _with_ones(v_t):
    return jnp.concatenate([v_t, jnp.ones((ONES_ROWS, v_t.shape[1]), BF16)], axis=0)


def _strips(width):
    w = min(STRIP, width)
    return [slice(c0, c0 + w) for c0 in range(0, width, w)]


def _gqa_kernel(qt_ref, k_ref, vt_ref, km_ref, vmt_ref, buf_ref, o_ref, m_scr, l_scr, acc_scr):
    del buf_ref
    ki = pl.program_id(3)
    tq = qt_ref.shape[1]
    heads = [slice(g * B_DH, (g + 1) * B_DH) for g in range(B_GROUP)]

    @pl.when(ki == 0)
    def _():
        _attn_init(m_scr, l_scr, acc_scr)
        key = lax.broadcasted_iota(jnp.int32, (LANES, 1), 0)
        pad = jnp.where(key < N_META, 0.0, NEG)
        vm_ext = _with_ones(vmt_ref[0])
        for g, rows in enumerate(heads):
            for cols in _strips(tq):
                s_t = jnp.dot(km_ref[0], qt_ref[rows, cols], preferred_element_type=F32) + pad
                _strip_update(s_t, 0.0, vm_ext, m_scr, l_scr, acc_scr,
                              slice(g * tq + cols.start, g * tq + cols.stop))

    k = k_ref[...]
    v_ext = _with_ones(vt_ref[...])
    work = [(g, rows, cols) for g, rows in enumerate(heads) for cols in _strips(tq)]
    scr_cols = [slice(g * tq + cols.start, g * tq + cols.stop) for g, _, cols in work]
    _run_strips([functools.partial(lambda rows, cols: jnp.dot(k, qt_ref[rows, cols], preferred_element_type=F32),
                                   rows, cols) for _, rows, cols in work],
                scr_cols, 0.0, v_ext, m_scr, l_scr, acc_scr)

    @pl.when(ki == pl.num_programs(3) - 1)
    def _():
        for g, rows in enumerate(heads):
            cols = slice(g * tq, (g + 1) * tq)
            o_ref[:, rows] = (acc_scr[:, cols] / l_scr[:, cols]).T


def _gqa_attention(qt, k, vt, km, vmt, buf, n_seq, n_q, n_k, q_col0, k_row0, tq, tk):
    nq, nk = n_q // tq, n_k // tk
    qb0, kb0 = q_col0 // tq, k_row0 // tk
    r = B_GROUP * tq
    gw = B_GROUP * B_DH
    return pl.pallas_call(
        _gqa_kernel,
        grid=(n_seq, B_KV_HEADS, nq, nk),
        in_specs=[pl.BlockSpec((gw, tq), lambda s, h, i, j: (h, qb0 + s * nq + i)),
                  pl.BlockSpec((tk, B_DH), lambda s, h, i, j: (kb0 + s * nk + j, h)),
                  pl.BlockSpec((B_DH, tk), lambda s, h, i, j: (h, kb0 + s * nk + j)),
                  pl.BlockSpec((1, LANES, B_DH), lambda s, h, i, j: (s, 0, h)),
                  pl.BlockSpec((1, B_DH, LANES), lambda s, h, i, j: (s, h, 0)),
                  pl.BlockSpec(memory_space=pl.ANY)],
        out_specs=pl.BlockSpec((tq, gw), lambda s, h, i, j: (qb0 + s * nq + i, h)),
        out_shape=jax.ShapeDtypeStruct(buf.shape, F32),
        scratch_shapes=[pltpu.VMEM((1, r), F32), pltpu.VMEM((1, r), F32), pltpu.VMEM((B_DH, r), F32)],
        input_output_aliases={5: 0},
        compiler_params=_cparams(("parallel", "parallel", "parallel", "arbitrary")),
        name="gqa_attention",
    )(qt, k, vt, km, vmt, buf)


def _diff_kernel(qt_ref, k_ref, vt_ref, km_ref, vmt_ref, tt_ref, bm_ref, cst_ref, buf_ref, o_ref,
                 m_scr, l_scr, acc_scr, *, meta_q):
    del buf_ref
    qi, ki = pl.program_id(2), pl.program_id(3)
    tq, tk = qt_ref.shape[1], k_ref.shape[0]
    nt = tk // LANES
    qt = qt_ref[...]
    dim = lax.broadcasted_iota(jnp.int32, qt.shape, 0)
    zero = jnp.zeros_like(qt)
    maps = (jnp.where(dim < C_DQK, qt, zero), jnp.where(dim >= C_DQK, qt, zero))
    c_left = cst_ref[0, 0:1, 0:1]
    c_right = cst_ref[0, 1:2, 0:1]
    lam = cst_ref[0, 2:3, 0:1]
    strips = [(mp, cols) for mp in range(2) for cols in _strips(tq)]
    scr = lambda mp, cols: slice(mp * tq + cols.start, mp * tq + cols.stop)

    @pl.when(ki == 0)
    def _():
        _attn_init(m_scr, l_scr, acc_scr)
        vm_ext = _with_ones(vmt_ref[0])
        for mp, cols in strips:
            s_t = jnp.dot(km_ref[0], maps[mp][:, cols], preferred_element_type=F32) + bm_ref[0, 0, :, cols]
            _strip_update(s_t, 0.0, vm_ext, m_scr, l_scr, acc_scr, scr(mp, cols))

    k = k_ref[...]
    v_ext = _with_ones(vt_ref[...])

    ratio = max(tq // tk, 1)
    d = ki - qi * ratio
    near = (0, 0) if meta_q else (-1, ratio)
    mid = BIAS_TILES // 2

    def strip_bias(cols):
        q_tiles = range(cols.start // LANES, cols.stop // LANES)
        if meta_q:
            rows = [[tt_ref[0, mid]]] + [[tt_ref[0, BIAS_TILES - 1]] for _ in range(nt - 1)]
        else:
            rows = [[tt_ref[0, jnp.clip(d * nt + (b - a), -mid, mid) + mid] for a in q_tiles] for b in range(nt)]
        return jnp.concatenate([jnp.concatenate(r, axis=1) for r in rows], axis=0)

    def run(c, bias):
        def scores(mp, cols):
            s_t = jnp.dot(k, maps[mp][:, cols], preferred_element_type=F32)
            return s_t if bias is None else s_t + bias(cols)

        _run_strips([functools.partial(scores, mp, cols) for mp, cols in strips],
                    [scr(mp, cols) for mp, cols in strips], c, v_ext, m_scr, l_scr, acc_scr)

    is_near = (d >= near[0]) & (d <= near[1])

    @pl.when(is_near)
    def _():
        run(0.0, strip_bias)

    @pl.when(jnp.logical_not(is_near))
    def _():
        run(jnp.where(d < 0, c_left, c_right), None)

    @pl.when(ki == pl.num_programs(3) - 1)
    def _():
        o1 = acc_scr[:, :tq] / l_scr[:, :tq]
        o2 = acc_scr[:, tq:] / l_scr[:, tq:]
        o_ref[...] = (o1 - lam * o2).T


def _diff_attention(qt, k, vt, km, vmt, tt, bm, cst, buf, n_seq, n_q, n_k, q_col0, k_row0, tq, tk, meta_q):
    nq, nk = n_q // tq, n_k // tk
    qb0, kb0 = q_col0 // tq, k_row0 // tk
    r = 2 * tq
    return pl.pallas_call(
        functools.partial(_diff_kernel, meta_q=meta_q),
        grid=(n_seq, C_HEADS, nq, nk),
        in_specs=[pl.BlockSpec((LANES, tq), lambda s, h, i, j: (h, qb0 + s * nq + i)),
                  pl.BlockSpec((tk, LANES), lambda s, h, i, j: (kb0 + s * nk + j, h)),
                  pl.BlockSpec((C_DV, tk), lambda s, h, i, j: (h, kb0 + s * nk + j)),
                  pl.BlockSpec((1, LANES, LANES), lambda s, h, i, j: (s, 0, h)),
                  pl.BlockSpec((1, C_DV, LANES), lambda s, h, i, j: (s, h, 0)),
                  pl.BlockSpec((1, BIAS_TILES, LANES, LANES), lambda s, h, i, j: (h, 0, 0, 0)),
                  pl.BlockSpec((1, 1, LANES, tq), lambda s, h, i, j: (h, jnp.minimum(i, 1), 0, 0)),
                  pl.BlockSpec((1, 8, LANES), lambda s, h, i, j: (h, 0, 0)),
                  pl.BlockSpec(memory_space=pl.ANY)],
        out_specs=pl.BlockSpec((tq, C_DV), lambda s, h, i, j: (qb0 + s * nq + i, h)),
        out_shape=jax.ShapeDtypeStruct(buf.shape, F32),
        scratch_shapes=[pltpu.VMEM((1, r), F32), pltpu.VMEM((1, r), F32), pltpu.VMEM((C_DV, r), F32)],
        input_output_aliases={8: 0},
        compiler_params=_cparams(("parallel", "parallel", "parallel", "arbitrary")),
        name="diff_attention",
    )(qt, k, vt, km, vmt, tt, bm, cst, buf)


def _tri_masks(rev):
    i = lax.broadcasted_iota(jnp.int32, (CHUNK, CHUNK), 0)
    j = lax.broadcasted_iota(jnp.int32, (CHUNK, CHUNK), 1)
    if rev:
        return j >= i, j > i, i, j
    return j <= i, j < i, i, j


def _cum_mat_t(rev):
    t = lax.broadcasted_iota(jnp.int32, (CHUNK, CHUNK), 0)
    j = lax.broadcasted_iota(jnp.int32, (CHUNK, CHUNK), 1)
    return jnp.where((t >= j) if rev else (t <= j), 1.0, 0.0)


def _gdn_kernel(qf_ref, kf_ref, vf_ref, cf_ref, rf_ref, qb_ref, kb_ref, vb_ref, cb_ref, rb_ref,
                s0f_ref, s0b_ref, bf_buf, bb_buf, of_ref, ob_ref, sff_ref, sfb_ref, s_scr):
    del bf_buf, bb_buf
    c = pl.program_id(2)
    nv = 2 * A_HG

    @pl.when(c == 0)
    def _():
        s_scr[0] = s0f_ref[0]
        s_scr[1] = s0b_ref[0]

    streams = ((qf_ref, kf_ref, vf_ref, cf_ref, rf_ref, False), (qb_ref, kb_ref, vb_ref, cb_ref, rb_ref, True))
    units = [(d, e) for d in range(2) for e in range(nv)]
    eye = jnp.where(_tri_masks(False)[0] & _tri_masks(True)[0], 1.0, 0.0)
    q, k, kk, qk, gc_all, gr_all, col, masks = [], [], [], [], [], [], [], []
    for q_ref, k_ref, _, c_ref, r_ref, rev in streams:
        tri, strict, _, _ = _tri_masks(rev)
        masks.append((tri, strict))
        qd = [q_ref[:, j * A_DK:(j + 1) * A_DK] for j in range(A_HG)]
        kd = [k_ref[:, j * A_DK:(j + 1) * A_DK] for j in range(A_HG)]
        q.append(qd)
        k.append(kd)
        kk.append([_dot_nt(x, x) for x in kd])
        qk.append([_dot_nt(x, y) for x, y in zip(qd, kd)])
        cv = c_ref[...]
        col.append(cv)
        gc_all.append(_dot_exact_lhs(jnp.where(tri, 1.0, 0.0), cv))
        gr_all.append(_dot_exact_rhs(r_ref[0, 0], _cum_mat_t(rev)))
    gcb = [jnp.broadcast_to(gc_all[d][:, e:e + 1], (CHUNK, LANES)) for d, e in units]
    beta = [jnp.broadcast_to(col[d][:, nv + e:nv + e + 1], (CHUNK, LANES)) for d, e in units]
    decay = [jnp.exp(jnp.where(masks[d][0], gcb[u][:, :CHUNK] - gr_all[d][e:e + 1, :], -jnp.inf))
             for u, (d, e) in enumerate(units)]
    a = [jnp.where(masks[d][1], beta[u][:, :CHUNK] * kk[d][e // 2] * decay[u], 0.0)
         for u, (d, e) in enumerate(units)]
    x = [eye - au for au in a]
    p = a
    for _ in range(5):
        p = [_dot(pu, pu) for pu in p]
        x = [xu + _dot(xu, pu) for xu, pu in zip(x, p)]
    egc = [jnp.exp(g) for g in gcb]
    v = [streams[d][2][:, e * A_DV:(e + 1) * A_DV] for d, e in units]
    uw = [_dot(x[u], jnp.concatenate([v[u] * beta[u], k[d][e // 2] * (beta[u] * egc[u])], axis=1))
          for u, (d, e) in enumerate(units)]
    s = [s_scr[d, e] for d, e in units]
    r2 = [_dot(jnp.concatenate([uw[u][:, A_DV:], q[d][e // 2] * egc[u]], axis=0), s[u])
          for u, (d, e) in enumerate(units)]
    v_new = [uw[u][:, :A_DV] - r2[u][:CHUNK] for u in range(len(units))]
    o = [r2[u][CHUNK:] + _dot(qk[d][e // 2] * decay[u], v_new[u]) for u, (d, e) in enumerate(units)]
    gtot = [gcb[u][0:1, :] if d else gcb[u][CHUNK - 1:CHUNK, :] for u, (d, e) in enumerate(units)]
    s_new = [s[u] * jnp.exp(gtot[u]) + _dot_tn(k[d][e // 2] * jnp.exp(gtot[u] - gcb[u]), v_new[u])
             for u, (d, e) in enumerate(units)]
    for u, (d, e) in enumerate(units):
        (ob_ref if d else of_ref)[:, e * A_DV:(e + 1) * A_DV] = o[u]
        s_scr[d, e] = s_new[u]

    @pl.when(c == pl.num_programs(2) - 1)
    def _():
        sff_ref[0] = s_scr[0]
        sfb_ref[0] = s_scr[1]


def _gdn_scan(qkv, colf, colb, rowf, rowb, s0f, s0b, buf_f, buf_b, n_seq, n_tok, row0):
    nc = n_tok // CHUNK
    cb0 = row0 // CHUNK
    ng = A_QK_HEADS // A_HG
    nv = 2 * A_HG
    fwd = lambda s, j, c: cb0 + s * nc + c
    bwd = lambda s, j, c: cb0 + s * nc + (nc - 1 - c)
    k_blk, v_blk = A_QK // (A_HG * A_DK), 2 * A_QK // (nv * A_DV)

    def stream_specs(ix):
        return [pl.BlockSpec((CHUNK, A_HG * A_DK), lambda s, j, c: (ix(s, j, c), j)),
                pl.BlockSpec((CHUNK, A_HG * A_DK), lambda s, j, c: (ix(s, j, c), k_blk + j)),
                pl.BlockSpec((CHUNK, nv * A_DV), lambda s, j, c: (ix(s, j, c), v_blk + j)),
                pl.BlockSpec((CHUNK, LANES), lambda s, j, c: (ix(s, j, c), j)),
                pl.BlockSpec((1, 1, nv, CHUNK), lambda s, j, c: (ix(s, j, c), j, 0, 0))]

    st_spec = pl.BlockSpec((1, nv, A_DK, A_DV), lambda s, j, c: (s, j, 0, 0))
    st_shape = jax.ShapeDtypeStruct((n_seq, A_V_HEADS, A_DK, A_DV), F32)
    any_spec = pl.BlockSpec(memory_space=pl.ANY)
    return pl.pallas_call(
        _gdn_kernel,
        grid=(n_seq, ng, nc),
        in_specs=stream_specs(fwd) + stream_specs(bwd) + [st_spec, st_spec, any_spec, any_spec],
        out_specs=[pl.BlockSpec((CHUNK, nv * A_DV), lambda s, j, c: (fwd(s, j, c), j)),
                   pl.BlockSpec((CHUNK, nv * A_DV), lambda s, j, c: (bwd(s, j, c), j)),
                   st_spec, st_spec],
        out_shape=[jax.ShapeDtypeStruct(buf_f.shape, F32), jax.ShapeDtypeStruct(buf_b.shape, F32),
                   st_shape, st_shape],
        scratch_shapes=[pltpu.VMEM((2, nv, A_DK, A_DV), F32)],
        input_output_aliases={12: 0, 13: 1},
        compiler_params=_cparams(("parallel", "parallel", "arbitrary")),
        name="gdn_scan",
    )(qkv, qkv, qkv, colf, rowf, qkv, qkv, qkv, colb, rowb, s0f, s0b, buf_f, buf_b)


GLA_LEVELS = (5, 4, 3, 2, 1, 0)
GLA_SAFE_EXP = 60.0


def _gla_level(i, j, lv, rev):
    sup = lv + 1
    base = jnp.left_shift(jnp.right_shift(i, sup), sup)
    ref_row = base + (1 << lv) - (0 if rev else 1)
    same = jnp.right_shift(i, sup) == jnp.right_shift(j, sup)
    i_late = jnp.bitwise_and(jnp.right_shift(i, lv), 1) == (0 if rev else 1)
    j_early = jnp.bitwise_and(jnp.right_shift(j, lv), 1) == (1 if rev else 0)
    return jnp.where(j == ref_row, 1.0, 0.0), same & i_late & j_early


def _rows_bcast(x, rows, span):
    zero = jnp.zeros((1, x.shape[1]), x.dtype)
    picked = [x[r:r + 1] if 0 <= r < x.shape[0] else zero for r in rows]
    return jnp.concatenate([jnp.broadcast_to(r, (span, x.shape[1])) for r in picked], axis=0)


def _gla_block16(i, j, rev):
    blk = jnp.right_shift(i, 4)
    ref_row = jnp.left_shift(blk + 1, 4) if rev else jnp.left_shift(blk, 4) - 1
    tri = (j >= i) if rev else (j <= i)
    return jnp.where(j == ref_row, 1.0, 0.0), tri & (blk == jnp.right_shift(j, 4))


def _gla_pair(q, k, bc, ref, key_side_nonpositive):
    ek = ref - bc
    if key_side_nonpositive:
        ek = jnp.minimum(ek, 0.0)
    return (q * jnp.exp(jnp.minimum(bc - ref, 0.0))).astype(BF16), (k * jnp.exp(ek)).astype(BF16)


def _gla_kernel(qf_ref, kf_ref, vf_ref, lf_ref, qb_ref, kb_ref, vb_ref, lb_ref, w2_ref, gb_ref,
                s0f_ref, s0b_ref, bf_buf, bb_buf, of_ref, ob_ref, sff_ref, sfb_ref, s_scr, a_scr):
    del bf_buf, bb_buf
    c = pl.program_id(1)

    @pl.when(c == 0)
    def _():
        s_scr[0] = s0f_ref[0]
        s_scr[1] = s0b_ref[0]

    streams = ((qf_ref, kf_ref, vf_ref, lf_ref, False), (qb_ref, kb_ref, vb_ref, lb_ref, True))
    heads = [slice(hd * D_DK, (hd + 1) * D_DK) for hd in range(D_HEADS)]
    units = [(d, hd) for d in range(2) for hd in range(D_HEADS)]
    ops = []
    for d, (q_ref, k_ref, v_ref, l_ref, rev) in enumerate(streams):
        tri, _, i, j = _tri_masks(rev)
        x = _dot(l_ref[...], w2_ref[d]) + gb_ref[d]
        lg = (jnp.minimum(x, 0.0) - jnp.log(1.0 + jnp.exp(-jnp.abs(x)))) * (1.0 / GATE_TAU)
        bc = _dot_exact_lhs(jnp.where(tri, 1.0, 0.0), lg)
        btot = bc[0:1] if rev else bc[CHUNK - 1:CHUNK]
        q, k = q_ref[...] * D_DK ** -0.5, k_ref[...]
        def level(lv, q=q, k=k, bc=bc, i=i, j=j, rev=rev):
            pick, mask = _gla_level(i, j, lv, rev)
            if (1 << lv) >= 8:
                span = 2 << lv
                ref = _rows_bcast(bc, [b + (1 << lv) - (0 if rev else 1) for b in range(0, CHUNK, span)], span)
            else:
                ref = _dot_exact_lhs(pick, bc)
            return (mask,) + _gla_pair(q, k, bc, ref, True)

        _, mask16 = _gla_block16(i, j, rev)
        ref16 = _rows_bcast(bc, [b + 16 if rev else b - 1 for b in range(0, CHUNK, 16)], 16)
        ops.append(dict(level=level, coarse=[level(lv) for lv in GLA_LEVELS[:2]],
                        block16=(mask16,) + _gla_pair(q, k, bc, ref16, False), risk=jnp.max(ref16 - bc),
                        diag=(i == j, q.astype(BF16), k.astype(BF16)),
                        qg=(q * jnp.exp(bc)).astype(BF16), kd=(k * jnp.exp(btot - bc)).astype(BF16),
                        etot=jnp.exp(btot), v=v_ref[...].astype(BF16)))

    def term(pair, hd):
        mask, qe, ke = pair
        return jnp.where(mask, _dot_nt(qe[:, heads[hd]], ke[:, heads[hd]]), 0.0)

    a_coarse = [term(ops[d]['coarse'][0], hd) + term(ops[d]['coarse'][1], hd) for d, hd in units]
    risk = jnp.maximum(ops[0]['risk'], ops[1]['risk'])

    @pl.when(risk <= GLA_SAFE_EXP)
    def _():
        for u, (d, hd) in enumerate(units):
            a_scr[u] = a_coarse[u] + term(ops[d]['block16'], hd)

    @pl.when(risk > GLA_SAFE_EXP)
    def _():
        fine = [[ops[d]['diag']] + [ops[d]['level'](lv) for lv in GLA_LEVELS[2:]] for d in range(2)]
        for u, (d, hd) in enumerate(units):
            a_scr[u] = a_coarse[u] + functools.reduce(lambda x, y: x + y, [term(pr, hd) for pr in fine[d]])

    a = [a_scr[u] for u in range(len(units))]
    st = [s_scr[d, hd] for d, hd in units]
    vs = [ops[d]['v'][:, hd * D_DV:(hd + 1) * D_DV] for d, hd in units]
    o = [_dot_nt(ops[d]['qg'][:, heads[hd]], st[u]) + _dot(a[u], vs[u]) for u, (d, hd) in enumerate(units)]
    st_new = [st[u] * ops[d]['etot'][:, heads[hd]] + _dot_tn(vs[u], ops[d]['kd'][:, heads[hd]])
              for u, (d, hd) in enumerate(units)]
    for u, (d, hd) in enumerate(units):
        (ob_ref if d else of_ref)[:, hd * D_DV:(hd + 1) * D_DV] = o[u]
        s_scr[d, hd] = st_new[u]

    @pl.when(c == pl.num_programs(1) - 1)
    def _():
        sff_ref[0] = s_scr[0]
        sfb_ref[0] = s_scr[1]


def _gla_scan(p, lr, w2, gb, s0f, s0b, buf_f, buf_b, n_seq, n_tok, row0):
    nc = n_tok // CHUNK
    cb0 = row0 // CHUNK
    fwd = lambda s, c: cb0 + s * nc + c
    bwd = lambda s, c: cb0 + s * nc + (nc - 1 - c)

    def stream_specs(ix):
        return [pl.BlockSpec((CHUNK, D_QKW), lambda s, c: (ix(s, c), 0)),
                pl.BlockSpec((CHUNK, D_QKW), lambda s, c: (ix(s, c), 1)),
                pl.BlockSpec((CHUNK, D_VW), lambda s, c: (ix(s, c), 2 * D_QKW // D_VW)),
                pl.BlockSpec((CHUNK, 2 * GATE_RANK), lambda s, c: (ix(s, c), 0))]

    st_spec = pl.BlockSpec((1, D_HEADS, D_DV, D_DK), lambda s, c: (s, 0, 0, 0))
    st_shape = jax.ShapeDtypeStruct((n_seq, D_HEADS, D_DV, D_DK), F32)
    any_spec = pl.BlockSpec(memory_space=pl.ANY)
    return pl.pallas_call(
        _gla_kernel,
        grid=(n_seq, nc),
        in_specs=stream_specs(fwd) + stream_specs(bwd) + [
            pl.BlockSpec((2, 2 * GATE_RANK, D_QKW), lambda s, c: (0, 0, 0)),
            pl.BlockSpec((2, 1, D_QKW), lambda s, c: (0, 0, 0)),
            st_spec, st_spec, any_spec, any_spec],
        out_specs=[pl.BlockSpec((CHUNK, D_VW), lambda s, c: (fwd(s, c), 0)),
                   pl.BlockSpec((CHUNK, D_VW), lambda s, c: (bwd(s, c), 0)),
                   st_spec, st_spec],
        out_shape=[jax.ShapeDtypeStruct(buf_f.shape, F32), jax.ShapeDtypeStruct(buf_b.shape, F32),
                   st_shape, st_shape],
        scratch_shapes=[pltpu.VMEM((2, D_HEADS, D_DV, D_DK), F32), pltpu.VMEM((2 * D_HEADS, CHUNK, CHUNK), F32)],
        input_output_aliases={12: 0, 13: 1},
        compiler_params=_cparams(("parallel", "arbitrary")),
        name="gla_scan",
    )(p, p, p, lr, p, p, p, lr, w2, gb, s0f, s0b, buf_f, buf_b)


def _gdn_prep_kernel(x_ref, prev_ref, next_ref, edge_ref, w_ref, o_ref, *, starts, ends):
    i, j = pl.program_id(0), pl.program_id(1)
    tm, cw = x_ref.shape
    r0 = i * tm
    is_start = functools.reduce(jnp.logical_or, [r0 == s for s in starts])
    is_end = functools.reduce(jnp.logical_or, [r0 + tm == e for e in ends])
    x = x_ref[...]
    prev_row = jnp.where(is_start, edge_ref[0, 7:8, :], prev_ref[7:8, :])
    next_row = jnp.where(is_end, 0.0, next_ref[0:1, :])
    rows = lax.broadcasted_iota(jnp.int32, (tm, 1), 0)
    xp = jnp.where(rows == 0, prev_row, pltpu.roll(x, 1, axis=0))
    xn = jnp.where(rows == tm - 1, next_row, pltpu.roll(x, tm - 1, axis=0))
    y = xp * w_ref[0:1, :] + x * w_ref[1:2, :] + xn * w_ref[2:3, :]
    y = y * _sigmoid(y)
    is_qk = j < 2 * A_QK // cw
    q_scale = jnp.where(j < A_QK // cw, A_DK ** -0.5, 1.0)
    for c0 in range(0, cw, A_DK):
        yh = y[:, c0:c0 + A_DK]
        nrm = yh * (lax.rsqrt(jnp.sum(yh * yh, axis=-1, keepdims=True) + EPS) * q_scale)
        o_ref[:, c0:c0 + A_DK] = jnp.where(is_qk, nrm, yh)


def _gdn_prep(lay, x, edge, conv_w):
    r, c = x.shape
    tm, cw = min(1024, min(n for _, n, _, _ in lay.groups)), 512
    starts = [row + s * n for b, n, row, _ in lay.groups for s in range(b)]
    ends = [row + (s + 1) * n for b, n, row, _ in lay.groups for s in range(b)]
    seq_of = lambda i: sum((i * tm >= s).astype(jnp.int32) for s in starts) - 1
    w8 = jnp.pad(conv_w.astype(F32), ((0, 8 - conv_w.shape[0]), (0, 0)))
    return pl.pallas_call(
        functools.partial(_gdn_prep_kernel, starts=starts, ends=ends),
        grid=(r // tm, c // cw),
        in_specs=[pl.BlockSpec((tm, cw), lambda i, j: (i, j)),
                  pl.BlockSpec((8, cw), lambda i, j: (jnp.maximum(i * (tm // 8) - 1, 0), j)),
                  pl.BlockSpec((8, cw), lambda i, j: (jnp.minimum((i + 1) * (tm // 8), r // 8 - 1), j)),
                  pl.BlockSpec((1, 8, cw), lambda i, j: (seq_of(i), 0, j)),
                  pl.BlockSpec((8, cw), lambda i, j: (0, j))],
        out_specs=pl.BlockSpec((tm, cw), lambda i, j: (i, j)),
        out_shape=jax.ShapeDtypeStruct((r, c), F32),
        compiler_params=_cparams(("parallel", "parallel")),
        name="gdn_prep",
    )(x, x, x, edge, w8)


def _proj_prep_kernel(*refs, head_norm, rope, transpose):
    x_ref, g_ref, w_ref = refs[:3]
    rest = list(refs[3:])
    hg_ref = rest.pop(0) if head_norm else None
    cos_ref, sin_ref = (rest.pop(0), rest.pop(0)) if rope else (None, None)
    o_ref, = rest
    x = x_ref[...]
    xn = (x * lax.rsqrt(jnp.mean(x * x, axis=-1, keepdims=True) + EPS) * g_ref[...]).astype(BF16)
    acc = jnp.dot(xn, w_ref[...], preferred_element_type=F32)
    lo = lax.broadcasted_iota(jnp.int32, (1, LANES), 1) < C_DQK
    for c0 in range(0, acc.shape[1], LANES):
        y = acc[:, c0:c0 + LANES]
        if head_norm == "head":
            y = y * lax.rsqrt(jnp.mean(y * y, axis=-1, keepdims=True) + EPS) * hg_ref[...]
        elif head_norm == "maps":
            sq = y * y
            s_lo = jnp.sum(jnp.where(lo, sq, 0.0), axis=-1, keepdims=True)
            s_hi = jnp.sum(jnp.where(lo, 0.0, sq), axis=-1, keepdims=True)
            inv = jnp.where(lo, lax.rsqrt(s_lo * (1.0 / C_DQK) + EPS), lax.rsqrt(s_hi * (1.0 / C_DQK) + EPS))
            y = y * inv * hg_ref[...]
        if rope:
            y = y * cos_ref[...] + pltpu.roll(y, LANES // 2, axis=1) * sin_ref[...]
        if transpose:
            o_ref[c0:c0 + LANES, :] = y.T.astype(BF16)
        else:
            o_ref[:, c0:c0 + LANES] = y.astype(BF16)


def _pos_block(lay, tm):
    def pos(i):
        r = i * tm
        out = 0
        for b, n, row, _ in lay.groups:
            inside = (r >= row) & (r < row + b * n)
            out = out + jnp.where(inside, ((r - row) % n) // tm, 0)
        return out
    return pos


def _proj_prep(lay, x, g, w, head_gain=None, head_norm=None, rope=None, transpose=False):
    m, k = x.shape
    n = w.shape[1]
    tm = min(512, min(nt for _, nt, _, _ in lay.groups))
    tn = min(n, 1024)
    pos = _pos_block(lay, tm)
    fixed = lambda j, i: (0, 0)
    ins = [x, g.reshape(1, k).astype(F32), w.astype(BF16)]
    specs = [pl.BlockSpec((tm, k), lambda j, i: (i, 0)), pl.BlockSpec((1, k), fixed),
             pl.BlockSpec((k, tn), lambda j, i: (0, j))]
    if head_norm:
        ins.append(head_gain.reshape(1, LANES).astype(F32))
        specs.append(pl.BlockSpec((1, LANES), fixed))
    if rope:
        ins += list(rope)
        specs += [pl.BlockSpec((tm, LANES), lambda j, i: (pos(i), 0))] * 2
    if transpose:
        out_spec, out_shape = pl.BlockSpec((tn, tm), lambda j, i: (j, i)), (n, m)
    else:
        out_spec, out_shape = pl.BlockSpec((tm, tn), lambda j, i: (i, j)), (m, n)
    return pl.pallas_call(
        functools.partial(_proj_prep_kernel, head_norm=head_norm, rope=bool(rope), transpose=transpose),
        grid=(n // tn, m // tm),
        in_specs=specs, out_specs=out_spec,
        out_shape=jax.ShapeDtypeStruct(out_shape, BF16),
        compiler_params=_cparams(("parallel", "parallel")),
        name="proj_prep",
    )(*ins)


class _Layout:
    def __init__(self, shapes):
        self.groups = []
        row, seq = 0, 0
        for b, n in shapes:
            self.groups.append((b, n, row, seq))
            row += b * n
            seq += b
        self.rows, self.n_seq = row, seq

    def to_seqs(self, main, meta):
        out = []
        c = main.shape[-1]
        for b, n, row, seq in self.groups:
            m = meta[seq * N_META:(seq + b) * N_META].reshape(b, N_META, c)
            x = main[row:row + b * n].reshape(b, n, c)
            out.append(jnp.concatenate([m, x], axis=1))
        return out

    def from_seqs(self, seqs):
        main = jnp.concatenate([s[:, N_META:].reshape(-1, s.shape[-1]) for s in seqs], axis=0)
        meta = jnp.concatenate([s[:, :N_META].reshape(-1, s.shape[-1]) for s in seqs], axis=0)
        return main, meta


def _meta_chunk(meta):
    c = meta.shape[-1]
    m = meta.reshape(-1, N_META, c)
    return jnp.pad(m, ((0, 0), (CHUNK - N_META, 0), (0, 0))).reshape(-1, c)


def _meta_unchunk(x):
    c = x.shape[-1]
    return x.reshape(-1, CHUNK, c)[:, CHUNK - N_META:].reshape(-1, c)


def _map_seqs(lay, fn, main, meta):
    return lay.from_seqs([fn(s) for s in lay.to_seqs(main, meta)])


def _run_scan(scan, lay, main_args, meta_args, state_shape, width):
    s_all = lay.n_seq
    zeros = jnp.zeros((s_all,) + state_shape, F32)
    mrows = s_all * CHUNK
    mbuf = jnp.zeros((mrows, width), F32)
    _, _, s_meta, _ = scan(*meta_args, zeros, zeros, mbuf, mbuf, s_all, CHUNK, 0)
    of = jnp.zeros((lay.rows, width), F32)
    ob = jnp.zeros((lay.rows, width), F32)
    finals = []
    for b, n, row, seq in lay.groups:
        of, ob, _, sfb = scan(*main_args, s_meta[seq:seq + b], zeros[seq:seq + b], of, ob, b, n, row)
        finals.append(sfb)
    mof, mob, _, _ = scan(*meta_args, zeros, jnp.concatenate(finals, axis=0), mbuf, mbuf, s_all, CHUNK, 0)
    return (of, ob), (_meta_unchunk(mof), _meta_unchunk(mob))


def _gdn_layer(lay, h, hm, norm_g, w_in, conv_w, a_log, dt_bias, out_g, w_out):
    w_qkv, w_z, w_ab = (w_in[:, :A_CONV_CH], w_in[:, A_CONV_CH:A_CONV_CH + A_VW], w_in[:, A_CONV_CH + A_VW:])
    x, xm = _norm_proj(h, norm_g, w_qkv), _norm_proj(hm, norm_g, w_qkv)
    z, zm = _norm_proj(h, norm_g, w_z), _norm_proj(hm, norm_g, w_z)
    ab, abm = _norm_proj(h, norm_g, w_ab), _norm_proj(hm, norm_g, w_ab)
    s_all = lay.n_seq
    xm3 = xm.reshape(s_all, N_META, A_CONV_CH)
    qkv = _gdn_prep(lay, x, xm3[:, N_META - 8:], conv_w)

    def prep(x):
        prev = jnp.pad(x[:, :-1], ((0, 0), (1, 0), (0, 0)))
        nxt = jnp.pad(x[:, 1:], ((0, 0), (0, 1), (0, 0)))
        y = jax.nn.silu(prev * conv_w[0] + x * conv_w[1] + nxt * conv_w[2])
        b, l, _ = y.shape
        qk = y[..., :2 * A_QK].reshape(b, l, 2 * A_QK_HEADS, A_DK)
        qk = qk * lax.rsqrt(jnp.sum(qk * qk, axis=-1, keepdims=True) + EPS)
        qk = qk.reshape(b, l, 2, A_QK) * jnp.array([A_DK ** -0.5, 1.0], F32)[:, None]
        return jnp.concatenate([qk.reshape(b, l, 2 * A_QK), y[..., 2 * A_QK:]], axis=-1)

    first = x[np.array([row + s * n for b, n, row, _ in lay.groups for s in range(b)])]
    qkvm = prep(jnp.concatenate([xm3, first[:, None]], axis=1))[:, :N_META].reshape(-1, A_CONV_CH)

    def gates(x):
        x = x.reshape(-1, 2, 2, A_V_HEADS)
        g = -jnp.exp(a_log.astype(F32)) * jax.nn.softplus(x[:, 0] + dt_bias.astype(F32))
        return g, jax.nn.sigmoid(x[:, 1])

    ng, nv = A_QK_HEADS // A_HG, 2 * A_HG

    def pack(g, beta, d):
        r = g.shape[0]
        gd, bd = g[:, d].reshape(r, ng, nv), beta[:, d].reshape(r, ng, nv)
        col = jnp.concatenate([gd, bd, jnp.zeros((r, ng, LANES - 2 * nv), F32)], axis=-1).reshape(r, ng * LANES)
        row = gd.reshape(r // CHUNK, CHUNK, ng, nv).transpose(0, 2, 3, 1)
        return col, row

    def scan_args(qkv_rows, g, beta):
        colf, rowf = pack(g, beta, 0)
        colb, rowb = pack(g, beta, 1)
        return (qkv_rows, colf, colb, rowf, rowb)

    g, beta = gates(ab)
    gm, betam = gates(abm)
    pad_gate = lambda t: _meta_chunk(t.reshape(-1, 2 * A_V_HEADS)).reshape(-1, 2, A_V_HEADS)
    (of, ob), (mof, mob) = _run_scan(
        _gdn_scan, lay, scan_args(qkv, g, beta), scan_args(_meta_chunk(qkvm), pad_gate(gm), pad_gate(betam)),
        (A_V_HEADS, A_DK, A_DV), A_VW)
    gain = jnp.tile(out_g.astype(F32), A_V_HEADS)
    return (_out_proj([of, ob], z, gain, w_out, h, A_DV), _out_proj([mof, mob], zm, gain, w_out, hm, A_DV))


def _rope_tables(n_tok):
    rows = n_tok // GRID_W
    row = jnp.repeat(jnp.arange(rows, dtype=F32), GRID_W)
    col = (jnp.arange(rows * GRID_W) % GRID_W).astype(F32)
    inv = ROPE_THETA ** (-jnp.arange(0, ROPE_AXIS, 2, dtype=F32) / ROPE_AXIS)
    ang = jnp.concatenate([row[:, None] * inv, col[:, None] * inv], axis=-1)
    return jnp.cos(ang), jnp.sin(ang)


def _pad_meta(x, n_seq):
    return jnp.pad(x.reshape(n_seq, N_META, -1), ((0, 0), (0, LANES - N_META), (0, 0)))


def _gqa_layer(lay, h, hm, norm_g, w_in, q_g, k_g, w_out):
    k_end, kv_end = B_QW + B_KVW, B_QW + 2 * B_KVW
    z = _norm_proj(h, norm_g, w_in[:, kv_end:])
    pm = _norm_proj(hm, norm_g, w_in)

    def norm_heads(x, g):
        r = x.shape[0]
        x = x.reshape(r, -1, B_DH)
        return x * lax.rsqrt(jnp.mean(x * x, axis=-1, keepdims=True) + EPS) * g.astype(F32)

    scale = B_DH ** -0.5 * LOG2E
    cos, sin = _rope_tables(max(n for _, n, _, _ in lay.groups))
    rope = (jnp.concatenate([cos, cos], axis=-1), jnp.concatenate([-sin, sin], axis=-1))
    qt = _proj_prep(lay, h, norm_g, w_in[:, :B_QW], q_g.astype(F32) * scale, "head", rope, transpose=True)
    k = _proj_prep(lay, h, norm_g, w_in[:, B_QW:k_end], k_g, "head", rope)
    vt = _proj_prep(lay, h, norm_g, w_in[:, k_end:kv_end], transpose=True)
    s_all = lay.n_seq
    qm = (norm_heads(pm[:, :B_QW], q_g) * scale).reshape(-1, B_QW).astype(BF16)
    km = norm_heads(pm[:, B_QW:B_QW + B_KVW], k_g).reshape(-1, B_KVW).astype(BF16)
    vm = pm[:, B_QW + B_KVW:B_QW + 2 * B_KVW].astype(BF16)
    zm = pm[:, B_QW + 2 * B_KVW:]
    qmt = _pad_meta(qm, s_all).reshape(s_all * LANES, B_QW).T
    kmp, vmt = _pad_meta(km, s_all), _pad_meta(vm, s_all).transpose(0, 2, 1)

    o = jnp.zeros((lay.rows, B_QW), F32)
    om = jnp.zeros((s_all * LANES, B_QW), F32)
    for b, n, row, seq in lay.groups:
        tq, tk = min(GQA_TQ, n), min(GQA_TK, n)
        o = _gqa_attention(qt, k, vt, kmp[seq:seq + b], vmt[seq:seq + b], o, b, n, n, row, row, tq, tk)
        om = _gqa_attention(qmt, k, vt, kmp[seq:seq + b], vmt[seq:seq + b], om, b, LANES, n,
                            seq * LANES, row, LANES, tk)
    om = om.reshape(s_all, LANES, B_QW)[:, :N_META].reshape(-1, B_QW)
    ones = jnp.ones((B_QW,), F32)
    return _out_proj([o], z, ones, w_out, h, 0), _out_proj([om], zm, ones, w_out, hm, 0)


def _t5_bucket_np(rel):
    nb = REL_BUCKETS // 2
    max_exact = nb // 2
    n = np.abs(rel)
    steps = nb - max_exact
    large = np.full(n.shape, max_exact, np.int64)
    for kstep in range(1, steps + 1):
        large += (n.astype(np.float64) ** steps >= (max_exact ** steps) * float(REL_MAX_DIST // max_exact) ** kstep)
    large = np.minimum(large, nb - 1)
    return np.where(rel > 0, nb, 0) + np.where(n < max_exact, n, large)


def _diff_layer(lay, h, hm, norm_g, w_in, q_g, k_g, lam, sub_g, w_out, rel_bias, layer_idx):
    n_qkv = 2 * C_QKW + C_VW
    z = _norm_proj(h, norm_g, w_in[:, n_qkv:])
    pm = _norm_proj(hm, norm_g, w_in)

    def norm_maps(x, g):
        r = x.shape[0]
        x = x.reshape(r, -1, C_DQK)
        return (x * lax.rsqrt(jnp.mean(x * x, axis=-1, keepdims=True) + EPS) * g.astype(F32)).reshape(r, -1)

    s_all = lay.n_seq
    scale = C_DQK ** -0.5 * LOG2E
    qt = _proj_prep(lay, h, norm_g, w_in[:, :C_QKW], jnp.tile(q_g.astype(F32), 2) * scale, "maps", transpose=True)
    k = _proj_prep(lay, h, norm_g, w_in[:, C_QKW:2 * C_QKW], jnp.tile(k_g.astype(F32), 2), "maps")
    vt = _proj_prep(lay, h, norm_g, w_in[:, 2 * C_QKW:n_qkv], transpose=True)
    qm = (norm_maps(pm[:, :C_QKW], q_g) * scale).astype(BF16)
    km = norm_maps(pm[:, C_QKW:2 * C_QKW], k_g).astype(BF16)
    vm = pm[:, 2 * C_QKW:n_qkv].astype(BF16)
    zm = pm[:, n_qkv:]
    qmt = _pad_meta(qm, s_all).reshape(s_all * LANES, C_QKW).T
    kmp, vmt = _pad_meta(km, s_all), _pad_meta(vm, s_all).transpose(0, 2, 1)

    lam = lam.astype(F32)
    lam_init = 0.8 - 0.6 * math.exp(-0.3 * layer_idx)
    lam_full = jnp.exp(jnp.sum(lam[0] * lam[1])) - jnp.exp(jnp.sum(lam[2] * lam[3])) + lam_init
    table = rel_bias.astype(F32).T * LOG2E
    nb = REL_BUCKETS // 2

    r0 = (BIAS_TILES // 2) * LANES + N_META + LANES
    by_rel = table[:, _t5_bucket_np(np.arange(-r0, r0 + 1))]

    def toeplitz(shift):
        v = by_rel[:, r0 + shift - (LANES - 1):r0 + shift + LANES]
        w = jnp.pad(v, ((0, 0), (0, 1)))
        a = jnp.tile(w, (1, LANES))[:, :LANES * (2 * LANES - 1)].reshape(C_HEADS, LANES, 2 * LANES - 1)
        return a[:, :, LANES - 1:].transpose(0, 2, 1)

    cst = jnp.zeros((C_HEADS, 8, LANES), F32)
    cst = cst.at[:, 0].set(table[:, nb - 1:nb]).at[:, 1].set(table[:, 2 * nb - 1:2 * nb]).at[:, 2].set(lam_full)
    ar = np.arange(LANES)
    key_pad = jnp.asarray(np.where(ar < N_META, 0.0, NEG), F32)[:, None]
    offsets = range(-(BIAS_TILES // 2), BIAS_TILES // 2 + 1)
    tt = jnp.stack([toeplitz(o * LANES) for o in offsets], axis=1)
    tt_m = jnp.stack([toeplitz(o * LANES + N_META) for o in offsets], axis=1)
    meta_rows = (ar < N_META)[:, None]
    bm_m = (jnp.where(meta_rows, toeplitz(0), 0.0) + key_pad)[:, None]
    near_tile = toeplitz(-N_META)[:, :N_META]

    o = jnp.zeros((lay.rows, C_VW), F32)
    om = jnp.zeros((s_all * LANES, C_VW), F32)
    for b, n, row, seq in lay.groups:
        t, tk = min(DIFF_TQ, n), min(DIFF_TK, n)
        far = jnp.broadcast_to(table[:, nb - 1][:, None, None], (C_HEADS, N_META, t))
        near = jnp.concatenate([near_tile, far[:, :, LANES:]], axis=2)
        pad_rows = jnp.full((C_HEADS, 2, LANES - N_META, t), NEG, F32)
        bm = jnp.concatenate([jnp.stack([near, far], axis=1), pad_rows], axis=2)
        o = _diff_attention(qt, k, vt, kmp[seq:seq + b], vmt[seq:seq + b], tt, bm, cst, o,
                            b, n, n, row, row, t, tk, False)
        om = _diff_attention(qmt, k, vt, kmp[seq:seq + b], vmt[seq:seq + b], tt_m, bm_m, cst, om,
                             b, LANES, n, seq * LANES, row, LANES, tk, True)
    om = om.reshape(s_all, LANES, C_VW)[:, :N_META].reshape(-1, C_VW)
    gain = jnp.tile(sub_g.astype(F32), C_HEADS) * (1.0 - lam_init)
    return _out_proj([o], z, gain, w_out, h, C_DV), _out_proj([om], zm, gain, w_out, hm, C_DV)


def _gla_layer(lay, h, hm, norm_g, w_in, gate_w2, gate_b, out_g, w_out):
    n_qkv = 2 * D_QKW + D_VW
    w_qkv, w_z, w_lr = w_in[:, :n_qkv], w_in[:, n_qkv:n_qkv + D_VW], w_in[:, n_qkv + D_VW:]
    p, pm = _norm_proj(h, norm_g, w_qkv), _norm_proj(hm, norm_g, w_qkv)
    z, zm = _norm_proj(h, norm_g, w_z), _norm_proj(hm, norm_g, w_z)
    lr, lrm = _norm_proj(h, norm_g, w_lr), _norm_proj(hm, norm_g, w_lr)
    zero = jnp.zeros((GATE_RANK, D_QKW), F32)
    w2 = jnp.stack([jnp.concatenate([gate_w2[0].astype(F32), zero], axis=0),
                    jnp.concatenate([zero, gate_w2[1].astype(F32)], axis=0)])
    gb = gate_b.astype(F32).reshape(2, 1, D_QKW)
    (of, ob), (mof, mob) = _run_scan(
        _gla_scan, lay, (p, lr, w2, gb), (_meta_chunk(pm), _meta_chunk(lrm), w2, gb), (D_HEADS, D_DV, D_DK), D_VW)
    gain = jnp.tile(out_g.astype(F32), D_HEADS)
    return _out_proj([of, ob], z, gain, w_out, h, D_DV), _out_proj([mof, mob], zm, gain, w_out, hm, D_DV)


def _trunk(xs, meta_tokens, rel_bias, a, b, c, d, depth):
    lay = _Layout([(x.shape[0], x.shape[1]) for x in xs])
    h = jnp.concatenate([x.reshape(-1, D_MODEL) for x in xs], axis=0)
    hm = jnp.tile(meta_tokens.astype(F32), (lay.n_seq, 1))
    for i in range(depth):
        m, j = i % 4, i // 4
        if m == 0:
            h, hm = _gdn_layer(lay, h, hm, *(t[j] for t in a))
        elif m == 1:
            h, hm = _gqa_layer(lay, h, hm, *(t[j] for t in b))
        elif m == 2:
            h, hm = _diff_layer(lay, h, hm, *(t[j] for t in c), rel_bias, i)
        else:
            h, hm = _gla_layer(lay, h, hm, *(t[j] for t in d))
    return [h[row:row + bsz * n].reshape(bsz, n, D_MODEL) for bsz, n, row, _ in lay.groups]


def kernel(x_prompt, x_sample, meta_tokens, rel_bias, a_norm, a_w_in, a_conv, a_a_log, a_dt_bias, a_out_norm, a_w_out, b_norm, b_w_in, b_q_norm, b_k_norm, b_w_out, c_norm, c_w_in, c_q_norm, c_k_norm, c_lambda, c_sub_norm, c_w_out, d_norm, d_w_in, d_gate_w2, d_gate_b, d_out_norm, d_w_out):
    y_prompt, y_sample = _trunk(
        [x_prompt, x_sample], meta_tokens, rel_bias,
        (a_norm, a_w_in, a_conv, a_a_log, a_dt_bias, a_out_norm, a_w_out),
        (b_norm, b_w_in, b_q_norm, b_k_norm, b_w_out),
        (c_norm, c_w_in, c_q_norm, c_k_norm, c_lambda, c_sub_norm, c_w_out),
        (d_norm, d_w_in, d_gate_w2, d_gate_b, d_out_norm, d_w_out), 4)
    return (y_prompt, y_sample)
```

```python
import functools
import math

import numpy as np
import jax
import jax.numpy as jnp
from jax import lax
from jax.experimental import pallas as pl
from jax.experimental.pallas import tpu as pltpu

F32 = jnp.float32
BF16 = jnp.bfloat16

D_MODEL = 1024
N_META = 16
CHUNK = 64
GRID_W = 64
EPS = 1e-6
LANES = 128
VMEM_LIMIT = 56 * 1024 * 1024
NEG = -1e30
LOG2E = math.log2(math.e)
STRIP = 1024
ONES_ROWS = 16
PROJ_TN_MAX = 2048
GQA_TQ, GQA_TK = 1024, 2048
DIFF_TQ, DIFF_TK = 2048, 2048
BIAS_TILES = 5
META_TK = 8192

A_QK_HEADS, A_V_HEADS, A_DK, A_DV = 8, 16, 128, 128
A_QK, A_VW = A_QK_HEADS * A_DK, A_V_HEADS * A_DV
A_CONV_CH = 2 * A_QK + A_VW
A_HG = 8
B_HEADS, B_KV_HEADS, B_DH = 8, 2, 128
B_GROUP = B_HEADS // B_KV_HEADS
B_QW, B_KVW = B_HEADS * B_DH, B_KV_HEADS * B_DH
ROPE_AXIS = B_DH // 2
ROPE_THETA = 10000.0
C_HEADS, C_DQK, C_DV = 8, 64, 128
C_QKW, C_VW = C_HEADS * 2 * C_DQK, C_HEADS * C_DV
REL_BUCKETS, REL_MAX_DIST = 32, 128
D_HEADS, D_DK, D_DV = 4, 128, 256
D_QKW, D_VW = D_HEADS * D_DK, D_HEADS * D_DV
GATE_RANK, GATE_TAU = 16, 16.0


def _cparams(sem):
    return pltpu.CompilerParams(dimension_semantics=sem, vmem_limit_bytes=VMEM_LIMIT)


def _sigmoid(x):
    return 1.0 / (1.0 + jnp.exp(-x))


def _dot(a, b):
    return jnp.dot(a.astype(BF16), b.astype(BF16), preferred_element_type=F32)


def _dot_nt(a, b):
    return lax.dot_general(a.astype(BF16), b.astype(BF16), (((1,), (1,)), ((), ())),
                           preferred_element_type=F32)


def _dot_tn(a, b):
    return lax.dot_general(a.astype(BF16), b.astype(BF16), (((0,), (0,)), ((), ())),
                           preferred_element_type=F32)


def _split3(x):
    x1 = x.astype(BF16)
    r1 = x - x1.astype(F32)
    x2 = r1.astype(BF16)
    x3 = (r1 - x2.astype(F32)).astype(BF16)
    return x1, x2, x3


def _dot_exact_lhs(m01, x):
    m = m01.astype(BF16)
    x1, x2, x3 = _split3(x)
    return (jnp.dot(m, x1, preferred_element_type=F32) + jnp.dot(m, x2, preferred_element_type=F32)
            + jnp.dot(m, x3, preferred_element_type=F32))


def _dot_exact_rhs(x, m01):
    m = m01.astype(BF16)
    x1, x2, x3 = _split3(x)
    return (jnp.dot(x1, m, preferred_element_type=F32) + jnp.dot(x2, m, preferred_element_type=F32)
            + jnp.dot(x3, m, preferred_element_type=F32))


def _norm_proj_kernel(x_ref, g_ref, w_ref, o_ref):
    x = x_ref[...]
    ms = jnp.mean(x * x, axis=-1, keepdims=True)
    xn = (x * lax.rsqrt(ms + EPS) * g_ref[...]).astype(BF16)
    o_ref[...] = jnp.dot(xn, w_ref[...], preferred_element_type=F32)


def _norm_proj(x, g, w):
    m, k = x.shape
    n = w.shape[1]
    tm = 512 if m % 512 == 0 else m
    tn = max([t for t in range(256, PROJ_TN_MAX + 1, 256) if n % t == 0], default=n)
    return pl.pallas_call(
        _norm_proj_kernel,
        grid=(n // tn, m // tm),
        in_specs=[pl.BlockSpec((tm, k), lambda j, i: (i, 0)),
                  pl.BlockSpec((1, k), lambda j, i: (0, 0)),
                  pl.BlockSpec((k, tn), lambda j, i: (0, j))],
        out_specs=pl.BlockSpec((tm, tn), lambda j, i: (i, j)),
        out_shape=jax.ShapeDtypeStruct((m, n), F32),
        compiler_params=_cparams(("parallel", "parallel")),
        name="norm_proj",
    )(x, g.reshape(1, k).astype(F32), w.astype(BF16))


def _out_proj_kernel(*refs, n_o, hw):
    o_refs = refs[:n_o]
    z_ref, g_ref, w_ref, h_ref, out_ref = refs[n_o:]
    k = z_ref.shape[1]
    cw = hw if hw else 256
    acc = h_ref[...]
    for c0 in range(0, k, cw):
        o = o_refs[0][:, c0:c0 + cw]
        for r in o_refs[1:]:
            o = o + r[:, c0:c0 + cw]
        if hw:
            ms = jnp.mean(o * o, axis=-1, keepdims=True)
            o = o * lax.rsqrt(ms + EPS) * g_ref[:, c0:c0 + cw]
        z = z_ref[:, c0:c0 + cw]
        gated = o * (z * _sigmoid(z))
        acc = acc + jnp.dot(gated.astype(BF16), w_ref[c0:c0 + cw, :], preferred_element_type=F32)
    out_ref[...] = acc


def _out_proj(os, z, g, w, h, hw):
    m, k = z.shape
    n = w.shape[1]
    tm = (512 if k <= 1024 else 256) if m % 512 == 0 else m
    row = lambda i: (i, 0)
    fixed = lambda i: (0, 0)
    return pl.pallas_call(
        functools.partial(_out_proj_kernel, n_o=len(os), hw=hw),
        grid=(m // tm,),
        in_specs=[pl.BlockSpec((tm, k), row) for _ in os] + [
            pl.BlockSpec((tm, k), row), pl.BlockSpec((1, k), fixed),
            pl.BlockSpec((k, n), fixed), pl.BlockSpec((tm, n), row)],
        out_specs=pl.BlockSpec((tm, n), row),
        out_shape=jax.ShapeDtypeStruct((m, n), F32),
        compiler_params=_cparams(("parallel",)),
        name="out_proj",
    )(*os, z, g.reshape(1, k).astype(F32), w.astype(BF16), h)


def _strip_update(s_t, c, v_ext, m_scr, l_scr, acc_scr, cols):
    p, alpha = _strip_softmax(s_t, c, m_scr, cols)
    _strip_pv(p, alpha, v_ext, l_scr, acc_scr, cols)


def _strip_softmax(s_t, c, m_scr, cols):
    m_prev = m_scr[:, cols]
    m_new = jnp.maximum(m_prev, jnp.max(s_t, axis=0, keepdims=True) + c)
    m_scr[:, cols] = m_new
    return jnp.exp2(s_t - (m_new - c)).astype(BF16), jnp.exp2(m_prev - m_new)


def _strip_pv(p, alpha, v_ext, l_scr, acc_scr, cols):
    dv = acc_scr.shape[0]
    pv = jnp.dot(v_ext, p, preferred_element_type=F32)
    acc_scr[:, cols] = alpha * acc_scr[:, cols] + pv[:dv]
    l_scr[:, cols] = alpha * l_scr[:, cols] + pv[dv:dv + 1]


def _run_strips(score_fns, cols_list, c, v_ext, m_scr, l_scr, acc_scr):
    s_next = score_fns[0]()
    pending = None
    for n, cols in enumerate(cols_list):
        s_t, s_next = s_next, (score_fns[n + 1]() if n + 1 < len(score_fns) else None)
        if pending is not None:
            _strip_pv(pending[0], pending[1], v_ext, l_scr, acc_scr, pending[2])
        p, alpha = _strip_softmax(s_t, c, m_scr, cols)
        pending = (p, alpha, cols)
    _strip_pv(pending[0], pending[1], v_ext, l_scr, acc_scr, pending[2])


def _attn_init(m_scr, l_scr, acc_scr):
    m_scr[...] = jnp.full(m_scr.shape, NEG, F32)
    l_scr[...] = jnp.zeros(l_scr.shape, F32)
    acc_scr[...] = jnp.zeros(acc_scr.shape, F32)


def _with_ones(v_t):
    return jnp.concatenate([v_t, jnp.ones((ONES_ROWS, v_t.shape[1]), BF16)], axis=0)


def _strips(width):
    w = min(STRIP, width)
    return [slice(c0, c0 + w) for c0 in range(0, width, w)]


def _gqa_kernel(qt_ref, k_ref, vt_ref, km_ref, vmt_ref, buf_ref, o_ref, m_scr, l_scr, acc_scr):
    del buf_ref
    ki = pl.program_id(3)
    tq = qt_ref.shape[1]
    heads = [slice(g * B_DH, (g + 1) * B_DH) for g in range(B_GROUP)]

    @pl.when(ki == 0)
    def _():
        _attn_init(m_scr, l_scr, acc_scr)
        key = lax.broadcasted_iota(jnp.int32, (LANES, 1), 0)
        pad = jnp.where(key < N_META, 0.0, NEG)
        vm_ext = _with_ones(vmt_ref[0])
        for g, rows in enumerate(heads):
            for cols in _strips(tq):
                s_t = jnp.dot(km_ref[0], qt_ref[rows, cols], preferred_element_type=F32) + pad
                _strip_update(s_t, 0.0, vm_ext, m_scr, l_scr, acc_scr,
                              slice(g * tq + cols.start, g * tq + cols.stop))

    k = k_ref[...]
    v_ext = _with_ones(vt_ref[...])
    work = [(g, rows, cols) for g, rows in enumerate(heads) for cols in _strips(tq)]
    scr_cols = [slice(g * tq + cols.start, g * tq + cols.stop) for g, _, cols in work]
    _run_strips([functools.partial(lambda rows, cols: jnp.dot(k, qt_ref[rows, cols], preferred_element_type=F32),
                                   rows, cols) for _, rows, cols in work],
                scr_cols, 0.0, v_ext, m_scr, l_scr, acc_scr)

    @pl.when(ki == pl.num_programs(3) - 1)
    def _():
        for g, rows in enumerate(heads):
            cols = slice(g * tq, (g + 1) * tq)
            o_ref[:, rows] = (acc_scr[:, cols] / l_scr[:, cols]).T


def _gqa_attention(qt, k, vt, km, vmt, buf, n_seq, n_q, n_k, q_col0, k_row0, tq, tk):
    nq, nk = n_q // tq, n_k // tk
    qb0, kb0 = q_col0 // tq, k_row0 // tk
    r = B_GROUP * tq
    gw = B_GROUP * B_DH
    return pl.pallas_call(
        _gqa_kernel,
        grid=(n_seq, B_KV_HEADS, nq, nk),
        in_specs=[pl.BlockSpec((gw, tq), lambda s, h, i, j: (h, qb0 + s * nq + i)),
                  pl.BlockSpec((tk, B_DH), lambda s, h, i, j: (kb0 + s * nk + j, h)),
                  pl.BlockSpec((B_DH, tk), lambda s, h, i, j: (h, kb0 + s * nk + j)),
                  pl.BlockSpec((1, LANES, B_DH), lambda s, h, i, j: (s, 0, h)),
                  pl.BlockSpec((1, B_DH, LANES), lambda s, h, i, j: (s, h, 0)),
                  pl.BlockSpec(memory_space=pl.ANY)],
        out_specs=pl.BlockSpec((tq, gw), lambda s, h, i, j: (qb0 + s * nq + i, h)),
        out_shape=jax.ShapeDtypeStruct(buf.shape, F32),
        scratch_shapes=[pltpu.VMEM((1, r), F32), pltpu.VMEM((1, r), F32), pltpu.VMEM((B_DH, r), F32)],
        input_output_aliases={5: 0},
        compiler_params=_cparams(("parallel", "parallel", "parallel", "arbitrary")),
        name="gqa_attention",
    )(qt, k, vt, km, vmt, buf)


def _diff_kernel(qt_ref, k_ref, vt_ref, km_ref, vmt_ref, tt_ref, bm_ref, cst_ref, buf_ref, o_ref,
                 m_scr, l_scr, acc_scr, *, meta_q):
    del buf_ref
    qi, ki = pl.program_id(2), pl.program_id(3)
    tq, tk = qt_ref.shape[1], k_ref.shape[0]
    nt = tk // LANES
    qt = qt_ref[...]
    dim = lax.broadcasted_iota(jnp.int32, qt.shape, 0)
    zero = jnp.zeros_like(qt)
    maps = (jnp.where(dim < C_DQK, qt, zero), jnp.where(dim >= C_DQK, qt, zero))
    c_left = cst_ref[0, 0:1, 0:1]
    c_right = cst_ref[0, 1:2, 0:1]
    lam = cst_ref[0, 2:3, 0:1]
    strips = [(mp, cols) for mp in range(2) for cols in _strips(tq)]
    scr = lambda mp, cols: slice(mp * tq + cols.start, mp * tq + cols.stop)

    @pl.when(ki == 0)
    def _():
        _attn_init(m_scr, l_scr, acc_scr)
        vm_ext = _with_ones(vmt_ref[0])
        for mp, cols in strips:
            s_t = jnp.dot(km_ref[0], maps[mp][:, cols], preferred_element_type=F32) + bm_ref[0, 0, :, cols]
            _strip_update(s_t, 0.0, vm_ext, m_scr, l_scr, acc_scr, scr(mp, cols))

    k = k_ref[...]
    v_ext = _with_ones(vt_ref[...])

    ratio = max(tq // tk, 1)
    d = ki - qi * ratio
    near = (0, 0) if meta_q else (-1, ratio)
    mid = BIAS_TILES // 2

    def strip_bias(cols):
        q_tiles = range(cols.start // LANES, cols.stop // LANES)
        if meta_q:
            rows = [[tt_ref[0, mid]]] + [[tt_ref[0, BIAS_TILES - 1]] for _ in range(nt - 1)]
        else:
            rows = [[tt_ref[0, jnp.clip(d * nt + (b - a), -mid, mid) + mid] for a in q_tiles] for b in range(nt)]
        return jnp.concatenate([jnp.concatenate(r, axis=1) for r in rows], axis=0)

    def run(c, bias):
        def scores(mp, cols):
            s_t = jnp.dot(k, maps[mp][:, cols], preferred_element_type=F32)
            return s_t if bias is None else s_t + bias(cols)

        _run_strips([functools.partial(scores, mp, cols) for mp, cols in strips],
                    [scr(mp, cols) for mp, cols in strips], c, v_ext, m_scr, l_scr, acc_scr)

    is_near = (d >= near[0]) & (d <= near[1])

    @pl.when(is_near)
    def _():
        run(0.0, strip_bias)

    @pl.when(jnp.logical_not(is_near))
    def _():
        run(jnp.where(d < 0, c_left, c_right), None)

    @pl.when(ki == pl.num_programs(3) - 1)
    def _():
        o1 = acc_scr[:, :tq] / l_scr[:, :tq]
        o2 = acc_scr[:, tq:] / l_scr[:, tq:]
        o_ref[...] = (o1 - lam * o2).T


def _diff_attention(qt, k, vt, km, vmt, tt, bm, cst, buf, n_seq, n_q, n_k, q_col0, k_row0, tq, tk, meta_q):
    nq, nk = n_q // tq, n_k // tk
    qb0, kb0 = q_col0 // tq, k_row0 // tk
    r = 2 * tq
    return pl.pallas_call(
        functools.partial(_diff_kernel, meta_q=meta_q),
        grid=(n_seq, C_HEADS, nq, nk),
        in_specs=[pl.BlockSpec((LANES, tq), lambda s, h, i, j: (h, qb0 + s * nq + i)),
                  pl.BlockSpec((tk, LANES), lambda s, h, i, j: (kb0 + s * nk + j, h)),
                  pl.BlockSpec((C_DV, tk), lambda s, h, i, j: (h, kb0 + s * nk + j)),
                  pl.BlockSpec((1, LANES, LANES), lambda s, h, i, j: (s, 0, h)),
                  pl.BlockSpec((1, C_DV, LANES), lambda s, h, i, j: (s, h, 0)),
                  pl.BlockSpec((1, BIAS_TILES, LANES, LANES), lambda s, h, i, j: (h, 0, 0, 0)),
                  pl.BlockSpec((1, 1, LANES, tq), lambda s, h, i, j: (h, jnp.minimum(i, 1), 0, 0)),
                  pl.BlockSpec((1, 8, LANES), lambda s, h, i, j: (h, 0, 0)),
                  pl.BlockSpec(memory_space=pl.ANY)],
        out_specs=pl.BlockSpec((tq, C_DV), lambda s, h, i, j: (qb0 + s * nq + i, h)),
        out_shape=jax.ShapeDtypeStruct(buf.shape, F32),
        scratch_shapes=[pltpu.VMEM((1, r), F32), pltpu.VMEM((1, r), F32), pltpu.VMEM((C_DV, r), F32)],
        input_output_aliases={8: 0},
        compiler_params=_cparams(("parallel", "parallel", "parallel", "arbitrary")),
        name="diff_attention",
    )(qt, k, vt, km, vmt, tt, bm, cst, buf)


def _tri_masks(rev):
    i = lax.broadcasted_iota(jnp.int32, (CHUNK, CHUNK), 0)
    j = lax.broadcasted_iota(jnp.int32, (CHUNK, CHUNK), 1)
    if rev:
        return j >= i, j > i, i, j
    return j <= i, j < i, i, j


def _cum_mat_t(rev):
    t = lax.broadcasted_iota(jnp.int32, (CHUNK, CHUNK), 0)
    j = lax.broadcasted_iota(jnp.int32, (CHUNK, CHUNK), 1)
    return jnp.where((t >= j) if rev else (t <= j), 1.0, 0.0)


def _gdn_kernel(qf_ref, kf_ref, vf_ref, cf_ref, rf_ref, qb_ref, kb_ref, vb_ref, cb_ref, rb_ref,
                s0f_ref, s0b_ref, bf_buf, bb_buf, of_ref, ob_ref, sff_ref, sfb_ref, s_scr):
    del bf_buf, bb_buf
    c = pl.program_id(2)
    nv = 2 * A_HG

    @pl.when(c == 0)
    def _():
        s_scr[0] = s0f_ref[0]
        s_scr[1] = s0b_ref[0]

    streams = ((qf_ref, kf_ref, vf_ref, cf_ref, rf_ref, False), (qb_ref, kb_ref, vb_ref, cb_ref, rb_ref, True))
    units = [(d, e) for d in range(2) for e in range(nv)]
    eye = jnp.where(_tri_masks(False)[0] & _tri_masks(True)[0], 1.0, 0.0)
    q, k, kk, qk, gc_all, gr_all, col, masks = [], [], [], [], [], [], [], []
    for q_ref, k_ref, _, c_ref, r_ref, rev in streams:
        tri, strict, _, _ = _tri_masks(rev)
        masks.append((tri, strict))
        qd = [q_ref[:, j * A_DK:(j + 1) * A_DK] for j in range(A_HG)]
        kd = [k_ref[:, j * A_DK:(j + 1) * A_DK] for j in range(A_HG)]
        q.append(qd)
        k.append(kd)
        kk.append([_dot_nt(x, x) for x in kd])
        qk.append([_dot_nt(x, y) for x, y in zip(qd, kd)])
        cv = c_ref[...]
        col.append(cv)
        gc_all.append(_dot_exact_lhs(jnp.where(tri, 1.0, 0.0), cv))
        gr_all.append(_dot_exact_rhs(r_ref[0, 0], _cum_mat_t(rev)))
    gcb = [jnp.broadcast_to(gc_all[d][:, e:e + 1], (CHUNK, LANES)) for d, e in units]
    beta = [jnp.broadcast_to(col[d][:, nv + e:nv + e + 1], (CHUNK, LANES)) for d, e in units]
    decay = [jnp.exp(jnp.where(masks[d][0], gcb[u][:, :CHUNK] - gr_all[d][e:e + 1, :], -jnp.inf))
             for u, (d, e) in enumerate(units)]
    a = [jnp.where(masks[d][1], beta[u][:, :CHUNK] * kk[d][e // 2] * decay[u], 0.0)
         for u, (d, e) in enumerate(units)]
    x = [eye - au for au in a]
    p = a
    for _ in range(5):
        p = [_dot(pu, pu) for pu in p]
        x = [xu + _dot(xu, pu) for xu, pu in zip(x, p)]
    egc = [jnp.exp(g) for g in gcb]
    v = [streams[d][2][:, e * A_DV:(e + 1) * A_DV] for d, e in units]
    uw = [_dot(x[u], jnp.concatenate([v[u] * beta[u], k[d][e // 2] * (beta[u] * egc[u])], axis=1))
          for u, (d, e) in enumerate(units)]
    s = [s_scr[d, e] for d, e in units]
    r2 = [_dot(jnp.concatenate([uw[u][:, A_DV:], q[d][e // 2] * egc[u]], axis=0), s[u])
          for u, (d, e) in enumerate(units)]
    v_new = [uw[u][:, :A_DV] - r2[u][:CHUNK] for u in range(len(units))]
    o = [r2[u][CHUNK:] + _dot(qk[d][e // 2] * decay[u], v_new[u]) for u, (d, e) in enumerate(units)]
    gtot = [gcb[u][0:1, :] if d else gcb[u][CHUNK - 1:CHUNK, :] for u, (d, e) in enumerate(units)]
    s_new = [s[u] * jnp.exp(gtot[u]) + _dot_tn(k[d][e // 2] * jnp.exp(gtot[u] - gcb[u]), v_new[u])
             for u, (d, e) in enumerate(units)]
    for u, (d, e) in enumerate(units):
        (ob_ref if d else of_ref)[:, e * A_DV:(e + 1) * A_DV] = o[u]
        s_scr[d, e] = s_new[u]

    @pl.when(c == pl.num_programs(2) - 1)
    def _():
        sff_ref[0] = s_scr[0]
        sfb_ref[0] = s_scr[1]


def _gdn_scan(qkv, colf, colb, rowf, rowb, s0f, s0b, buf_f, buf_b, n_seq, n_tok, row0):
    nc = n_tok // CHUNK
    cb0 = row0 // CHUNK
    ng = A_QK_HEADS // A_HG
    nv = 2 * A_HG
    fwd = lambda s, j, c: cb0 + s * nc + c
    bwd = lambda s, j, c: cb0 + s * nc + (nc - 1 - c)
    k_blk, v_blk = A_QK // (A_HG * A_DK), 2 * A_QK // (nv * A_DV)

    def stream_specs(ix):
        return [pl.BlockSpec((CHUNK, A_HG * A_DK), lambda s, j, c: (ix(s, j, c), j)),
                pl.BlockSpec((CHUNK, A_HG * A_DK), lambda s, j, c: (ix(s, j, c), k_blk + j)),
                pl.BlockSpec((CHUNK, nv * A_DV), lambda s, j, c: (ix(s, j, c), v_blk + j)),
                pl.BlockSpec((CHUNK, LANES), lambda s, j, c: (ix(s, j, c), j)),
                pl.BlockSpec((1, 1, nv, CHUNK), lambda s, j, c: (ix(s, j, c), j, 0, 0))]

    st_spec = pl.BlockSpec((1, nv, A_DK, A_DV), lambda s, j, c: (s, j, 0, 0))
    st_shape = jax.ShapeDtypeStruct((n_seq, A_V_HEADS, A_DK, A_DV), F32)
    any_spec = pl.BlockSpec(memory_space=pl.ANY)
    return pl.pallas_call(
        _gdn_kernel,
        grid=(n_seq, ng, nc),
        in_specs=stream_specs(fwd) + stream_specs(bwd) + [st_spec, st_spec, any_spec, any_spec],
        out_specs=[pl.BlockSpec((CHUNK, nv * A_DV), lambda s, j, c: (fwd(s, j, c), j)),
                   pl.BlockSpec((CHUNK, nv * A_DV), lambda s, j, c: (bwd(s, j, c), j)),
                   st_spec, st_spec],
        out_shape=[jax.ShapeDtypeStruct(buf_f.shape, F32), jax.ShapeDtypeStruct(buf_b.shape, F32),
                   st_shape, st_shape],
        scratch_shapes=[pltpu.VMEM((2, nv, A_DK, A_DV), F32)],
        input_output_aliases={12: 0, 13: 1},
        compiler_params=_cparams(("parallel", "parallel", "arbitrary")),
        name="gdn_scan",
    )(qkv, qkv, qkv, colf, rowf, qkv, qkv, qkv, colb, rowb, s0f, s0b, buf_f, buf_b)


GLA_LEVELS = (5, 4, 3, 2, 1, 0)
GLA_SAFE_EXP = 60.0


def _gla_level(i, j, lv, rev):
    sup = lv + 1
    base = jnp.left_shift(jnp.right_shift(i, sup), sup)
    ref_row = base + (1 << lv) - (0 if rev else 1)
    same = jnp.right_shift(i, sup) == jnp.right_shift(j, sup)
    i_late = jnp.bitwise_and(jnp.right_shift(i, lv), 1) == (0 if rev else 1)
    j_early = jnp.bitwise_and(jnp.right_shift(j, lv), 1) == (1 if rev else 0)
    return jnp.where(j == ref_row, 1.0, 0.0), same & i_late & j_early


def _rows_bcast(x, rows, span):
    zero = jnp.zeros((1, x.shape[1]), x.dtype)
    picked = [x[r:r + 1] if 0 <= r < x.shape[0] else zero for r in rows]
    return jnp.concatenate([jnp.broadcast_to(r, (span, x.shape[1])) for r in picked], axis=0)


def _gla_block16(i, j, rev):
    blk = jnp.right_shift(i, 4)
    ref_row = jnp.left_shift(blk + 1, 4) if rev else jnp.left_shift(blk, 4) - 1
    tri = (j >= i) if rev else (j <= i)
    return jnp.where(j == ref_row, 1.0, 0.0), tri & (blk == jnp.right_shift(j, 4))


def _gla_pair(q, k, bc, ref, key_side_nonpositive):
    ek = ref - bc
    if key_side_nonpositive:
        ek = jnp.minimum(ek, 0.0)
    return (q * jnp.exp(jnp.minimum(bc - ref, 0.0))).astype(BF16), (k * jnp.exp(ek)).astype(BF16)


def _gla_kernel(qf_ref, kf_ref, vf_ref, lf_ref, qb_ref, kb_ref, vb_ref, lb_ref, w2_ref, gb_ref,
                s0f_ref, s0b_ref, bf_buf, bb_buf, of_ref, ob_ref, sff_ref, sfb_ref, s_scr, a_scr):
    del bf_buf, bb_buf
    c = pl.program_id(1)

    @pl.when(c == 0)
    def _():
        s_scr[0] = s0f_ref[0]
        s_scr[1] = s0b_ref[0]

    streams = ((qf_ref, kf_ref, vf_ref, lf_ref, False), (qb_ref, kb_ref, vb_ref, lb_ref, True))
    heads = [slice(hd * D_DK, (hd + 1) * D_DK) for hd in range(D_HEADS)]
    units = [(d, hd) for d in range(2) for hd in range(D_HEADS)]
    ops = []
    for d, (q_ref, k_ref, v_ref, l_ref, rev) in enumerate(streams):
        tri, _, i, j = _tri_masks(rev)
        x = _dot(l_ref[...], w2_ref[d]) + gb_ref[d]
        lg = (jnp.minimum(x, 0.0) - jnp.log(1.0 + jnp.exp(-jnp.abs(x)))) * (1.0 / GATE_TAU)
        bc = _dot_exact_lhs(jnp.where(tri, 1.0, 0.0), lg)
        btot = bc[0:1] if rev else bc[CHUNK - 1:CHUNK]
        q, k = q_ref[...] * D_DK ** -0.5, k_ref[...]
        def level(lv, q=q, k=k, bc=bc, i=i, j=j, rev=rev):
            pick, mask = _gla_level(i, j, lv, rev)
            if (1 << lv) >= 8:
                span = 2 << lv
                ref = _rows_bcast(bc, [b + (1 << lv) - (0 if rev else 1) for b in range(0, CHUNK, span)], span)
            else:
                ref = _dot_exact_lhs(pick, bc)
            return (mask,) + _gla_pair(q, k, bc, ref, True)

        _, mask16 = _gla_block16(i, j, rev)
        ref16 = _rows_bcast(bc, [b + 16 if rev else b - 1 for b in range(0, CHUNK, 16)], 16)
        ops.append(dict(level=level, coarse=[level(lv) for lv in GLA_LEVELS[:2]],
                        block16=(mask16,) + _gla_pair(q, k, bc, ref16, False), risk=jnp.max(ref16 - bc),
                        diag=(i == j, q.astype(BF16), k.astype(BF16)),
                        qg=(q * jnp.exp(bc)).astype(BF16), kd=(k * jnp.exp(btot - bc)).astype(BF16),
                        etot=jnp.exp(btot), v=v_ref[...].astype(BF16)))

    def term(pair, hd):
        mask, qe, ke = pair
        return jnp.where(mask, _dot_nt(qe[:, heads[hd]], ke[:, heads[hd]]), 0.0)

    a_coarse = [term(ops[d]['coarse'][0], hd) + term(ops[d]['coarse'][1], hd) for d, hd in units]
    risk = jnp.maximum(ops[0]['risk'], ops[1]['risk'])

    @pl.when(risk <= GLA_SAFE_EXP)
    def _():
        for u, (d, hd) in enumerate(units):
            a_scr[u] = a_coarse[u] + term(ops[d]['block16'], hd)

    @pl.when(risk > GLA_SAFE_EXP)
    def _():
        fine = [[ops[d]['diag']] + [ops[d]['level'](lv) for lv in GLA_LEVELS[2:]] for d in range(2)]
        for u, (d, hd) in enumerate(units):
            a_scr[u] = a_coarse[u] + functools.reduce(lambda x, y: x + y, [term(pr, hd) for pr in fine[d]])

    a = [a_scr[u] for u in range(len(units))]
    st = [s_scr[d, hd] for d, hd in units]
    vs = [ops[d]['v'][:, hd * D_DV:(hd + 1) * D_DV] for d, hd in units]
    o = [_dot_nt(ops[d]['qg'][:, heads[hd]], st[u]) + _dot(a[u], vs[u]) for u, (d, hd) in enumerate(units)]
    st_new = [st[u] * ops[d]['etot'][:, heads[hd]] + _dot_tn(vs[u], ops[d]['kd'][:, heads[hd]])
              for u, (d, hd) in enumerate(units)]
    for u, (d, hd) in enumerate(units):
        (ob_ref if d else of_ref)[:, hd * D_DV:(hd + 1) * D_DV] = o[u]
        s_scr[d, hd] = st_new[u]

    @pl.when(c == pl.num_programs(1) - 1)
    def _():
        sff_ref[0] = s_scr[0]
        sfb_ref[0] = s_scr[1]


def _gla_scan(p, lr, w2, gb, s0f, s0b, buf_f, buf_b, n_seq, n_tok, row0):
    nc = n_tok // CHUNK
    cb0 = row0 // CHUNK
    fwd = lambda s, c: cb0 + s * nc + c
    bwd = lambda s, c: cb0 + s * nc + (nc - 1 - c)

    def stream_specs(ix):
        return [pl.BlockSpec((CHUNK, D_QKW), lambda s, c: (ix(s, c), 0)),
                pl.BlockSpec((CHUNK, D_QKW), lambda s, c: (ix(s, c), 1)),
                pl.BlockSpec((CHUNK, D_VW), lambda s, c: (ix(s, c), 2 * D_QKW // D_VW)),
                pl.BlockSpec((CHUNK, 2 * GATE_RANK), lambda s, c: (ix(s, c), 0))]

    st_spec = pl.BlockSpec((1, D_HEADS, D_DV, D_DK), lambda s, c: (s, 0, 0, 0))
    st_shape = jax.ShapeDtypeStruct((n_seq, D_HEADS, D_DV, D_DK), F32)
    any_spec = pl.BlockSpec(memory_space=pl.ANY)
    return pl.pallas_call(
        _gla_kernel,
        grid=(n_seq, nc),
        in_specs=stream_specs(fwd) + stream_specs(bwd) + [
            pl.BlockSpec((2, 2 * GATE_RANK, D_QKW), lambda s, c: (0, 0, 0)),
            pl.BlockSpec((2, 1, D_QKW), lambda s, c: (0, 0, 0)),
            st_spec, st_spec, any_spec, any_spec],
        out_specs=[pl.BlockSpec((CHUNK, D_VW), lambda s, c: (fwd(s, c), 0)),
                   pl.BlockSpec((CHUNK, D_VW), lambda s, c: (bwd(s, c), 0)),
                   st_spec, st_spec],
        out_shape=[jax.ShapeDtypeStruct(buf_f.shape, F32), jax.ShapeDtypeStruct(buf_b.shape, F32),
                   st_shape, st_shape],
        scratch_shapes=[pltpu.VMEM((2, D_HEADS, D_DV, D_DK), F32), pltpu.VMEM((2 * D_HEADS, CHUNK, CHUNK), F32)],
        input_output_aliases={12: 0, 13: 1},
        compiler_params=_cparams(("parallel", "arbitrary")),
        name="gla_scan",
    )(p, p, p, lr, p, p, p, lr, w2, gb, s0f, s0b, buf_f, buf_b)


def _gdn_prep_kernel(x_ref, prev_ref, next_ref, edge_ref, w_ref, o_ref, *, starts, ends):
    i, j = pl.program_id(0), pl.program_id(1)
    tm, cw = x_ref.shape
    r0 = i * tm
    is_start = functools.reduce(jnp.logical_or, [r0 == s for s in starts])
    is_end = functools.reduce(jnp.logical_or, [r0 + tm == e for e in ends])
    x = x_ref[...]
    prev_row = jnp.where(is_start, edge_ref[0, 7:8, :], prev_ref[7:8, :])
    next_row = jnp.where(is_end, 0.0, next_ref[0:1, :])
    rows = lax.broadcasted_iota(jnp.int32, (tm, 1), 0)
    xp = jnp.where(rows == 0, prev_row, pltpu.roll(x, 1, axis=0))
    xn = jnp.where(rows == tm - 1, next_row, pltpu.roll(x, tm - 1, axis=0))
    y = xp * w_ref[0:1, :] + x * w_ref[1:2, :] + xn * w_ref[2:3, :]
    y = y * _sigmoid(y)
    is_qk = j < 2 * A_QK // cw
    q_scale = jnp.where(j < A_QK // cw, A_DK ** -0.5, 1.0)
    for c0 in range(0, cw, A_DK):
        yh = y[:, c0:c0 + A_DK]
        nrm = yh * (lax.rsqrt(jnp.sum(yh * yh, axis=-1, keepdims=True) + EPS) * q_scale)
        o_ref[:, c0:c0 + A_DK] = jnp.where(is_qk, nrm, yh)


def _gdn_prep(lay, x, edge, conv_w):
    r, c = x.shape
    tm, cw = min(1024, min(n for _, n, _, _ in lay.groups)), 512
    starts = [row + s * n for b, n, row, _ in lay.groups for s in range(b)]
    ends = [row + (s + 1) * n for b, n, row, _ in lay.groups for s in range(b)]
    seq_of = lambda i: sum((i * tm >= s).astype(jnp.int32) for s in starts) - 1
    w8 = jnp.pad(conv_w.astype(F32), ((0, 8 - conv_w.shape[0]), (0, 0)))
    return pl.pallas_call(
        functools.partial(_gdn_prep_kernel, starts=starts, ends=ends),
        grid=(r // tm, c // cw),
        in_specs=[pl.BlockSpec((tm, cw), lambda i, j: (i, j)),
                  pl.BlockSpec((8, cw), lambda i, j: (jnp.maximum(i * (tm // 8) - 1, 0), j)),
                  pl.BlockSpec((8, cw), lambda i, j: (jnp.minimum((i + 1) * (tm // 8), r // 8 - 1), j)),
                  pl.BlockSpec((1, 8, cw), lambda i, j: (seq_of(i), 0, j)),
                  pl.BlockSpec((8, cw), lambda i, j: (0, j))],
        out_specs=pl.BlockSpec((tm, cw), lambda i, j: (i, j)),
        out_shape=jax.ShapeDtypeStruct((r, c), F32),
        compiler_params=_cparams(("parallel", "parallel")),
        name="gdn_prep",
    )(x, x, x, edge, w8)


def _proj_prep_kernel(*refs, head_norm, rope, transpose):
    x_ref, g_ref, w_ref = refs[:3]
    rest = list(refs[3:])
    hg_ref = rest.pop(0) if head_norm else None
    cos_ref, sin_ref = (rest.pop(0), rest.pop(0)) if rope else (None, None)
    o_ref, = rest
    x = x_ref[...]
    xn = (x * lax.rsqrt(jnp.mean(x * x, axis=-1, keepdims=True) + EPS) * g_ref[...]).astype(BF16)
    acc = jnp.dot(xn, w_ref[...], preferred_element_type=F32)
    lo = lax.broadcasted_iota(jnp.int32, (1, LANES), 1) < C_DQK
    for c0 in range(0, acc.shape[1], LANES):
        y = acc[:, c0:c0 + LANES]
        if head_norm == "head":
            y = y * lax.rsqrt(jnp.mean(y * y, axis=-1, keepdims=True) + EPS) * hg_ref[...]
        elif head_norm == "maps":
            sq = y * y
            s_lo = jnp.sum(jnp.where(lo, sq, 0.0), axis=-1, keepdims=True)
            s_hi = jnp.sum(jnp.where(lo, 0.0, sq), axis=-1, keepdims=True)
            inv = jnp.where(lo, lax.rsqrt(s_lo * (1.0 / C_DQK) + EPS), lax.rsqrt(s_hi * (1.0 / C_DQK) + EPS))
            y = y * inv * hg_ref[...]
        if rope:
            y = y * cos_ref[...] + pltpu.roll(y, LANES // 2, axis=1) * sin_ref[...]
        if transpose:
            o_ref[c0:c0 + LANES, :] = y.T.astype(BF16)
        else:
            o_ref[:, c0:c0 + LANES] = y.astype(BF16)


def _pos_block(lay, tm):
    def pos(i):
        r = i * tm
        out = 0
        for b, n, row, _ in lay.groups:
            inside = (r >= row) & (r < row + b * n)
            out = out + jnp.where(inside, ((r - row) % n) // tm, 0)
        return out
    return pos


def _proj_prep(lay, x, g, w, head_gain=None, head_norm=None, rope=None, transpose=False):
    m, k = x.shape
    n = w.shape[1]
    tm = min(512, min(nt for _, nt, _, _ in lay.groups))
    tn = min(n, 1024)
    pos = _pos_block(lay, tm)
    fixed = lambda j, i: (0, 0)
    ins = [x, g.reshape(1, k).astype(F32), w.astype(BF16)]
    specs = [pl.BlockSpec((tm, k), lambda j, i: (i, 0)), pl.BlockSpec((1, k), fixed),
             pl.BlockSpec((k, tn), lambda j, i: (0, j))]
    if head_norm:
        ins.append(head_gain.reshape(1, LANES).astype(F32))
        specs.append(pl.BlockSpec((1, LANES), fixed))
    if rope:
        ins += list(rope)
        specs += [pl.BlockSpec((tm, LANES), lambda j, i: (pos(i), 0))] * 2
    if transpose:
        out_spec, out_shape = pl.BlockSpec((tn, tm), lambda j, i: (j, i)), (n, m)
    else:
        out_spec, out_shape = pl.BlockSpec((tm, tn), lambda j, i: (i, j)), (m, n)
    return pl.pallas_call(
        functools.partial(_proj_prep_kernel, head_norm=head_norm, rope=bool(rope), transpose=transpose),
        grid=(n // tn, m // tm),
        in_specs=specs, out_specs=out_spec,
        out_shape=jax.ShapeDtypeStruct(out_shape, BF16),
        compiler_params=_cparams(("parallel", "parallel")),
        name="proj_prep",
    )(*ins)


class _Layout:
    def __init__(self, shapes):
        self.groups = []
        row, seq = 0, 0
        for b, n in shapes:
            self.groups.append((b, n, row, seq))
            row += b * n
            seq += b
        self.rows, self.n_seq = row, seq

    def to_seqs(self, main, meta):
        out = []
        c = main.shape[-1]
        for b, n, row, seq in self.groups:
            m = meta[seq * N_META:(seq + b) * N_META].reshape(b, N_META, c)
            x = main[row:row + b * n].reshape(b, n, c)
            out.append(jnp.concatenate([m, x], axis=1))
        return out

    def from_seqs(self, seqs):
        main = jnp.concatenate([s[:, N_META:].reshape(-1, s.shape[-1]) for s in seqs], axis=0)
        meta = jnp.concatenate([s[:, :N_META].reshape(-1, s.shape[-1]) for s in seqs], axis=0)
        return main, meta


def _meta_chunk(meta):
    c = meta.shape[-1]
    m = meta.reshape(-1, N_META, c)
    return jnp.pad(m, ((0, 0), (CHUNK - N_META, 0), (0, 0))).reshape(-1, c)


def _meta_unchunk(x):
    c = x.shape[-1]
    return x.reshape(-1, CHUNK, c)[:, CHUNK - N_META:].reshape(-1, c)


def _map_seqs(lay, fn, main, meta):
    return lay.from_seqs([fn(s) for s in lay.to_seqs(main, meta)])


def _run_scan(scan, lay, main_args, meta_args, state_shape, width):
    s_all = lay.n_seq
    zeros = jnp.zeros((s_all,) + state_shape, F32)
    mrows = s_all * CHUNK
    mbuf = jnp.zeros((mrows, width), F32)
    _, _, s_meta, _ = scan(*meta_args, zeros, zeros, mbuf, mbuf, s_all, CHUNK, 0)
    of = jnp.zeros((lay.rows, width), F32)
    ob = jnp.zeros((lay.rows, width), F32)
    finals = []
    for b, n, row, seq in lay.groups:
        of, ob, _, sfb = scan(*main_args, s_meta[seq:seq + b], zeros[seq:seq + b], of, ob, b, n, row)
        finals.append(sfb)
    mof, mob, _, _ = scan(*meta_args, zeros, jnp.concatenate(finals, axis=0), mbuf, mbuf, s_all, CHUNK, 0)
    return (of, ob), (_meta_unchunk(mof), _meta_unchunk(mob))


def _gdn_layer(lay, h, hm, norm_g, w_in, conv_w, a_log, dt_bias, out_g, w_out):
    w_qkv, w_z, w_ab = (w_in[:, :A_CONV_CH], w_in[:, A_CONV_CH:A_CONV_CH + A_VW], w_in[:, A_CONV_CH + A_VW:])
    x, xm = _norm_proj(h, norm_g, w_qkv), _norm_proj(hm, norm_g, w_qkv)
    z, zm = _norm_proj(h, norm_g, w_z), _norm_proj(hm, norm_g, w_z)
    ab, abm = _norm_proj(h, norm_g, w_ab), _norm_proj(hm, norm_g, w_ab)
    s_all = lay.n_seq
    xm3 = xm.reshape(s_all, N_META, A_CONV_CH)
    qkv = _gdn_prep(lay, x, xm3[:, N_META - 8:], conv_w)

    def prep(x):
        prev = jnp.pad(x[:, :-1], ((0, 0), (1, 0), (0, 0)))
        nxt = jnp.pad(x[:, 1:], ((0, 0), (0, 1), (0, 0)))
        y = jax.nn.silu(prev * conv_w[0] + x * conv_w[1] + nxt * conv_w[2])
        b, l, _ = y.shape
        qk = y[..., :2 * A_QK].reshape(b, l, 2 * A_QK_HEADS, A_DK)
        qk = qk * lax.rsqrt(jnp.sum(qk * qk, axis=-1, keepdims=True) + EPS)
        qk = qk.reshape(b, l, 2, A_QK) * jnp.array([A_DK ** -0.5, 1.0], F32)[:, None]
        return jnp.concatenate([qk.reshape(b, l, 2 * A_QK), y[..., 2 * A_QK:]], axis=-1)

    first = x[np.array([row + s * n for b, n, row, _ in lay.groups for s in range(b)])]
    qkvm = prep(jnp.concatenate([xm3, first[:, None]], axis=1))[:, :N_META].reshape(-1, A_CONV_CH)

    def gates(x):
        x = x.reshape(-1, 2, 2, A_V_HEADS)
        g = -jnp.exp(a_log.astype(F32)) * jax.nn.softplus(x[:, 0] + dt_bias.astype(F32))
        return g, jax.nn.sigmoid(x[:, 1])

    ng, nv = A_QK_HEADS // A_HG, 2 * A_HG

    def pack(g, beta, d):
        r = g.shape[0]
        gd, bd = g[:, d].reshape(r, ng, nv), beta[:, d].reshape(r, ng, nv)
        col = jnp.concatenate([gd, bd, jnp.zeros((r, ng, LANES - 2 * nv), F32)], axis=-1).reshape(r, ng * LANES)
        row = gd.reshape(r // CHUNK, CHUNK, ng, nv).transpose(0, 2, 3, 1)
        return col, row

    def scan_args(qkv_rows, g, beta):
        colf, rowf = pack(g, beta, 0)
        colb, rowb = pack(g, beta, 1)
        return (qkv_rows, colf, colb, rowf, rowb)

    g, beta = gates(ab)
    gm, betam = gates(abm)
    pad_gate = lambda t: _meta_chunk(t.reshape(-1, 2 * A_V_HEADS)).reshape(-1, 2, A_V_HEADS)
    (of, ob), (mof, mob) = _run_scan(
        _gdn_scan, lay, scan_args(qkv, g, beta), scan_args(_meta_chunk(qkvm), pad_gate(gm), pad_gate(betam)),
        (A_V_HEADS, A_DK, A_DV), A_VW)
    gain = jnp.tile(out_g.astype(F32), A_V_HEADS)
    return (_out_proj([of, ob], z, gain, w_out, h, A_DV), _out_proj([mof, mob], zm, gain, w_out, hm, A_DV))


def _rope_tables(n_tok):
    rows = n_tok // GRID_W
    row = jnp.repeat(jnp.arange(rows, dtype=F32), GRID_W)
    col = (jnp.arange(rows * GRID_W) % GRID_W).astype(F32)
    inv = ROPE_THETA ** (-jnp.arange(0, ROPE_AXIS, 2, dtype=F32) / ROPE_AXIS)
    ang = jnp.concatenate([row[:, None] * inv, col[:, None] * inv], axis=-1)
    return jnp.cos(ang), jnp.sin(ang)


def _pad_meta(x, n_seq):
    return jnp.pad(x.reshape(n_seq, N_META, -1), ((0, 0), (0, LANES - N_META), (0, 0)))


def _gqa_layer(lay, h, hm, norm_g, w_in, q_g, k_g, w_out):
    k_end, kv_end = B_QW + B_KVW, B_QW + 2 * B_KVW
    z = _norm_proj(h, norm_g, w_in[:, kv_end:])
    pm = _norm_proj(hm, norm_g, w_in)

    def norm_heads(x, g):
        r = x.shape[0]
        x = x.reshape(r, -1, B_DH)
        return x * lax.rsqrt(jnp.mean(x * x, axis=-1, keepdims=True) + EPS) * g.astype(F32)

    scale = B_DH ** -0.5 * LOG2E
    cos, sin = _rope_tables(max(n for _, n, _, _ in lay.groups))
    rope = (jnp.concatenate([cos, cos], axis=-1), jnp.concatenate([-sin, sin], axis=-1))
    qt = _proj_prep(lay, h, norm_g, w_in[:, :B_QW], q_g.astype(F32) * scale, "head", rope, transpose=True)
    k = _proj_prep(lay, h, norm_g, w_in[:, B_QW:k_end], k_g, "head", rope)
    vt = _proj_prep(lay, h, norm_g, w_in[:, k_end:kv_end], transpose=True)
    s_all = lay.n_seq
    qm = (norm_heads(pm[:, :B_QW], q_g) * scale).reshape(-1, B_QW).astype(BF16)
    km = norm_heads(pm[:, B_QW:B_QW + B_KVW], k_g).reshape(-1, B_KVW).astype(BF16)
    vm = pm[:, B_QW + B_KVW:B_QW + 2 * B_KVW].astype(BF16)
    zm = pm[:, B_QW + 2 * B_KVW:]
    qmt = _pad_meta(qm, s_all).reshape(s_all * LANES, B_QW).T
    kmp, vmt = _pad_meta(km, s_all), _pad_meta(vm, s_all).transpose(0, 2, 1)

    o = jnp.zeros((lay.rows, B_QW), F32)
    om = jnp.zeros((s_all * LANES, B_QW), F32)
    for b, n, row, seq in lay.groups:
        tq, tk = min(GQA_TQ, n), min(GQA_TK, n)
        o = _gqa_attention(qt, k, vt, kmp[seq:seq + b], vmt[seq:seq + b], o, b, n, n, row, row, tq, tk)
        om = _gqa_attention(qmt, k, vt, kmp[seq:seq + b], vmt[seq:seq + b], om, b, LANES, n,
                            seq * LANES, row, LANES, min(META_TK, n))
    om = om.reshape(s_all, LANES, B_QW)[:, :N_META].reshape(-1, B_QW)
    ones = jnp.ones((B_QW,), F32)
    return _out_proj([o], z, ones, w_out, h, 0), _out_proj([om], zm, ones, w_out, hm, 0)


def _t5_bucket_np(rel):
    nb = REL_BUCKETS // 2
    max_exact = nb // 2
    n = np.abs(rel)
    steps = nb - max_exact
    large = np.full(n.shape, max_exact, np.int64)
    for kstep in range(1, steps + 1):
        large += (n.astype(np.float64) ** steps >= (max_exact ** steps) * float(REL_MAX_DIST // max_exact) ** kstep)
    large = np.minimum(large, nb - 1)
    return np.where(rel > 0, nb, 0) + np.where(n < max_exact, n, large)


def _diff_layer(lay, h, hm, norm_g, w_in, q_g, k_g, lam, sub_g, w_out, rel_bias, layer_idx):
    n_qkv = 2 * C_QKW + C_VW
    z = _norm_proj(h, norm_g, w_in[:, n_qkv:])
    pm = _norm_proj(hm, norm_g, w_in)

    def norm_maps(x, g):
        r = x.shape[0]
        x = x.reshape(r, -1, C_DQK)
        return (x * lax.rsqrt(jnp.mean(x * x, axis=-1, keepdims=True) + EPS) * g.astype(F32)).reshape(r, -1)

    s_all = lay.n_seq
    scale = C_DQK ** -0.5 * LOG2E
    qt = _proj_prep(lay, h, norm_g, w_in[:, :C_QKW], jnp.tile(q_g.astype(F32), 2) * scale, "maps", transpose=True)
    k = _proj_prep(lay, h, norm_g, w_in[:, C_QKW:2 * C_QKW], jnp.tile(k_g.astype(F32), 2), "maps")
    vt = _proj_prep(lay, h, norm_g, w_in[:, 2 * C_QKW:n_qkv], transpose=True)
    qm = (norm_maps(pm[:, :C_QKW], q_g) * scale).astype(BF16)
    km = norm_maps(pm[:, C_QKW:2 * C_QKW], k_g).astype(BF16)
    vm = pm[:, 2 * C_QKW:n_qkv].astype(BF16)
    zm = pm[:, n_qkv:]
    qmt = _pad_meta(qm, s_all).reshape(s_all * LANES, C_QKW).T
    kmp, vmt = _pad_meta(km, s_all), _pad_meta(vm, s_all).transpose(0, 2, 1)

    lam = lam.astype(F32)
    lam_init = 0.8 - 0.6 * math.exp(-0.3 * layer_idx)
    lam_full = jnp.exp(jnp.sum(lam[0] * lam[1])) - jnp.exp(jnp.sum(lam[2] * lam[3])) + lam_init
    table = rel_bias.astype(F32).T * LOG2E
    nb = REL_BUCKETS // 2

    r0 = (BIAS_TILES // 2) * LANES + N_META + LANES
    by_rel = table[:, _t5_bucket_np(np.arange(-r0, r0 + 1))]

    def toeplitz(shift):
        v = by_rel[:, r0 + shift - (LANES - 1):r0 + shift + LANES]
        w = jnp.pad(v, ((0, 0), (0, 1)))
        a = jnp.tile(w, (1, LANES))[:, :LANES * (2 * LANES - 1)].reshape(C_HEADS, LANES, 2 * LANES - 1)
        return a[:, :, LANES - 1:].transpose(0, 2, 1)

    cst = jnp.zeros((C_HEADS, 8, LANES), F32)
    cst = cst.at[:, 0].set(table[:, nb - 1:nb]).at[:, 1].set(table[:, 2 * nb - 1:2 * nb]).at[:, 2].set(lam_full)
    ar = np.arange(LANES)
    key_pad = jnp.asarray(np.where(ar < N_META, 0.0, NEG), F32)[:, None]
    offsets = range(-(BIAS_TILES // 2), BIAS_TILES // 2 + 1)
    tt = jnp.stack([toeplitz(o * LANES) for o in offsets], axis=1)
    tt_m = jnp.stack([toeplitz(o * LANES + N_META) for o in offsets], axis=1)
    meta_rows = (ar < N_META)[:, None]
    bm_m = (jnp.where(meta_rows, toeplitz(0), 0.0) + key_pad)[:, None]
    near_tile = toeplitz(-N_META)[:, :N_META]

    o = jnp.zeros((lay.rows, C_VW), F32)
    om = jnp.zeros((s_all * LANES, C_VW), F32)
    for b, n, row, seq in lay.groups:
        t, tk = min(DIFF_TQ, n), min(DIFF_TK, n)
        far = jnp.broadcast_to(table[:, nb - 1][:, None, None], (C_HEADS, N_META, t))
        near = jnp.concatenate([near_tile, far[:, :, LANES:]], axis=2)
        pad_rows = jnp.full((C_HEADS, 2, LANES - N_META, t), NEG, F32)
        bm = jnp.concatenate([jnp.stack([near, far], axis=1), pad_rows], axis=2)
        o = _diff_attention(qt, k, vt, kmp[seq:seq + b], vmt[seq:seq + b], tt, bm, cst, o,
                            b, n, n, row, row, t, tk, False)
        om = _diff_attention(qmt, k, vt, kmp[seq:seq + b], vmt[seq:seq + b], tt_m, bm_m, cst, om,
                             b, LANES, n, seq * LANES, row, LANES, min(META_TK, n), True)
    om = om.reshape(s_all, LANES, C_VW)[:, :N_META].reshape(-1, C_VW)
    gain = jnp.tile(sub_g.astype(F32), C_HEADS) * (1.0 - lam_init)
    return _out_proj([o], z, gain, w_out, h, C_DV), _out_proj([om], zm, gain, w_out, hm, C_DV)


def _gla_layer(lay, h, hm, norm_g, w_in, gate_w2, gate_b, out_g, w_out):
    n_qkv = 2 * D_QKW + D_VW
    w_qkv, w_z, w_lr = w_in[:, :n_qkv], w_in[:, n_qkv:n_qkv + D_VW], w_in[:, n_qkv + D_VW:]
    p, pm = _norm_proj(h, norm_g, w_qkv), _norm_proj(hm, norm_g, w_qkv)
    z, zm = _norm_proj(h, norm_g, w_z), _norm_proj(hm, norm_g, w_z)
    lr, lrm = _norm_proj(h, norm_g, w_lr), _norm_proj(hm, norm_g, w_lr)
    zero = jnp.zeros((GATE_RANK, D_QKW), F32)
    w2 = jnp.stack([jnp.concatenate([gate_w2[0].astype(F32), zero], axis=0),
                    jnp.concatenate([zero, gate_w2[1].astype(F32)], axis=0)])
    gb = gate_b.astype(F32).reshape(2, 1, D_QKW)
    (of, ob), (mof, mob) = _run_scan(
        _gla_scan, lay, (p, lr, w2, gb), (_meta_chunk(pm), _meta_chunk(lrm), w2, gb), (D_HEADS, D_DV, D_DK), D_VW)
    gain = jnp.tile(out_g.astype(F32), D_HEADS)
    return _out_proj([of, ob], z, gain, w_out, h, D_DV), _out_proj([mof, mob], zm, gain, w_out, hm, D_DV)


def _trunk(xs, meta_tokens, rel_bias, a, b, c, d, depth):
    lay = _Layout([(x.shape[0], x.shape[1]) for x in xs])
    h = jnp.concatenate([x.reshape(-1, D_MODEL) for x in xs], axis=0)
    hm = jnp.tile(meta_tokens.astype(F32), (lay.n_seq, 1))
    for i in range(depth):
        m, j = i % 4, i // 4
        if m == 0:
            h, hm = _gdn_layer(lay, h, hm, *(t[j] for t in a))
        elif m == 1:
            h, hm = _gqa_layer(lay, h, hm, *(t[j] for t in b))
        elif m == 2:
            h, hm = _diff_layer(lay, h, hm, *(t[j] for t in c), rel_bias, i)
        else:
            h, hm = _gla_layer(lay, h, hm, *(t[j] for t in d))
    return [h[row:row + bsz * n].reshape(bsz, n, D_MODEL) for bsz, n, row, _ in lay.groups]


def kernel(x_prompt, x_sample, meta_tokens, rel_bias, a_norm, a_w_in, a_conv, a_a_log, a_dt_bias, a_out_norm, a_w_out, b_norm, b_w_in, b_q_norm, b_k_norm, b_w_out, c_norm, c_w_in, c_q_norm, c_k_norm, c_lambda, c_sub_norm, c_w_out, d_norm, d_w_in, d_gate_w2, d_gate_b, d_out_norm, d_w_out):
    y_prompt, y_sample = _trunk(
        [x_prompt, x_sample], meta_tokens, rel_bias,
        (a_norm, a_w_in, a_conv, a_a_log, a_dt_bias, a_out_norm, a_w_out),
        (b_norm, b_w_in, b_q_norm, b_k_norm, b_w_out),
        (c_norm, c_w_in, c_q_norm, c_k_norm, c_lambda, c_sub_norm, c_w_out),
        (d_norm, d_w_in, d_gate_w2, d_gate_b, d_out_norm, d_w_out), 4)
    return (y_prompt, y_sample)
```
